```python
import math
import jax, jax.numpy as jnp
from jax import lax
import numpy as np

D_MODEL = 2048
BATCH = 1
SEQ = 8192
DEPTH = 4

CHUNK = 64
QBLOCK = 128
ATT_WIDTH = D_MODEL // 2
DIFF_HEAD_DIM = 64
DIFF_V_DIM = 2 * DIFF_HEAD_DIM
DIFF_HEADS = ATT_WIDTH // DIFF_V_DIM
SSM_WIDTH = D_MODEL // 2
SSM_HEAD_DIM = 64
SSM_HEADS = SSM_WIDTH // SSM_HEAD_DIM
SSM_GROUPS = 2
SSM_STATE = 128
CONV_WIDTH = 4
XBC_WIDTH = SSM_WIDTH + 2 * SSM_GROUPS * SSM_STATE
EVEN_IN = 4 * ATT_WIDTH + SSM_WIDTH + XBC_WIDTH + SSM_HEADS
POOL_WINDOWS = (2, 4, 8, 16)
POOL_WIDTH = D_MODEL
POOL_NGROUPS = len(POOL_WINDOWS)
POOL_GROUP = POOL_WIDTH // POOL_NGROUPS
N_EVEN = (DEPTH + 1) // 2
N_ODD = DEPTH // 2
DEEPNORM_ALPHA = (2.0 * DEPTH) ** 0.25
DEEPNORM_BETA = (8.0 * DEPTH) ** -0.25
LN_EPS = 1e-5
RMS_EPS = 1e-5

kernel_name = "hybrid_diffattn_ssd_pool_deepnorm"

F32 = jnp.float32


def layer_norm(x, g, b):
    xf = x.astype(F32)
    mu = jnp.mean(xf, axis=-1, keepdims=True)
    var = jnp.mean(jnp.square(xf - mu), axis=-1, keepdims=True)
    return ((xf - mu) * lax.rsqrt(var + LN_EPS) * g.astype(F32) + b.astype(F32)).astype(x.dtype)


def rms_norm(x, g=None):
    xf = x.astype(F32)
    y = xf * lax.rsqrt(jnp.mean(jnp.square(xf), axis=-1, keepdims=True) + RMS_EPS)
    if g is not None:
        y = y * g.astype(F32)
    return y


def segsum(a):
    t = a.shape[-1]
    ar = jnp.broadcast_to(a[..., :, None], a.shape + (t,))
    ar = jnp.where(jnp.tril(jnp.ones((t, t), bool), -1), ar, 0.0)
    cs = jnp.cumsum(ar, axis=-2)
    return jnp.where(jnp.tril(jnp.ones((t, t), bool), 0), cs, -jnp.inf)


def diff_attention(q, k, v, lam):
    s = q.shape[1]
    scale = DIFF_HEAD_DIM ** -0.5
    qf, kf, vf = q.astype(F32), k.astype(F32), v.astype(F32)
    outs = []
    for i in range(s // QBLOCK):
        qs, qe = i * QBLOCK, (i + 1) * QBLOCK
        sc = jnp.einsum('bqhmd,bkhmd->bhmqk', qf[:, qs:qe], kf[:, :qe]) * scale
        q_chunk = (qs + jnp.arange(QBLOCK)) // CHUNK
        k_chunk = jnp.arange(qe) // CHUNK
        sc = jnp.where(k_chunk[None, :] <= q_chunk[:, None], sc, -jnp.inf)
        p = jax.nn.softmax(sc, axis=-1)
        attn = p[:, :, 0] - lam * p[:, :, 1]
        outs.append(jnp.einsum('bhqk,bkhe->bqhe', attn, vf[:, :qe]))
    return jnp.concatenate(outs, axis=1)


def causal_depthwise_conv(u, w, bias):
    c = u.shape[-1]
    taps = w.astype(u.dtype)[:, None, :]
    y = lax.conv_general_dilated(u, taps, window_strides=(1,), padding=[(CONV_WIDTH - 1, 0)],
                                 dimension_numbers=('NWC', 'WIO', 'NWC'), feature_group_count=c)
    return y + bias.astype(u.dtype)


def ssd_chunked(xdt, a_dt, b_h, c_h):
    bsz, s, h, p = xdt.shape
    n = b_h.shape[-1]
    nc = s // CHUNK
    x = xdt.reshape(bsz, nc, CHUNK, h, p)
    a = a_dt.reshape(bsz, nc, CHUNK, h).transpose(0, 3, 1, 2)
    bc = b_h.reshape(bsz, nc, CHUNK, h, n)
    cc = c_h.reshape(bsz, nc, CHUNK, h, n)
    a_cs = jnp.cumsum(a, axis=-1)
    decay = jnp.exp(segsum(a))
    scores = jnp.einsum('bclhn,bcshn->bhcls', cc, bc) * decay
    y_diag = jnp.einsum('bhcls,bcshp->bclhp', scores, x)
    decay_states = jnp.exp(a_cs[..., -1:] - a_cs)
    states = jnp.einsum('bclhn,bhcl,bclhp->bchpn', bc, decay_states, x)
    chunk_decay = jnp.exp(a_cs[..., -1])

    def step(carry, inp):
        st, dec = inp
        return carry * dec[..., None, None] + st, carry

    _, prev = lax.scan(step, jnp.zeros((bsz, h, p, n), F32),
                       (states.transpose(1, 0, 2, 3, 4), chunk_decay.transpose(2, 0, 1)))
    prev = prev.transpose(1, 0, 2, 3, 4)
    y_off = jnp.einsum('bclhn,bchpn,bhcl->bclhp', cc, prev, jnp.exp(a_cs))
    return (y_diag + y_off).reshape(bsz, s, h, p)


def even_mixer(x, w_in, conv_w, conv_b, dt_bias, a_log, d_skip, ssm_norm_g,
               lq1, lk1, lq2, lk2, subln_g, w_out, lam_init):
    bsz, s, _ = x.shape
    hcat = x @ w_in
    o1 = 4 * ATT_WIDTH + SSM_WIDTH
    q, k, v, g_att, z, xbc, dt_raw = jnp.split(
        hcat, [ATT_WIDTH, 2 * ATT_WIDTH, 3 * ATT_WIDTH, 4 * ATT_WIDTH, o1, o1 + XBC_WIDTH], axis=-1)
    lam = (jnp.exp(jnp.sum(lq1.astype(F32) * lk1.astype(F32)))
           - jnp.exp(jnp.sum(lq2.astype(F32) * lk2.astype(F32))) + lam_init)
    q = q.reshape(bsz, s, DIFF_HEADS, 2, DIFF_HEAD_DIM)
    k = k.reshape(bsz, s, DIFF_HEADS, 2, DIFF_HEAD_DIM)
    v = v.reshape(bsz, s, DIFF_HEADS, DIFF_V_DIM)
    o = rms_norm(diff_attention(q, k, v, lam), subln_g) * (1.0 - lam_init)
    y_att = o.reshape(bsz, s, ATT_WIDTH).astype(x.dtype) * jax.nn.silu(g_att)
    xbc = jax.nn.silu(causal_depthwise_conv(xbc, conv_w, conv_b))
    xs, bm, cm = jnp.split(xbc, [SSM_WIDTH, SSM_WIDTH + SSM_GROUPS * SSM_STATE], axis=-1)
    rep = SSM_HEADS // SSM_GROUPS
    xs = xs.reshape(bsz, s, SSM_HEADS, SSM_HEAD_DIM).astype(F32)
    bm = jnp.repeat(bm.reshape(bsz, s, SSM_GROUPS, SSM_STATE), rep, axis=2).astype(F32)
    cm = jnp.repeat(cm.reshape(bsz, s, SSM_GROUPS, SSM_STATE), rep, axis=2).astype(F32)
    dt = jax.nn.softplus(dt_raw.astype(F32) + dt_bias.astype(F32))
    a = -jnp.exp(a_log.astype(F32))
    y = ssd_chunked(xs * dt[..., None], a * dt, bm, cm) + d_skip.astype(F32)[:, None] * xs
    y = y.reshape(bsz, s, SSM_WIDTH) * jax.nn.silu(z.astype(F32))
    y = rms_norm(y.reshape(bsz, s, SSM_GROUPS, SSM_WIDTH // SSM_GROUPS))
    y_ssm = (y.reshape(bsz, s, SSM_WIDTH) * ssm_norm_g.astype(F32)).astype(x.dtype)
    return jnp.concatenate([y_att, y_ssm], axis=-1) @ w_out


def pool_mixer(x, w_in, w_grp, b_grp, scale, w_out):
    bsz, s, _ = x.shape
    v, g = jnp.split(x @ w_in, [POOL_WIDTH], axis=-1)
    vf = v.astype(F32).reshape(bsz, s, POOL_NGROUPS, POOL_GROUP)
    cs = jnp.pad(jnp.cumsum(vf, axis=1), ((0, 0), (1, 0), (0, 0), (0, 0)))
    pos = jnp.arange(1, s + 1, dtype=F32)
    pooled = []
    for gi, w in enumerate(POOL_WINDOWS):
        c = cs[:, :, gi]
        lag = jnp.pad(c, ((0, 0), (w, 0), (0, 0)))[:, 1:s + 1]
        mean = (c[:, 1:] - lag) / jnp.minimum(pos, float(w))[None, :, None]
        pooled.append(mean - vf[:, :, gi])
    pooled = jnp.stack(pooled, axis=2).astype(x.dtype)
    m = jnp.einsum('bsgc,gcd->bsgd', pooled, w_grp) + b_grp
    y = m.reshape(bsz, s, POOL_WIDTH) * scale * jax.nn.silu(g)
    return y @ w_out


def setup_inputs(seed: int = 0) -> dict:
    key = jax.random.key(seed)
    ks = jax.random.split(key, 24)
    nrm = jax.random.normal
    dt0 = jnp.exp(jax.random.uniform(ks[4], (N_EVEN, SSM_HEADS), F32) * (math.log(0.1) - math.log(0.001))
                  + math.log(0.001))
    mix_w = ATT_WIDTH + SSM_WIDTH
    return {
        "x": nrm(ks[0], (BATCH, SEQ, D_MODEL), F32),
        "ev_w_in": nrm(ks[1], (N_EVEN, D_MODEL, EVEN_IN), F32) * D_MODEL ** -0.5,
        "ev_conv_w": nrm(ks[2], (N_EVEN, CONV_WIDTH, XBC_WIDTH), F32) * CONV_WIDTH ** -0.5,
        "ev_conv_b": 0.01 * nrm(ks[3], (N_EVEN, XBC_WIDTH), F32),
        "ev_dt_bias": dt0 + jnp.log(-jnp.expm1(-dt0)),
        "ev_a_log": jnp.log(jax.random.uniform(ks[5], (N_EVEN, SSM_HEADS), F32, 1.0, 16.0)),
        "ev_d_skip": 1.0 + 0.1 * nrm(ks[6], (N_EVEN, SSM_HEADS), F32),
        "ev_ssm_norm_g": 1.0 + 0.02 * nrm(ks[7], (N_EVEN, SSM_WIDTH), F32),
        "ev_lambda_q1": 0.1 * nrm(ks[8], (N_EVEN, DIFF_HEAD_DIM), F32),
        "ev_lambda_k1": 0.1 * nrm(ks[9], (N_EVEN, DIFF_HEAD_DIM), F32),
        "ev_lambda_q2": 0.1 * nrm(ks[10], (N_EVEN, DIFF_HEAD_DIM), F32),
        "ev_lambda_k2": 0.1 * nrm(ks[11], (N_EVEN, DIFF_HEAD_DIM), F32),
        "ev_subln_g": 1.0 + 0.02 * nrm(ks[12], (N_EVEN, DIFF_V_DIM), F32),
        "ev_w_out": nrm(ks[13], (N_EVEN, mix_w, D_MODEL), F32) * mix_w ** -0.5 * DEEPNORM_BETA,
        "od_w_in": nrm(ks[14], (N_ODD, D_MODEL, 2 * POOL_WIDTH), F32) * D_MODEL ** -0.5,
        "od_w_grp": nrm(ks[15], (N_ODD, POOL_NGROUPS, POOL_GROUP, POOL_GROUP), F32) * POOL_GROUP ** -0.5,
        "od_b_grp": 0.01 * nrm(ks[16], (N_ODD, POOL_NGROUPS, POOL_GROUP), F32),
        "od_scale": 1.0 + 0.02 * nrm(ks[17], (N_ODD, POOL_WIDTH), F32),
        "od_w_out": nrm(ks[18], (N_ODD, POOL_WIDTH, D_MODEL), F32) * POOL_WIDTH ** -0.5 * DEEPNORM_BETA,
        "ln_g": 1.0 + 0.02 * nrm(ks[19], (DEPTH, D_MODEL), F32),
        "ln_b": 0.02 * nrm(ks[20], (DEPTH, D_MODEL), F32),
    }


def reference(x, ev_w_in, ev_conv_w, ev_conv_b, ev_dt_bias, ev_a_log, ev_d_skip, ev_ssm_norm_g,
              ev_lambda_q1, ev_lambda_k1, ev_lambda_q2, ev_lambda_k2, ev_subln_g, ev_w_out,
              od_w_in, od_w_grp, od_b_grp, od_scale, od_w_out, ln_g, ln_b):
    for l in range(DEPTH):
        i = l // 2
        if l % 2 == 0:
            lam_init = 0.8 - 0.6 * math.exp(-0.3 * l)
            y = even_mixer(x, ev_w_in[i], ev_conv_w[i], ev_conv_b[i], ev_dt_bias[i], ev_a_log[i],
                           ev_d_skip[i], ev_ssm_norm_g[i], ev_lambda_q1[i], ev_lambda_k1[i],
                           ev_lambda_q2[i], ev_lambda_k2[i], ev_subln_g[i], ev_w_out[i], lam_init)
        else:
            y = pool_mixer(x, od_w_in[i], od_w_grp[i], od_b_grp[i], od_scale[i], od_w_out[i])
        x = layer_norm(DEEPNORM_ALPHA * x + y, ln_g[l], ln_b[l])
    return x
```

```python
import functools
import math

import jax
import jax.numpy as jnp
from jax import lax
from jax.experimental import pallas as pl
from jax.experimental.pallas import tpu as pltpu

F32 = jnp.float32
BF16 = jnp.bfloat16

DEPTH = 4
CHUNK = 64
CHUNK_SHIFT = CHUNK.bit_length() - 1
DIFF_HEAD_DIM = 64
DIFF_V_DIM = 2 * DIFF_HEAD_DIM
SSM_HEAD_DIM = 64
HEAD_SHIFT = SSM_HEAD_DIM.bit_length() - 1
SSM_GROUPS = 2
SSM_STATE = 128
CONV_WIDTH = 4
POOL_WINDOWS = (2, 4, 8, 16)
DEEPNORM_ALPHA = (2.0 * DEPTH) ** 0.25
LN_EPS = 1e-5
RMS_EPS = 1e-5

LANES = 128
SUBLANES = 8
NEG_BIG = -1e30
VMEM_LIMIT = 56 * 1024 * 1024

ATTN_TQ = 256
ATTN_TK = 1024
SSD_L = 256
MM_TM = 1024
MM_TN = 512
OUT_TM = 256
POOL_TM = 512
POOL_HALO = 16


def _silu(x):
    return x * (1.0 / (1.0 + jnp.exp(-x)))


def _softplus(x):
    return jnp.maximum(x, 0.0) + jnp.log(1.0 + jnp.exp(-jnp.abs(x)))


def _split3(a):
    hi = a.astype(BF16)
    r1 = a - hi.astype(F32)
    mid = r1.astype(BF16)
    lo = (r1 - mid.astype(F32)).astype(BF16)
    return hi, mid, lo


def _dot(a, b):
    return jnp.dot(a, b, preferred_element_type=F32)


def _dot_nt(a, b):
    return lax.dot_general(a, b, (((1,), (1,)), ((), ())), preferred_element_type=F32)


def _matmul_kernel(x_ref, w_ref, o_ref):
    o_ref[...] = _dot(x_ref[...].astype(BF16), w_ref[...]).astype(o_ref.dtype)


def _matmul(x, w, out_dtype, name):
    m, k = x.shape
    n = w.shape[1]
    tm = min(MM_TM, m)
    tn = min(MM_TN, n)
    return pl.pallas_call(
        _matmul_kernel,
        grid=(m // tm, n // tn),
        in_specs=[pl.BlockSpec((tm, k), lambda i, j: (i, 0)),
                  pl.BlockSpec((k, tn), lambda i, j: (0, j))],
        out_specs=pl.BlockSpec((tm, tn), lambda i, j: (i, j)),
        out_shape=jax.ShapeDtypeStruct((m, n), out_dtype),
        compiler_params=pltpu.CompilerParams(
            dimension_semantics=("parallel", "parallel"), vmem_limit_bytes=VMEM_LIMIT),
        name=name,
    )(x, w)


def _attn_kernel(lam_ref, subg_ref, q_ref, k_ref, v_ref, g_ref, o_ref, *, tq, tk, lam_init):
    qi = pl.program_id(1)
    lp = lam_ref[...]
    lam = (jnp.exp(jnp.sum(lp[0:1] * lp[1:2], axis=1, keepdims=True))
           - jnp.exp(jnp.sum(lp[2:3] * lp[3:4], axis=1, keepdims=True)) + lam_init)

    q = q_ref[...] * (DIFF_HEAD_DIM ** -0.5)
    lane = lax.broadcasted_iota(jnp.int32, (tq, DIFF_V_DIM), 1)
    zero = jnp.zeros_like(q)
    qq = jnp.concatenate([jnp.where(lane < DIFF_HEAD_DIM, q, zero),
                          jnp.where(lane >= DIFF_HEAD_DIM, q, zero)], axis=0)

    row = lax.broadcasted_iota(jnp.int32, (2 * tq, 1), 0)
    q_chunk = (qi * tq + jnp.where(row >= tq, row - tq, row)) >> CHUNK_SHIFT

    def block(kb, carry, masked):
        m, l, acc = carry
        ks = pl.multiple_of(kb * tk, tk)
        k = k_ref[pl.ds(ks, tk), :]
        v = v_ref[pl.ds(ks, tk), :]
        s = _dot_nt(qq, k)
        if masked:
            k_chunk = (ks + lax.broadcasted_iota(jnp.int32, (1, tk), 1)) >> CHUNK_SHIFT
            s = jnp.where(k_chunk <= q_chunk, s, NEG_BIG)
        m_new = jnp.maximum(m, jnp.max(s, axis=1, keepdims=True))
        alpha = jnp.exp(m - m_new)
        p = jnp.exp(s - m_new)
        l = alpha * l + jnp.sum(p, axis=1, keepdims=True)
        acc = alpha * acc + _dot(p.astype(BF16), v)
        return m_new, l, acc

    init = (jnp.full((2 * tq, 1), NEG_BIG, F32), jnp.zeros((2 * tq, 1), F32),
            jnp.zeros((2 * tq, DIFF_V_DIM), F32))
    n_full = (qi * tq) // tk
    carry = lax.fori_loop(0, n_full, lambda kb, c: block(kb, c, False), init)
    _, l, acc = block(n_full, carry, True)

    o = acc * (1.0 / l)
    o = o[:tq] - lam * o[tq:]
    o = o * lax.rsqrt(jnp.mean(o * o, axis=1, keepdims=True) + RMS_EPS)
    o = o * subg_ref[...] * (1.0 - lam_init)
    o_ref[...] = (o * _silu(g_ref[...])).astype(o_ref.dtype)


def _attention(qkv, gzx, lam_params, subln_g, lam_init):
    s = qkv.shape[0]
    heads = qkv.shape[1] // (3 * DIFF_V_DIM)
    tq, tk = ATTN_TQ, ATTN_TK
    kern = functools.partial(_attn_kernel, tq=tq, tk=tk, lam_init=lam_init)
    return pl.pallas_call(
        kern,
        grid=(heads, s // tq),
        in_specs=[
            pl.BlockSpec((4, DIFF_HEAD_DIM), lambda h, i: (0, 0)),
            pl.BlockSpec((1, DIFF_V_DIM), lambda h, i: (0, 0)),
            pl.BlockSpec((tq, DIFF_V_DIM), lambda h, i: (i, h)),
            pl.BlockSpec((s, DIFF_V_DIM), lambda h, i: (0, heads + h)),
            pl.BlockSpec((s, DIFF_V_DIM), lambda h, i: (0, 2 * heads + h)),
            pl.BlockSpec((tq, DIFF_V_DIM), lambda h, i: (i, h)),
        ],
        out_specs=pl.BlockSpec((tq, DIFF_V_DIM), lambda h, i: (i, h)),
        out_shape=jax.ShapeDtypeStruct((s, heads * DIFF_V_DIM), BF16),
        compiler_params=pltpu.CompilerParams(
            dimension_semantics=("parallel", "parallel"), vmem_limit_bytes=VMEM_LIMIT),
        name="diff_attention",
    )(lam_params, subln_g, qkv, qkv, qkv, gzx)


def _conv_silu(cur_ref, tail_ref, ext_ref, w_ref, b_ref, rows):
    ext_ref[0:SUBLANES, :] = tail_ref[...]
    ext_ref[SUBLANES:SUBLANES + rows, :] = cur_ref[...]
    tail_ref[...] = cur_ref[rows - SUBLANES:rows, :]
    acc = b_ref[...]
    for t in range(CONV_WIDTH):
        start = SUBLANES - (CONV_WIDTH - 1) + t
        acc = acc + w_ref[t:t + 1, :] * ext_ref[start:start + rows, :]
    return _silu(acc)


def _ssd_kernel(z_ref, xs_ref, bc_ref, dt_ref, cwx_ref, cbx_ref, cwbc_ref, cbbc_ref,
                dtb_ref, alog_ref, dskip_ref, ng_ref, y_ref,
                state_ref, tailx_ref, tailbc_ref, extx_ref, extbc_ref, *, rows):
    c = pl.program_id(0)
    width = xs_ref.shape[1]
    gw = width // SSM_GROUPS
    heads_per_pair = LANES // SSM_HEAD_DIM

    @pl.when(c == 0)
    def _():
        state_ref[...] = jnp.zeros_like(state_ref)
        tailx_ref[...] = jnp.zeros_like(tailx_ref)
        tailbc_ref[...] = jnp.zeros_like(tailbc_ref)

    xs = _conv_silu(xs_ref, tailx_ref, extx_ref, cwx_ref, cbx_ref, rows)
    bcv = _conv_silu(bc_ref, tailbc_ref, extbc_ref, cwbc_ref, cbbc_ref, rows)

    dtc = _softplus(dt_ref[...] + dtb_ref[...])
    adt = -jnp.exp(alog_ref[...]) * dtc
    ri = lax.broadcasted_iota(jnp.int32, (rows, rows), 0)
    ci = lax.broadcasted_iota(jnp.int32, (rows, rows), 1)
    causal = ci <= ri
    tri = jnp.where(causal, 1.0, 0.0).astype(BF16)
    csc = sum(_dot(tri, part) for part in _split3(adt))
    cs_t = csc.T

    er = lax.broadcasted_iota(jnp.int32, (LANES, width), 0)
    ec = lax.broadcasted_iota(jnp.int32, (LANES, width), 1)
    expand = jnp.where((ec >> HEAD_SHIFT) == er, 1.0, 0.0).astype(BF16)
    dt_e = sum(_dot(part, expand) for part in _split3(dtc))
    cs_e = sum(_dot(part, expand) for part in _split3(csc))
    cs_last = cs_e[rows - 1:rows, :]

    xdt = xs * dt_e
    xdt_b = xdt.astype(BF16)
    xd_b = (xdt * jnp.exp(cs_last - cs_e)).astype(BF16)
    ecs = jnp.exp(cs_e)
    chunk_decay = jnp.exp(cs_last)
    lane = lax.broadcasted_iota(jnp.int32, (rows, LANES), 1)

    for g in range(SSM_GROUPS):
        gsl = slice(g * gw, (g + 1) * gw)
        b_f = bcv[:, g * SSM_STATE:(g + 1) * SSM_STATE]
        c_b = bcv[:, (SSM_GROUPS + g) * SSM_STATE:(SSM_GROUPS + g + 1) * SSM_STATE].astype(BF16)
        scores = _dot_nt(c_b, b_f.astype(BF16))
        st = state_ref[g]
        y_off = _dot(c_b, st.astype(BF16)) * ecs[:, gsl]
        state_ref[g] = st * chunk_decay[:, gsl] + _dot(b_f.T.astype(BF16), xd_b[:, gsl])

        y_diag = []
        for pair in range(gw // LANES):
            col0 = g * gw + pair * LANES
            xpair = xdt_b[:, col0:col0 + LANES]
            parts = []
            for hh in range(heads_per_pair):
                h = col0 // SSM_HEAD_DIM + hh
                seg = cs_e[:, h * SSM_HEAD_DIM:h * SSM_HEAD_DIM + 1] - cs_t[h:h + 1, :]
                decay = jnp.exp(jnp.where(causal, seg, NEG_BIG))
                parts.append(_dot((scores * decay).astype(BF16), xpair))
            y_diag.append(jnp.where(lane < SSM_HEAD_DIM, parts[0], parts[1]))
        y = jnp.concatenate(y_diag, axis=1) + y_off + dskip_ref[:, gsl] * xs[:, gsl]
        y = y * _silu(z_ref[:, gsl])
        y = y * lax.rsqrt(jnp.mean(y * y, axis=1, keepdims=True) + RMS_EPS)
        y_ref[:, gsl] = (y * ng_ref[:, gsl]).astype(y_ref.dtype)


def _ssd(gzx, dt_raw, conv_w, conv_b, dt_bias, a_log, d_skip, norm_g, width):
    s = gzx.shape[0]
    rows = SSD_L
    bcw = 2 * SSM_GROUPS * SSM_STATE
    heads = width // SSM_HEAD_DIM

    def pad_heads(p):
        return jnp.pad(p.astype(F32), (0, LANES - heads)).reshape(1, LANES)

    def per_channel(p):
        return jnp.repeat(p.astype(F32), SSM_HEAD_DIM).reshape(1, width)

    z_blk = 1
    xs_blk = 2
    bc_blk = (3 * width) // bcw
    const = lambda c: (0, 0)
    kern = functools.partial(_ssd_kernel, rows=rows)
    return pl.pallas_call(
        kern,
        grid=(s // rows,),
        in_specs=[
            pl.BlockSpec((rows, width), lambda c: (c, z_blk)),
            pl.BlockSpec((rows, width), lambda c: (c, xs_blk)),
            pl.BlockSpec((rows, bcw), lambda c: (c, bc_blk)),
            pl.BlockSpec((rows, LANES), lambda c: (c, 0)),
            pl.BlockSpec((CONV_WIDTH, width), const),
            pl.BlockSpec((1, width), const),
            pl.BlockSpec((CONV_WIDTH, bcw), const),
            pl.BlockSpec((1, bcw), const),
            pl.BlockSpec((1, LANES), const),
            pl.BlockSpec((1, LANES), const),
            pl.BlockSpec((1, width), const),
            pl.BlockSpec((1, width), const),
        ],
        out_specs=pl.BlockSpec((rows, width), lambda c: (c, 0)),
        out_shape=jax.ShapeDtypeStruct((s, width), BF16),
        scratch_shapes=[
            pltpu.VMEM((SSM_GROUPS, SSM_STATE, width // SSM_GROUPS), F32),
            pltpu.VMEM((SUBLANES, width), F32),
            pltpu.VMEM((SUBLANES, bcw), F32),
            pltpu.VMEM((SUBLANES + rows, width), F32),
            pltpu.VMEM((SUBLANES + rows, bcw), F32),
        ],
        compiler_params=pltpu.CompilerParams(
            dimension_semantics=("arbitrary",), vmem_limit_bytes=VMEM_LIMIT),
        name="ssd_scan",
    )(gzx, gzx, gzx, dt_raw,
      conv_w[:, :width], conv_b[:width].reshape(1, width),
      conv_w[:, width:], conv_b[width:].reshape(1, bcw),
      pad_heads(dt_bias), pad_heads(a_log), per_channel(d_skip), norm_g.reshape(1, width))


def _pool_kernel(v_ref, g_ref, wg_ref, bg_ref, sc_ref, y_ref, tail_ref, ext_ref, *, rows):
    i = pl.program_id(0)
    gc = wg_ref.shape[1]

    @pl.when(i == 0)
    def _():
        tail_ref[...] = jnp.zeros_like(tail_ref)

    ext_ref[0:POOL_HALO, :] = tail_ref[...]
    ext_ref[POOL_HALO:POOL_HALO + rows, :] = v_ref[...]
    tail_ref[...] = v_ref[rows - POOL_HALO:rows, :]
    pos = (i * rows + lax.broadcasted_iota(jnp.int32, (rows, 1), 0) + 1).astype(F32)

    for gi, w in enumerate(POOL_WINDOWS):
        cols = slice(gi * gc, (gi + 1) * gc)
        v = v_ref[:, cols]
        acc = v
        for d in range(1, w):
            acc = acc + ext_ref[POOL_HALO - d:POOL_HALO - d + rows, cols]
        pooled = acc * (1.0 / jnp.minimum(pos, float(w))) - v
        m = _dot(pooled.astype(BF16), wg_ref[gi]) + bg_ref[:, cols]
        y_ref[:, cols] = (m * sc_ref[:, cols] * _silu(g_ref[:, cols])).astype(y_ref.dtype)


def _pool(vg, w_grp, b_grp, scale):
    s = vg.shape[0]
    width = vg.shape[1] // 2
    rows = POOL_TM
    ng, gc, _ = w_grp.shape
    kern = functools.partial(_pool_kernel, rows=rows)
    return pl.pallas_call(
        kern,
        grid=(s // rows,),
        in_specs=[
            pl.BlockSpec((rows, width), lambda i: (i, 0)),
            pl.BlockSpec((rows, width), lambda i: (i, 1)),
            pl.BlockSpec((ng, gc, gc), lambda i: (0, 0, 0)),
            pl.BlockSpec((1, width), lambda i: (0, 0)),
            pl.BlockSpec((1, width), lambda i: (0, 0)),
        ],
        out_specs=pl.BlockSpec((rows, width), lambda i: (i, 0)),
        out_shape=jax.ShapeDtypeStruct((s, width), BF16),
        scratch_shapes=[pltpu.VMEM((POOL_HALO, width), F32),
                        pltpu.VMEM((POOL_HALO + rows, width), F32)],
        compiler_params=pltpu.CompilerParams(
            dimension_semantics=("arbitrary",), vmem_limit_bytes=VMEM_LIMIT),
        name="pool_mixer",
    )(vg, vg, w_grp.astype(BF16), b_grp.reshape(1, width), scale.reshape(1, width))


def _out_ln_kernel(ya_ref, yb_ref, wa_ref, wb_ref, x_ref, g_ref, b_ref, xo_ref, xob_ref):
    y = _dot(ya_ref[...], wa_ref[...]) + _dot(yb_ref[...], wb_ref[...])
    h = DEEPNORM_ALPHA * x_ref[...] + y
    mu = jnp.mean(h, axis=1, keepdims=True)
    d = h - mu
    var = jnp.mean(d * d, axis=1, keepdims=True)
    out = d * lax.rsqrt(var + LN_EPS) * g_ref[...] + b_ref[...]
    xo_ref[...] = out
    xob_ref[...] = out.astype(BF16)


def _out_ln(ya, yb, ya_blk, yb_blk, w_out, x, ln_g, ln_b):
    s, d = x.shape
    half = w_out.shape[0] // 2
    tm = OUT_TM
    w = w_out.astype(BF16)
    return pl.pallas_call(
        _out_ln_kernel,
        grid=(s // tm,),
        in_specs=[
            pl.BlockSpec((tm, half), lambda i: (i, ya_blk)),
            pl.BlockSpec((tm, half), lambda i: (i, yb_blk)),
            pl.BlockSpec((half, d), lambda i: (0, 0)),
            pl.BlockSpec((half, d), lambda i: (1, 0)),
            pl.BlockSpec((tm, d), lambda i: (i, 0)),
            pl.BlockSpec((1, d), lambda i: (0, 0)),
            pl.BlockSpec((1, d), lambda i: (0, 0)),
        ],
        out_specs=[pl.BlockSpec((tm, d), lambda i: (i, 0)),
                   pl.BlockSpec((tm, d), lambda i: (i, 0))],
        out_shape=[jax.ShapeDtypeStruct((s, d), F32), jax.ShapeDtypeStruct((s, d), BF16)],
        compiler_params=pltpu.CompilerParams(
            dimension_semantics=("parallel",), vmem_limit_bytes=VMEM_LIMIT),
        name="out_proj_layernorm",
    )(ya, yb, w, w, x, ln_g.reshape(1, d), ln_b.reshape(1, d))


def _even_layer(x, xb, w_in, conv_w, conv_b, dt_bias, a_log, d_skip, ssm_norm_g,
                lq1, lk1, lq2, lk2, subln_g, w_out, ln_g, ln_b, lam_init):
    d = x.shape[1]
    att_w = d // 2
    ssm_w = d // 2
    heads = ssm_w // SSM_HEAD_DIM
    n_qkv = 3 * att_w
    n_gzx = att_w + ssm_w + ssm_w + 2 * SSM_GROUPS * SSM_STATE
    w = w_in.astype(BF16)
    qkv = _matmul(xb, w[:, :n_qkv], BF16, "even_in_qkv")
    gzx = _matmul(xb, w[:, n_qkv:n_qkv + n_gzx], F32, "even_in_gate_ssm")
    w_dt = jnp.pad(w[:, n_qkv + n_gzx:], ((0, 0), (0, LANES - heads)))
    dt_raw = _matmul(xb, w_dt, F32, "even_in_dt")
    lam_params = jnp.stack([lq1, lk1, lq2, lk2]).astype(F32)
    y_att = _attention(qkv, gzx, lam_params, subln_g.reshape(1, DIFF_V_DIM), lam_init)
    y_ssm = _ssd(gzx, dt_raw, conv_w, conv_b, dt_bias, a_log, d_skip, ssm_norm_g, ssm_w)
    return _out_ln(y_att, y_ssm, 0, 0, w_out, x, ln_g, ln_b)


def _odd_layer(x, xb, w_in, w_grp, b_grp, scale, w_out, ln_g, ln_b):
    vg = _matmul(xb, w_in.astype(BF16), F32, "odd_in")
    y = _pool(vg, w_grp, b_grp, scale)
    return _out_ln(y, y, 0, 1, w_out, x, ln_g, ln_b)


def kernel(x, ev_w_in, ev_conv_w, ev_conv_b, ev_dt_bias, ev_a_log, ev_d_skip, ev_ssm_norm_g, ev_lambda_q1, ev_lambda_k1, ev_lambda_q2, ev_lambda_k2, ev_subln_g, ev_w_out, od_w_in, od_w_grp, od_b_grp, od_scale, od_w_out, ln_g, ln_b):
    bsz, s, d = x.shape
    outs = []
    for b in range(bsz):
        xf = x[b]
        xb = xf
        for l in range(DEPTH):
            i = l // 2
            if l % 2 == 0:
                lam_init = 0.8 - 0.6 * math.exp(-0.3 * l)
                xf, xb = _even_layer(
                    xf, xb, ev_w_in[i], ev_conv_w[i], ev_conv_b[i], ev_dt_bias[i], ev_a_log[i],
                    ev_d_skip[i], ev_ssm_norm_g[i], ev_lambda_q1[i], ev_lambda_k1[i],
                    ev_lambda_q2[i], ev_lambda_k2[i], ev_subln_g[i], ev_w_out[i],
                    ln_g[l], ln_b[l], lam_init)
            else:
                xf, xb = _odd_layer(xf, xb, od_w_in[i], od_w_grp[i], od_b_grp[i], od_scale[i],
                                    od_w_out[i], ln_g[l], ln_b[l])
        outs.append(xf)
    return jnp.stack(outs)
```

```python
import functools
import math

import jax
import jax.numpy as jnp
from jax import lax
from jax.experimental import pallas as pl
from jax.experimental.pallas import tpu as pltpu

F32 = jnp.float32
BF16 = jnp.bfloat16

DEPTH = 4
CHUNK = 64
CHUNK_SHIFT = CHUNK.bit_length() - 1
DIFF_HEAD_DIM = 64
DIFF_V_DIM = 2 * DIFF_HEAD_DIM
SSM_HEAD_DIM = 64
HEAD_SHIFT = SSM_HEAD_DIM.bit_length() - 1
SSM_GROUPS = 2
SSM_STATE = 128
CONV_WIDTH = 4
POOL_WINDOWS = (2, 4, 8, 16)
DEEPNORM_ALPHA = (2.0 * DEPTH) ** 0.25
LN_EPS = 1e-5
RMS_EPS = 1e-5

LANES = 128
SUBLANES = 8
NEG_BIG = -1e30
FINITE_MAX = 3.0e38
VMEM_LIMIT = 56 * 1024 * 1024

ATTN_TQ = 256
ATTN_TK = 1024
ATTN_HP = 4
SSD_L = 256
MM_TM = 1024
MM_TN = 512
OUT_TM = 256
POOL_TM = 512
POOL_HALO = 16


def _silu(x):
    return x * (1.0 / (1.0 + jnp.exp(-x)))


def _softplus(x):
    return jnp.maximum(x, 0.0) + jnp.log(1.0 + jnp.exp(-jnp.abs(x)))


def _split3(a):
    hi = a.astype(BF16)
    r1 = a - hi.astype(F32)
    mid = r1.astype(BF16)
    lo = (r1 - mid.astype(F32)).astype(BF16)
    return hi, mid, lo


def _dot(a, b):
    return jnp.dot(a, b, preferred_element_type=F32)


def _dot_nt(a, b):
    return lax.dot_general(a, b, (((1,), (1,)), ((), ())), preferred_element_type=F32)


def _matmul_kernel(x_ref, w_ref, o_ref):
    o_ref[...] = _dot(x_ref[...].astype(BF16), w_ref[...]).astype(o_ref.dtype)


def _matmul(x, w, out_dtype, name):
    m, k = x.shape
    n = w.shape[1]
    tm = min(MM_TM, m)
    tn = min(MM_TN, n)
    return pl.pallas_call(
        _matmul_kernel,
        grid=(m // tm, n // tn),
        in_specs=[pl.BlockSpec((tm, k), lambda i, j: (i, 0)),
                  pl.BlockSpec((k, tn), lambda i, j: (0, j))],
        out_specs=pl.BlockSpec((tm, tn), lambda i, j: (i, j)),
        out_shape=jax.ShapeDtypeStruct((m, n), out_dtype),
        compiler_params=pltpu.CompilerParams(
            dimension_semantics=("parallel", "parallel"), vmem_limit_bytes=VMEM_LIMIT),
        name=name,
    )(x, w)


def _attn_kernel(lam_ref, subg_ref, q_ref, k_ref, v_ref, g_ref, o_ref, *, tq, tk, hp, lam_init):
    qi = pl.program_id(1)
    lp = lam_ref[...]
    lam = (jnp.exp(jnp.sum(lp[0:1] * lp[1:2], axis=1, keepdims=True))
           - jnp.exp(jnp.sum(lp[2:3] * lp[3:4], axis=1, keepdims=True)) + lam_init)

    lane = lax.broadcasted_iota(jnp.int32, (tq, DIFF_V_DIM), 1)
    row = lax.broadcasted_iota(jnp.int32, (2 * tq, 1), 0)
    q_chunk = (qi * tq + jnp.where(row >= tq, row - tq, row)) >> CHUNK_SHIFT

    def stacked_q(j):
        q = q_ref[:, j * DIFF_V_DIM:(j + 1) * DIFF_V_DIM] * (DIFF_HEAD_DIM ** -0.5)
        zero = jnp.zeros_like(q)
        return jnp.concatenate([jnp.where(lane < DIFF_HEAD_DIM, q, zero),
                                jnp.where(lane >= DIFF_HEAD_DIM, q, zero)], axis=0)

    qq = [stacked_q(j) for j in range(hp)]

    def kv(kb, j, size):
        ks = pl.multiple_of(kb * size, size)
        cols = slice(j * DIFF_V_DIM, (j + 1) * DIFF_V_DIM)
        k = k_ref[pl.ds(ks, size), cols]
        v1 = jnp.concatenate([v_ref[pl.ds(ks, size), cols],
                              jnp.ones((size, DIFF_V_DIM), BF16)], axis=1)
        return ks, k, v1

    def exact_block(kb, j, m, acc, masked):
        ks, k, v1 = kv(kb, j, tq)
        s = _dot_nt(qq[j], k)
        if masked:
            k_chunk = (ks + lax.broadcasted_iota(jnp.int32, (1, tq), 1)) >> CHUNK_SHIFT
            s = jnp.where(k_chunk <= q_chunk, s, NEG_BIG)
        m_new = jnp.maximum(m, jnp.max(s, axis=1, keepdims=True))
        p = jnp.exp(s - m_new).astype(BF16)
        return m_new, jnp.exp(m - m_new) * acc + _dot(p, v1)

    def fast_block(kb, j, ref, acc):
        ks, k, v1 = kv(kb, j, tk)
        before = (ks + lax.broadcasted_iota(jnp.int32, (1, tk), 1)) < qi * tq
        p = jnp.exp(jnp.where(before, _dot_nt(qq[j], k) - ref, NEG_BIG)).astype(BF16)
        return acc + _dot(p, v1)

    n_fast = (qi * tq + tk - 1) // tk
    first = [exact_block(qi, j, jnp.full((2 * tq, 1), NEG_BIG, F32),
                         jnp.zeros((2 * tq, 2 * DIFF_V_DIM), F32), True) for j in range(hp)]

    def finish(j, acc):
        cols = slice(j * DIFF_V_DIM, (j + 1) * DIFF_V_DIM)
        o = acc[:, :DIFF_V_DIM] * (1.0 / acc[:, DIFF_V_DIM:DIFF_V_DIM + 1])
        o = o[:tq] - lam * o[tq:]
        o = o * lax.rsqrt(jnp.mean(o * o, axis=1, keepdims=True) + RMS_EPS)
        o = o * subg_ref[...] * (1.0 - lam_init)
        o_ref[:, cols] = (o * _silu(g_ref[:, cols])).astype(o_ref.dtype)

    def fast_step(kb, accs):
        return tuple(fast_block(kb, j, first[j][0], accs[j]) for j in range(hp))

    accs = lax.fori_loop(0, n_fast, fast_step, tuple(f[1] for f in first))
    overflow = jnp.float32(0.0)
    for j in range(hp):
        finish(j, accs[j])
        overflow = jnp.maximum(overflow, jnp.max(jnp.where(jnp.abs(accs[j]) < FINITE_MAX, 0.0, 1.0)))

    @pl.when(overflow > 0.0)
    def _():
        def exact_step(kb, carry):
            return tuple(exact_block(kb, j, *carry[j], False) for j in range(hp))
        carry = lax.fori_loop(0, qi, exact_step, tuple(first))
        for j in range(hp):
            finish(j, carry[j][1])


def _attention(qkv, gzx, lam_params, subln_g, lam_init):
    s = qkv.shape[0]
    heads = qkv.shape[1] // (3 * DIFF_V_DIM)
    tq, tk, hp = ATTN_TQ, ATTN_TK, ATTN_HP
    groups = heads // hp
    bw = hp * DIFF_V_DIM
    kern = functools.partial(_attn_kernel, tq=tq, tk=tk, hp=hp, lam_init=lam_init)
    return pl.pallas_call(
        kern,
        grid=(groups, s // tq),
        in_specs=[
            pl.BlockSpec((4, DIFF_HEAD_DIM), lambda h, i: (0, 0)),
            pl.BlockSpec((1, DIFF_V_DIM), lambda h, i: (0, 0)),
            pl.BlockSpec((tq, bw), lambda h, i: (i, h)),
            pl.BlockSpec((s, bw), lambda h, i: (0, groups + h)),
            pl.BlockSpec((s, bw), lambda h, i: (0, 2 * groups + h)),
            pl.BlockSpec((tq, bw), lambda h, i: (i, h)),
        ],
        out_specs=pl.BlockSpec((tq, bw), lambda h, i: (i, h)),
        out_shape=jax.ShapeDtypeStruct((s, heads * DIFF_V_DIM), BF16),
        compiler_params=pltpu.CompilerParams(
            dimension_semantics=("parallel", "parallel"), vmem_limit_bytes=VMEM_LIMIT),
        name="diff_attention",
    )(lam_params, subln_g, qkv, qkv, qkv, gzx)


def _conv_silu(cur_ref, tail_ref, ext_ref, w_ref, b_ref, rows):
    ext_ref[0:SUBLANES, :] = tail_ref[...]
    ext_ref[SUBLANES:SUBLANES + rows, :] = cur_ref[...]
    tail_ref[...] = cur_ref[rows - SUBLANES:rows, :]
    acc = b_ref[...]
    for t in range(CONV_WIDTH):
        start = SUBLANES - (CONV_WIDTH - 1) + t
        acc = acc + w_ref[t:t + 1, :] * ext_ref[start:start + rows, :]
    return _silu(acc)


def _ssd_kernel(z_ref, xs_ref, bc_ref, dt_ref, cwx_ref, cbx_ref, cwbc_ref, cbbc_ref,
                dtb_ref, alog_ref, dskip_ref, ng_ref, y_ref,
                state_ref, tailx_ref, tailbc_ref, extx_ref, extbc_ref, *, rows):
    c = pl.program_id(0)
    width = xs_ref.shape[1]
    gw = width // SSM_GROUPS
    heads_per_pair = LANES // SSM_HEAD_DIM

    @pl.when(c == 0)
    def _():
        state_ref[...] = jnp.zeros_like(state_ref)
        tailx_ref[...] = jnp.zeros_like(tailx_ref)
        tailbc_ref[...] = jnp.zeros_like(tailbc_ref)

    xs = _conv_silu(xs_ref, tailx_ref, extx_ref, cwx_ref, cbx_ref, rows)
    bcv = _conv_silu(bc_ref, tailbc_ref, extbc_ref, cwbc_ref, cbbc_ref, rows)

    dtc = _softplus(dt_ref[...] + dtb_ref[...])
    adt = -jnp.exp(alog_ref[...]) * dtc
    ri = lax.broadcasted_iota(jnp.int32, (rows, rows), 0)
    ci = lax.broadcasted_iota(jnp.int32, (rows, rows), 1)
    causal = ci <= ri
    tri = jnp.where(causal, 1.0, 0.0).astype(BF16)
    csc = sum(_dot(tri, part) for part in _split3(adt))
    cs_t = csc.T

    er = lax.broadcasted_iota(jnp.int32, (LANES, width), 0)
    ec = lax.broadcasted_iota(jnp.int32, (LANES, width), 1)
    expand = jnp.where((ec >> HEAD_SHIFT) == er, 1.0, 0.0).astype(BF16)
    dt_e = sum(_dot(part, expand) for part in _split3(dtc))
    cs_e = sum(_dot(part, expand) for part in _split3(csc))
    cs_last = cs_e[rows - 1:rows, :]

    xdt = xs * dt_e
    xdt_b = xdt.astype(BF16)
    xd_b = (xdt * jnp.exp(cs_last - cs_e)).astype(BF16)
    ecs = jnp.exp(cs_e)
    chunk_decay = jnp.exp(cs_last)
    lane = lax.broadcasted_iota(jnp.int32, (rows, LANES), 1)

    for g in range(SSM_GROUPS):
        gsl = slice(g * gw, (g + 1) * gw)
        b_f = bcv[:, g * SSM_STATE:(g + 1) * SSM_STATE]
        c_b = bcv[:, (SSM_GROUPS + g) * SSM_STATE:(SSM_GROUPS + g + 1) * SSM_STATE].astype(BF16)
        scores = _dot_nt(c_b, b_f.astype(BF16))
        st = state_ref[g]
        y_off = _dot(c_b, st.astype(BF16)) * ecs[:, gsl]
        state_ref[g] = st * chunk_decay[:, gsl] + _dot(b_f.T.astype(BF16), xd_b[:, gsl])

        y_diag = []
        for pair in range(gw // LANES):
            col0 = g * gw + pair * LANES
            xpair = xdt_b[:, col0:col0 + LANES]
            parts = []
            for hh in range(heads_per_pair):
                h = col0 // SSM_HEAD_DIM + hh
                seg = cs_e[:, h * SSM_HEAD_DIM:h * SSM_HEAD_DIM + 1] - cs_t[h:h + 1, :]
                decay = jnp.exp(jnp.where(causal, seg, NEG_BIG))
                parts.append(_dot((scores * decay).astype(BF16), xpair))
            y_diag.append(jnp.where(lane < SSM_HEAD_DIM, parts[0], parts[1]))
        y = jnp.concatenate(y_diag, axis=1) + y_off + dskip_ref[:, gsl] * xs[:, gsl]
        y = y * _silu(z_ref[:, gsl])
        y = y * lax.rsqrt(jnp.mean(y * y, axis=1, keepdims=True) + RMS_EPS)
        y_ref[:, gsl] = (y * ng_ref[:, gsl]).astype(y_ref.dtype)


def _ssd(gzx, dt_raw, conv_w, conv_b, dt_bias, a_log, d_skip, norm_g, width):
    s = gzx.shape[0]
    rows = SSD_L
    bcw = 2 * SSM_GROUPS * SSM_STATE
    heads = width // SSM_HEAD_DIM

    def pad_heads(p):
        return jnp.pad(p.astype(F32), (0, LANES - heads)).reshape(1, LANES)

    def per_channel(p):
        return jnp.repeat(p.astype(F32), SSM_HEAD_DIM).reshape(1, width)

    z_blk = 1
    xs_blk = 2
    bc_blk = (3 * width) // bcw
    const = lambda c: (0, 0)
    kern = functools.partial(_ssd_kernel, rows=rows)
    return pl.pallas_call(
        kern,
        grid=(s // rows,),
        in_specs=[
            pl.BlockSpec((rows, width), lambda c: (c, z_blk)),
            pl.BlockSpec((rows, width), lambda c: (c, xs_blk)),
            pl.BlockSpec((rows, bcw), lambda c: (c, bc_blk)),
            pl.BlockSpec((rows, LANES), lambda c: (c, 0)),
            pl.BlockSpec((CONV_WIDTH, width), const),
            pl.BlockSpec((1, width), const),
            pl.BlockSpec((CONV_WIDTH, bcw), const),
            pl.BlockSpec((1, bcw), const),
            pl.BlockSpec((1, LANES), const),
            pl.BlockSpec((1, LANES), const),
            pl.BlockSpec((1, width), const),
            pl.BlockSpec((1, width), const),
        ],
        out_specs=pl.BlockSpec((rows, width), lambda c: (c, 0)),
        out_shape=jax.ShapeDtypeStruct((s, width), BF16),
        scratch_shapes=[
            pltpu.VMEM((SSM_GROUPS, SSM_STATE, width // SSM_GROUPS), F32),
            pltpu.VMEM((SUBLANES, width), F32),
            pltpu.VMEM((SUBLANES, bcw), F32),
            pltpu.VMEM((SUBLANES + rows, width), F32),
            pltpu.VMEM((SUBLANES + rows, bcw), F32),
        ],
        compiler_params=pltpu.CompilerParams(
            dimension_semantics=("arbitrary",), vmem_limit_bytes=VMEM_LIMIT),
        name="ssd_scan",
    )(gzx, gzx, gzx, dt_raw,
      conv_w[:, :width], conv_b[:width].reshape(1, width),
      conv_w[:, width:], conv_b[width:].reshape(1, bcw),
      pad_heads(dt_bias), pad_heads(a_log), per_channel(d_skip), norm_g.reshape(1, width))


def _pool_kernel(v_ref, g_ref, wg_ref, bg_ref, sc_ref, y_ref, tail_ref, ext_ref, *, rows):
    i = pl.program_id(0)
    gc = wg_ref.shape[1]

    @pl.when(i == 0)
    def _():
        tail_ref[...] = jnp.zeros_like(tail_ref)

    ext_ref[0:POOL_HALO, :] = tail_ref[...]
    ext_ref[POOL_HALO:POOL_HALO + rows, :] = v_ref[...]
    tail_ref[...] = v_ref[rows - POOL_HALO:rows, :]
    pos = (i * rows + lax.broadcasted_iota(jnp.int32, (rows, 1), 0) + 1).astype(F32)

    for gi, w in enumerate(POOL_WINDOWS):
        cols = slice(gi * gc, (gi + 1) * gc)
        v = v_ref[:, cols]
        acc = v
        for d in range(1, w):
            acc = acc + ext_ref[POOL_HALO - d:POOL_HALO - d + rows, cols]
        pooled = acc * (1.0 / jnp.minimum(pos, float(w))) - v
        m = _dot(pooled.astype(BF16), wg_ref[gi]) + bg_ref[:, cols]
        y_ref[:, cols] = (m * sc_ref[:, cols] * _silu(g_ref[:, cols])).astype(y_ref.dtype)


def _pool(vg, w_grp, b_grp, scale):
    s = vg.shape[0]
    width = vg.shape[1] // 2
    rows = POOL_TM
    ng, gc, _ = w_grp.shape
    kern = functools.partial(_pool_kernel, rows=rows)
    return pl.pallas_call(
        kern,
        grid=(s // rows,),
        in_specs=[
            pl.BlockSpec((rows, width), lambda i: (i, 0)),
            pl.BlockSpec((rows, width), lambda i: (i, 1)),
            pl.BlockSpec((ng, gc, gc), lambda i: (0, 0, 0)),
            pl.BlockSpec((1, width), lambda i: (0, 0)),
            pl.BlockSpec((1, width), lambda i: (0, 0)),
        ],
        out_specs=pl.BlockSpec((rows, width), lambda i: (i, 0)),
        out_shape=jax.ShapeDtypeStruct((s, width), BF16),
        scratch_shapes=[pltpu.VMEM((POOL_HALO, width), F32),
                        pltpu.VMEM((POOL_HALO + rows, width), F32)],
        compiler_params=pltpu.CompilerParams(
            dimension_semantics=("arbitrary",), vmem_limit_bytes=VMEM_LIMIT),
        name="pool_mixer",
    )(vg, vg, w_grp.astype(BF16), b_grp.reshape(1, width), scale.reshape(1, width))


def _out_ln_kernel(ya_ref, yb_ref, wa_ref, wb_ref, x_ref, g_ref, b_ref, xo_ref, xob_ref):
    y = _dot(ya_ref[...], wa_ref[...]) + _dot(yb_ref[...], wb_ref[...])
    h = DEEPNORM_ALPHA * x_ref[...] + y
    mu = jnp.mean(h, axis=1, keepdims=True)
    d = h - mu
    var = jnp.mean(d * d, axis=1, keepdims=True)
    out = d * lax.rsqrt(var + LN_EPS) * g_ref[...] + b_ref[...]
    xo_ref[...] = out
    xob_ref[...] = out.astype(BF16)


def _out_ln(ya, yb, ya_blk, yb_blk, w_out, x, ln_g, ln_b):
    s, d = x.shape
    half = w_out.shape[0] // 2
    tm = OUT_TM
    w = w_out.astype(BF16)
    return pl.pallas_call(
        _out_ln_kernel,
        grid=(s // tm,),
        in_specs=[
            pl.BlockSpec((tm, half), lambda i: (i, ya_blk)),
            pl.BlockSpec((tm, half), lambda i: (i, yb_blk)),
            pl.BlockSpec((half, d), lambda i: (0, 0)),
            pl.BlockSpec((half, d), lambda i: (1, 0)),
            pl.BlockSpec((tm, d), lambda i: (i, 0)),
            pl.BlockSpec((1, d), lambda i: (0, 0)),
            pl.BlockSpec((1, d), lambda i: (0, 0)),
        ],
        out_specs=[pl.BlockSpec((tm, d), lambda i: (i, 0)),
                   pl.BlockSpec((tm, d), lambda i: (i, 0))],
        out_shape=[jax.ShapeDtypeStruct((s, d), F32), jax.ShapeDtypeStruct((s, d), BF16)],
        compiler_params=pltpu.CompilerParams(
            dimension_semantics=("parallel",), vmem_limit_bytes=VMEM_LIMIT),
        name="out_proj_layernorm",
    )(ya, yb, w, w, x, ln_g.reshape(1, d), ln_b.reshape(1, d))


def _even_layer(x, xb, w_in, conv_w, conv_b, dt_bias, a_log, d_skip, ssm_norm_g,
                lq1, lk1, lq2, lk2, subln_g, w_out, ln_g, ln_b, lam_init):
    d = x.shape[1]
    att_w = d // 2
    ssm_w = d // 2
    heads = ssm_w // SSM_HEAD_DIM
    n_qkv = 3 * att_w
    n_gzx = att_w + ssm_w + ssm_w + 2 * SSM_GROUPS * SSM_STATE
    w = w_in.astype(BF16)
    qkv = _matmul(xb, w[:, :n_qkv], BF16, "even_in_qkv")
    gzx = _matmul(xb, w[:, n_qkv:n_qkv + n_gzx], F32, "even_in_gate_ssm")
    w_dt = jnp.pad(w[:, n_qkv + n_gzx:], ((0, 0), (0, LANES - heads)))
    dt_raw = _matmul(xb, w_dt, F32, "even_in_dt")
    lam_params = jnp.stack([lq1, lk1, lq2, lk2]).astype(F32)
    y_att = _attention(qkv, gzx, lam_params, subln_g.reshape(1, DIFF_V_DIM), lam_init)
    y_ssm = _ssd(gzx, dt_raw, conv_w, conv_b, dt_bias, a_log, d_skip, ssm_norm_g, ssm_w)
    return _out_ln(y_att, y_ssm, 0, 0, w_out, x, ln_g, ln_b)


def _odd_layer(x, xb, w_in, w_grp, b_grp, scale, w_out, ln_g, ln_b):
    vg = _matmul(xb, w_in.astype(BF16), F32, "odd_in")
    y = _pool(vg, w_grp, b_grp, scale)
    return _out_ln(y, y, 0, 1, w_out, x, ln_g, ln_b)


def kernel(x, ev_w_in, ev_conv_w, ev_conv_b, ev_dt_bias, ev_a_log, ev_d_skip, ev_ssm_norm_g, ev_lambda_q1, ev_lambda_k1, ev_lambda_q2, ev_lambda_k2, ev_subln_g, ev_w_out, od_w_in, od_w_grp, od_b_grp, od_scale, od_w_out, ln_g, ln_b):
    bsz, s, d = x.shape
    outs = []
    for b in range(bsz):
        xf = x[b]
        xb = xf
        for l in range(DEPTH):
            i = l // 2
            if l % 2 == 0:
                lam_init = 0.8 - 0.6 * math.exp(-0.3 * l)
                xf, xb = _even_layer(
                    xf, xb, ev_w_in[i], ev_conv_w[i], ev_conv_b[i], ev_dt_bias[i], ev_a_log[i],
                    ev_d_skip[i], ev_ssm_norm_g[i], ev_lambda_q1[i], ev_lambda_k1[i],
                    ev_lambda_q2[i], ev_lambda_k2[i], ev_subln_g[i], ev_w_out[i],
                    ln_g[l], ln_b[l], lam_init)
            else:
                xf, xb = _odd_layer(xf, xb, od_w_in[i], od_w_grp[i], od_b_grp[i], od_scale[i],
                                    od_w_out[i], ln_g[l], ln_b[l])
        outs.append(xf)
    return jnp.stack(outs)
```

```python
import functools
import math

import jax
import jax.numpy as jnp
from jax import lax
from jax.experimental import pallas as pl
from jax.experimental.pallas import tpu as pltpu

F32 = jnp.float32
BF16 = jnp.bfloat16

DEPTH = 4
CHUNK = 64
CHUNK_SHIFT = CHUNK.bit_length() - 1
DIFF_HEAD_DIM = 64
DIFF_V_DIM = 2 * DIFF_HEAD_DIM
SSM_HEAD_DIM = 64
HEAD_SHIFT = SSM_HEAD_DIM.bit_length() - 1
SSM_GROUPS = 2
SSM_STATE = 128
CONV_WIDTH = 4
POOL_WINDOWS = (2, 4, 8, 16)
DEEPNORM_ALPHA = (2.0 * DEPTH) ** 0.25
LN_EPS = 1e-5
RMS_EPS = 1e-5

LANES = 128
SUBLANES = 8
NEG_BIG = -1e30
FINITE_MAX = 3.0e38
VMEM_LIMIT = 56 * 1024 * 1024

ATTN_TQ = 256
ATTN_TK = 1024
ATTN_HP = 4
SSD_L = 256
MM_TM = 1024
MM_TN = 1024
OUT_TM = 256
POOL_TM = 512
POOL_HALO = 16


def _silu(x):
    return x * (1.0 / (1.0 + jnp.exp(-x)))


def _softplus(x):
    return jnp.maximum(x, 0.0) + jnp.log(1.0 + jnp.exp(-jnp.abs(x)))


def _split3(a):
    hi = a.astype(BF16)
    r1 = a - hi.astype(F32)
    mid = r1.astype(BF16)
    lo = (r1 - mid.astype(F32)).astype(BF16)
    return hi, mid, lo


def _dot(a, b):
    return jnp.dot(a, b, preferred_element_type=F32)


def _dot_nt(a, b):
    return lax.dot_general(a, b, (((1,), (1,)), ((), ())), preferred_element_type=F32)


def _matmul_kernel(x_ref, w_ref, o_ref, wb_ref):
    @pl.when(pl.program_id(1) == 0)
    def _():
        wb_ref[...] = w_ref[...].astype(BF16)

    o_ref[...] = _dot(x_ref[...], wb_ref[...]).astype(o_ref.dtype)


def _matmul(x, w, layer, col0, n, tn, out_dtype, name):
    m, k = x.shape
    tm = min(MM_TM, m)
    assert n % tn == 0 and col0 % tn == 0 and m % tm == 0
    cb = col0 // tn
    return pl.pallas_call(
        _matmul_kernel,
        grid=(n // tn, m // tm),
        in_specs=[pl.BlockSpec((tm, k), lambda j, i: (i, 0)),
                  pl.BlockSpec((None, k, tn), lambda j, i: (layer, 0, cb + j))],
        out_specs=pl.BlockSpec((tm, tn), lambda j, i: (i, j)),
        out_shape=jax.ShapeDtypeStruct((m, n), out_dtype),
        scratch_shapes=[pltpu.VMEM((k, tn), BF16)],
        compiler_params=pltpu.CompilerParams(
            dimension_semantics=("arbitrary", "arbitrary"), vmem_limit_bytes=VMEM_LIMIT),
        name=name,
    )(x, w)


def _attn_kernel(lam_ref, subg_ref, q_ref, k_ref, v_ref, g_ref, o_ref, *, tq, tk, hp, lam_init):
    qi = pl.program_id(1)
    lp = lam_ref[...]
    lam = (jnp.exp(jnp.sum(lp[0:1] * lp[1:2], axis=1, keepdims=True))
           - jnp.exp(jnp.sum(lp[2:3] * lp[3:4], axis=1, keepdims=True)) + lam_init)

    lane = lax.broadcasted_iota(jnp.int32, (tq, DIFF_V_DIM), 1)
    row = lax.broadcasted_iota(jnp.int32, (2 * tq, 1), 0)
    q_chunk = (qi * tq + jnp.where(row >= tq, row - tq, row)) >> CHUNK_SHIFT

    def stacked_q(j):
        q = q_ref[:, j * DIFF_V_DIM:(j + 1) * DIFF_V_DIM] * (DIFF_HEAD_DIM ** -0.5)
        zero = jnp.zeros_like(q)
        return jnp.concatenate([jnp.where(lane < DIFF_HEAD_DIM, q, zero),
                                jnp.where(lane >= DIFF_HEAD_DIM, q, zero)], axis=0)

    qq = [stacked_q(j) for j in range(hp)]

    def kv(kb, j, size):
        ks = pl.multiple_of(kb * size, size)
        cols = slice(j * DIFF_V_DIM, (j + 1) * DIFF_V_DIM)
        k = k_ref[pl.ds(ks, size), cols]
        v1 = jnp.concatenate([v_ref[pl.ds(ks, size), cols],
                              jnp.ones((size, DIFF_V_DIM), BF16)], axis=1)
        return ks, k, v1

    def exact_block(kb, j, m, acc, masked):
        ks, k, v1 = kv(kb, j, tq)
        s = _dot_nt(qq[j], k)
        if masked:
            k_chunk = (ks + lax.broadcasted_iota(jnp.int32, (1, tq), 1)) >> CHUNK_SHIFT
            s = jnp.where(k_chunk <= q_chunk, s, NEG_BIG)
        m_new = jnp.maximum(m, jnp.max(s, axis=1, keepdims=True))
        p = jnp.exp(s - m_new).astype(BF16)
        return m_new, jnp.exp(m - m_new) * acc + _dot(p, v1)

    def fast_block(kb, j, ref, acc):
        ks, k, v1 = kv(kb, j, tk)
        before = (ks + lax.broadcasted_iota(jnp.int32, (1, tk), 1)) < qi * tq
        p = jnp.exp(jnp.where(before, _dot_nt(qq[j], k) - ref, NEG_BIG)).astype(BF16)
        return acc + _dot(p, v1)

    n_fast = (qi * tq + tk - 1) // tk
    first = [exact_block(qi, j, jnp.full((2 * tq, 1), NEG_BIG, F32),
                         jnp.zeros((2 * tq, 2 * DIFF_V_DIM), F32), True) for j in range(hp)]

    def finish(j, acc):
        cols = slice(j * DIFF_V_DIM, (j + 1) * DIFF_V_DIM)
        o = acc[:, :DIFF_V_DIM] * (1.0 / acc[:, DIFF_V_DIM:DIFF_V_DIM + 1])
        o = o[:tq] - lam * o[tq:]
        o = o * lax.rsqrt(jnp.mean(o * o, axis=1, keepdims=True) + RMS_EPS)
        o = o * subg_ref[...] * (1.0 - lam_init)
        o_ref[:, cols] = (o * _silu(g_ref[:, cols])).astype(o_ref.dtype)

    def fast_step(kb, accs):
        return tuple(fast_block(kb, j, first[j][0], accs[j]) for j in range(hp))

    accs = lax.fori_loop(0, n_fast, fast_step, tuple(f[1] for f in first))
    overflow = jnp.float32(0.0)
    for j in range(hp):
        finish(j, accs[j])
        overflow = jnp.maximum(overflow, jnp.max(jnp.where(jnp.abs(accs[j]) < FINITE_MAX, 0.0, 1.0)))

    @pl.when(overflow > 0.0)
    def _():
        def exact_step(kb, carry):
            return tuple(exact_block(kb, j, *carry[j], False) for j in range(hp))
        carry = lax.fori_loop(0, qi, exact_step, tuple(first))
        for j in range(hp):
            finish(j, carry[j][1])


def _attention(qkv, gzx, lam_params, subln_g, lam_init):
    s = qkv.shape[0]
    heads = qkv.shape[1] // (3 * DIFF_V_DIM)
    tq, tk, hp = ATTN_TQ, ATTN_TK, ATTN_HP
    groups = heads // hp
    bw = hp * DIFF_V_DIM
    kern = functools.partial(_attn_kernel, tq=tq, tk=tk, hp=hp, lam_init=lam_init)
    return pl.pallas_call(
        kern,
        grid=(groups, s // tq),
        in_specs=[
            pl.BlockSpec((4, DIFF_HEAD_DIM), lambda h, i: (0, 0)),
            pl.BlockSpec((1, DIFF_V_DIM), lambda h, i: (0, 0)),
            pl.BlockSpec((tq, bw), lambda h, i: (i, h)),
            pl.BlockSpec((s, bw), lambda h, i: (0, groups + h)),
            pl.BlockSpec((s, bw), lambda h, i: (0, 2 * groups + h)),
            pl.BlockSpec((tq, bw), lambda h, i: (i, h)),
        ],
        out_specs=pl.BlockSpec((tq, bw), lambda h, i: (i, h)),
        out_shape=jax.ShapeDtypeStruct((s, heads * DIFF_V_DIM), BF16),
        compiler_params=pltpu.CompilerParams(
            dimension_semantics=("parallel", "parallel"), vmem_limit_bytes=VMEM_LIMIT),
        name="diff_attention",
    )(lam_params, subln_g, qkv, qkv, qkv, gzx)


def _conv_silu(cur_ref, tail_ref, ext_ref, w_ref, b_ref, rows):
    ext_ref[0:SUBLANES, :] = tail_ref[...]
    ext_ref[SUBLANES:SUBLANES + rows, :] = cur_ref[...]
    tail_ref[...] = cur_ref[rows - SUBLANES:rows, :]
    acc = b_ref[...]
    for t in range(CONV_WIDTH):
        start = SUBLANES - (CONV_WIDTH - 1) + t
        acc = acc + w_ref[t:t + 1, :] * ext_ref[start:start + rows, :]
    return _silu(acc)


def _ssd_kernel(z_ref, xs_ref, bc_ref, dt_ref, cwx_ref, cbx_ref, cwbc_ref, cbbc_ref,
                dtb_ref, alog_ref, dskip_ref, ng_ref, y_ref,
                state_ref, tailx_ref, tailbc_ref, extx_ref, extbc_ref, *, rows):
    c = pl.program_id(0)
    width = xs_ref.shape[1]
    gw = width // SSM_GROUPS
    heads_per_pair = LANES // SSM_HEAD_DIM

    @pl.when(c == 0)
    def _():
        state_ref[...] = jnp.zeros_like(state_ref)
        tailx_ref[...] = jnp.zeros_like(tailx_ref)
        tailbc_ref[...] = jnp.zeros_like(tailbc_ref)

    xs = _conv_silu(xs_ref, tailx_ref, extx_ref, cwx_ref, cbx_ref, rows)
    bcv = _conv_silu(bc_ref, tailbc_ref, extbc_ref, cwbc_ref, cbbc_ref, rows)

    dtc = _softplus(dt_ref[...] + dtb_ref[...])
    adt = -jnp.exp(alog_ref[...]) * dtc
    ri = lax.broadcasted_iota(jnp.int32, (rows, rows), 0)
    ci = lax.broadcasted_iota(jnp.int32, (rows, rows), 1)
    causal = ci <= ri
    tri = jnp.where(causal, 1.0, 0.0).astype(BF16)
    csc = sum(_dot(tri, part) for part in _split3(adt))
    cs_t = csc.T

    er = lax.broadcasted_iota(jnp.int32, (LANES, width), 0)
    ec = lax.broadcasted_iota(jnp.int32, (LANES, width), 1)
    expand = jnp.where((ec >> HEAD_SHIFT) == er, 1.0, 0.0).astype(BF16)
    dt_e = sum(_dot(part, expand) for part in _split3(dtc))
    cs_e = sum(_dot(part, expand) for part in _split3(csc))
    cs_last = cs_e[rows - 1:rows, :]

    xdt = xs * dt_e
    xdt_b = xdt.astype(BF16)
    xd_b = (xdt * jnp.exp(cs_last - cs_e)).astype(BF16)
    ecs = jnp.exp(cs_e)
    chunk_decay = jnp.exp(cs_last)
    lane = lax.broadcasted_iota(jnp.int32, (rows, LANES), 1)

    for g in range(SSM_GROUPS):
        gsl = slice(g * gw, (g + 1) * gw)
        b_f = bcv[:, g * SSM_STATE:(g + 1) * SSM_STATE]
        c_b = bcv[:, (SSM_GROUPS + g) * SSM_STATE:(SSM_GROUPS + g + 1) * SSM_STATE].astype(BF16)
        scores = _dot_nt(c_b, b_f.astype(BF16))
        st = state_ref[g]
        y_off = _dot(c_b, st.astype(BF16)) * ecs[:, gsl]
        state_ref[g] = st * chunk_decay[:, gsl] + _dot(b_f.T.astype(BF16), xd_b[:, gsl])

        y_diag = []
        for pair in range(gw // LANES):
            col0 = g * gw + pair * LANES
            xpair = xdt_b[:, col0:col0 + LANES]
            parts = []
            for hh in range(heads_per_pair):
                h = col0 // SSM_HEAD_DIM + hh
                seg = cs_e[:, h * SSM_HEAD_DIM:h * SSM_HEAD_DIM + 1] - cs_t[h:h + 1, :]
                decay = jnp.exp(jnp.where(causal, seg, NEG_BIG))
                parts.append(_dot((scores * decay).astype(BF16), xpair))
            y_diag.append(jnp.where(lane < SSM_HEAD_DIM, parts[0], parts[1]))
        y = jnp.concatenate(y_diag, axis=1) + y_off + dskip_ref[:, gsl] * xs[:, gsl]
        y = y * _silu(z_ref[:, gsl])
        y = y * lax.rsqrt(jnp.mean(y * y, axis=1, keepdims=True) + RMS_EPS)
        y_ref[:, gsl] = (y * ng_ref[:, gsl]).astype(y_ref.dtype)


def _ssd(gzx, bc, dt_raw, conv_w, conv_b, dt_bias, a_log, d_skip, norm_g, width):
    s = gzx.shape[0]
    rows = SSD_L
    bcw = 2 * SSM_GROUPS * SSM_STATE
    heads = width // SSM_HEAD_DIM

    def pad_heads(p):
        return jnp.pad(p.astype(F32), (0, LANES - heads)).reshape(1, LANES)

    def per_channel(p):
        return jnp.repeat(p.astype(F32), SSM_HEAD_DIM).reshape(1, width)

    z_blk = 1
    xs_blk = 2
    const = lambda c: (0, 0)
    kern = functools.partial(_ssd_kernel, rows=rows)
    return pl.pallas_call(
        kern,
        grid=(s // rows,),
        in_specs=[
            pl.BlockSpec((rows, width), lambda c: (c, z_blk)),
            pl.BlockSpec((rows, width), lambda c: (c, xs_blk)),
            pl.BlockSpec((rows, bcw), lambda c: (c, 0)),
            pl.BlockSpec((rows, LANES), lambda c: (c, 0)),
            pl.BlockSpec((CONV_WIDTH, width), const),
            pl.BlockSpec((1, width), const),
            pl.BlockSpec((CONV_WIDTH, bcw), const),
            pl.BlockSpec((1, bcw), const),
            pl.BlockSpec((1, LANES), const),
            pl.BlockSpec((1, LANES), const),
            pl.BlockSpec((1, width), const),
            pl.BlockSpec((1, width), const),
        ],
        out_specs=pl.BlockSpec((rows, width), lambda c: (c, 0)),
        out_shape=jax.ShapeDtypeStruct((s, width), BF16),
        scratch_shapes=[
            pltpu.VMEM((SSM_GROUPS, SSM_STATE, width // SSM_GROUPS), F32),
            pltpu.VMEM((SUBLANES, width), F32),
            pltpu.VMEM((SUBLANES, bcw), F32),
            pltpu.VMEM((SUBLANES + rows, width), F32),
            pltpu.VMEM((SUBLANES + rows, bcw), F32),
        ],
        compiler_params=pltpu.CompilerParams(
            dimension_semantics=("arbitrary",), vmem_limit_bytes=VMEM_LIMIT),
        name="ssd_scan",
    )(gzx, gzx, bc, dt_raw,
      conv_w[:, :width], conv_b[:width].reshape(1, width),
      conv_w[:, width:], conv_b[width:].reshape(1, bcw),
      pad_heads(dt_bias), pad_heads(a_log), per_channel(d_skip), norm_g.reshape(1, width))


def _pool_kernel(v_ref, g_ref, wg_ref, bg_ref, sc_ref, y_ref, tail_ref, ext_ref, wb_ref, *, rows):
    i = pl.program_id(0)
    gc = wg_ref.shape[1]

    @pl.when(i == 0)
    def _():
        tail_ref[...] = jnp.zeros_like(tail_ref)
        wb_ref[...] = wg_ref[...].astype(BF16)

    ext_ref[0:POOL_HALO, :] = tail_ref[...]
    ext_ref[POOL_HALO:POOL_HALO + rows, :] = v_ref[...]
    tail_ref[...] = v_ref[rows - POOL_HALO:rows, :]
    pos = (i * rows + lax.broadcasted_iota(jnp.int32, (rows, 1), 0) + 1).astype(F32)

    for gi, w in enumerate(POOL_WINDOWS):
        cols = slice(gi * gc, (gi + 1) * gc)
        v = v_ref[:, cols]
        acc = v
        for d in range(1, w):
            acc = acc + ext_ref[POOL_HALO - d:POOL_HALO - d + rows, cols]
        pooled = acc * (1.0 / jnp.minimum(pos, float(w))) - v
        m = _dot(pooled.astype(BF16), wb_ref[gi]) + bg_ref[:, cols]
        y_ref[:, cols] = (m * sc_ref[:, cols] * _silu(g_ref[:, cols])).astype(y_ref.dtype)


def _pool(vg, w_grp, layer, b_grp, scale):
    s = vg.shape[0]
    width = vg.shape[1] // 2
    rows = POOL_TM
    _, ng, gc, _ = w_grp.shape
    kern = functools.partial(_pool_kernel, rows=rows)
    return pl.pallas_call(
        kern,
        grid=(s // rows,),
        in_specs=[
            pl.BlockSpec((rows, width), lambda i: (i, 0)),
            pl.BlockSpec((rows, width), lambda i: (i, 1)),
            pl.BlockSpec((None, ng, gc, gc), lambda i: (layer, 0, 0, 0)),
            pl.BlockSpec((1, width), lambda i: (0, 0)),
            pl.BlockSpec((1, width), lambda i: (0, 0)),
        ],
        out_specs=pl.BlockSpec((rows, width), lambda i: (i, 0)),
        out_shape=jax.ShapeDtypeStruct((s, width), BF16),
        scratch_shapes=[pltpu.VMEM((POOL_HALO, width), F32),
                        pltpu.VMEM((POOL_HALO + rows, width), F32),
                        pltpu.VMEM((ng, gc, gc), BF16)],
        compiler_params=pltpu.CompilerParams(
            dimension_semantics=("arbitrary",), vmem_limit_bytes=VMEM_LIMIT),
        name="pool_mixer",
    )(vg, vg, w_grp, b_grp.reshape(1, width), scale.reshape(1, width))


def _out_ln_kernel(ya_ref, yb_ref, w_ref, x_ref, g_ref, b_ref, xo_ref, xob_ref, wb_ref):
    half = ya_ref.shape[1]

    @pl.when(pl.program_id(0) == 0)
    def _():
        wb_ref[...] = w_ref[...].astype(BF16)

    y = _dot(ya_ref[...], wb_ref[:half, :]) + _dot(yb_ref[...], wb_ref[half:, :])
    h = DEEPNORM_ALPHA * x_ref[...] + y
    mu = jnp.mean(h, axis=1, keepdims=True)
    d = h - mu
    var = jnp.mean(d * d, axis=1, keepdims=True)
    out = d * lax.rsqrt(var + LN_EPS) * g_ref[...] + b_ref[...]
    xo_ref[...] = out
    xob_ref[...] = out.astype(BF16)


def _out_ln(ya, yb, ya_blk, yb_blk, w_out, layer, x, ln_g, ln_b):
    s, d = x.shape
    half = w_out.shape[1] // 2
    tm = OUT_TM
    return pl.pallas_call(
        _out_ln_kernel,
        grid=(s // tm,),
        in_specs=[
            pl.BlockSpec((tm, half), lambda i: (i, ya_blk)),
            pl.BlockSpec((tm, half), lambda i: (i, yb_blk)),
            pl.BlockSpec((None, 2 * half, d), lambda i: (layer, 0, 0),
                         pipeline_mode=pl.Buffered(1)),
            pl.BlockSpec((tm, d), lambda i: (i, 0)),
            pl.BlockSpec((1, d), lambda i: (0, 0)),
            pl.BlockSpec((1, d), lambda i: (0, 0)),
        ],
        out_specs=[pl.BlockSpec((tm, d), lambda i: (i, 0)),
                   pl.BlockSpec((tm, d), lambda i: (i, 0))],
        out_shape=[jax.ShapeDtypeStruct((s, d), F32), jax.ShapeDtypeStruct((s, d), BF16)],
        scratch_shapes=[pltpu.VMEM((2 * half, d), BF16)],
        compiler_params=pltpu.CompilerParams(
            dimension_semantics=("arbitrary",), vmem_limit_bytes=VMEM_LIMIT),
        name="out_proj_layernorm",
    )(ya, yb, w_out, x, ln_g.reshape(1, d), ln_b.reshape(1, d))


def _even_layer(x, xb, i, w_in, conv_w, conv_b, dt_bias, a_log, d_skip, ssm_norm_g,
                lq1, lk1, lq2, lk2, subln_g, w_out, ln_g, ln_b, lam_init):
    d = x.shape[1]
    att_w = d // 2
    ssm_w = d // 2
    heads = ssm_w // SSM_HEAD_DIM
    n_qkv = 3 * att_w
    n_gzx = att_w + ssm_w + ssm_w
    n_bc = 2 * SSM_GROUPS * SSM_STATE
    qkv = _matmul(xb, w_in, i, 0, n_qkv, MM_TN, BF16, "even_in_qkv")
    gzx = _matmul(xb, w_in, i, n_qkv, n_gzx, MM_TN, F32, "even_in_gate_ssm")
    bc = _matmul(xb, w_in, i, n_qkv + n_gzx, n_bc, n_bc, F32, "even_in_bc")
    w_dt = jnp.pad(w_in[i, :, n_qkv + n_gzx + n_bc:], ((0, 0), (0, LANES - heads)))
    dt_raw = _matmul(xb, w_dt[None], 0, 0, LANES, LANES, F32, "even_in_dt")
    lam_params = jnp.stack([lq1, lk1, lq2, lk2]).astype(F32)
    y_att = _attention(qkv, gzx, lam_params, subln_g.reshape(1, DIFF_V_DIM), lam_init)
    y_ssm = _ssd(gzx, bc, dt_raw, conv_w, conv_b, dt_bias, a_log, d_skip, ssm_norm_g, ssm_w)
    return _out_ln(y_att, y_ssm, 0, 0, w_out, i, x, ln_g, ln_b)


def _odd_layer(x, xb, i, w_in, w_grp, b_grp, scale, w_out, ln_g, ln_b):
    vg = _matmul(xb, w_in, i, 0, w_in.shape[2], MM_TN, F32, "odd_in")
    y = _pool(vg, w_grp, i, b_grp, scale)
    return _out_ln(y, y, 0, 1, w_out, i, x, ln_g, ln_b)


def kernel(x, ev_w_in, ev_conv_w, ev_conv_b, ev_dt_bias, ev_a_log, ev_d_skip, ev_ssm_norm_g, ev_lambda_q1, ev_lambda_k1, ev_lambda_q2, ev_lambda_k2, ev_subln_g, ev_w_out, od_w_in, od_w_grp, od_b_grp, od_scale, od_w_out, ln_g, ln_b):
    bsz, s, d = x.shape
    outs = []
    for b in range(bsz):
        xf = x[b]
        xb = xf.astype(BF16)
        for l in range(DEPTH):
            i = l // 2
            if l % 2 == 0:
                lam_init = 0.8 - 0.6 * math.exp(-0.3 * l)
                xf, xb = _even_layer(
                    xf, xb, i, ev_w_in, ev_conv_w[i], ev_conv_b[i], ev_dt_bias[i], ev_a_log[i],
                    ev_d_skip[i], ev_ssm_norm_g[i], ev_lambda_q1[i], ev_lambda_k1[i],
                    ev_lambda_q2[i], ev_lambda_k2[i], ev_subln_g[i], ev_w_out,
                    ln_g[l], ln_b[l], lam_init)
            else:
                xf, xb = _odd_layer(xf, xb, i, od_w_in, od_w_grp, od_b_grp[i], od_scale[i],
                                    od_w_out, ln_g[l], ln_b[l])
        outs.append(xf)
    return jnp.stack(outs)
```

```python
import functools
import math

import jax
import jax.numpy as jnp
from jax import lax
from jax.experimental import pallas as pl
from jax.experimental.pallas import tpu as pltpu

F32 = jnp.float32
BF16 = jnp.bfloat16

DEPTH = 4
CHUNK = 64
CHUNK_SHIFT = CHUNK.bit_length() - 1
DIFF_HEAD_DIM = 64
DIFF_V_DIM = 2 * DIFF_HEAD_DIM
SSM_HEAD_DIM = 64
HEAD_SHIFT = SSM_HEAD_DIM.bit_length() - 1
SSM_GROUPS = 2
SSM_STATE = 128
CONV_WIDTH = 4
POOL_WINDOWS = (2, 4, 8, 16)
DEEPNORM_ALPHA = (2.0 * DEPTH) ** 0.25
LN_EPS = 1e-5
RMS_EPS = 1e-5

LANES = 128
SUBLANES = 8
NEG_BIG = -1e30
FINITE_MAX = 3.0e38
VMEM_LIMIT = 56 * 1024 * 1024

ATTN_TQ = 256
ATTN_TK = 1024
ATTN_HP = 4
SSD_L = 256
MM_TM = 1024
MM_TN = 1024
OUT_TM = 512
POOL_TM = 512
POOL_HALO = 24


def _silu(x):
    return x * (1.0 / (1.0 + jnp.exp(-x)))


def _softplus(x):
    return jnp.maximum(x, 0.0) + jnp.log(1.0 + jnp.exp(-jnp.abs(x)))


def _split3(a):
    hi = a.astype(BF16)
    r1 = a - hi.astype(F32)
    mid = r1.astype(BF16)
    lo = (r1 - mid.astype(F32)).astype(BF16)
    return hi, mid, lo


def _dot(a, b):
    return jnp.dot(a, b, preferred_element_type=F32)


def _dot_nt(a, b):
    return lax.dot_general(a, b, (((1,), (1,)), ((), ())), preferred_element_type=F32)


def _matmul_kernel(x_ref, w_ref, o_ref, wb_ref, *, w_is_nk):
    @pl.when(pl.program_id(1) == 0)
    def _():
        wb_ref[...] = w_ref[...].astype(BF16)

    dot = _dot_nt if w_is_nk else _dot
    o_ref[...] = dot(x_ref[...], wb_ref[...]).astype(o_ref.dtype)


def _matmul(x, w, layer, col0, n, tn, out_dtype, name, w_is_nk=False):
    m, k = x.shape
    tm = min(MM_TM, m)
    assert n % tn == 0 and col0 % tn == 0 and m % tm == 0
    cb = col0 // tn
    if w_is_nk:
        w_spec = pl.BlockSpec((None, tn, k), lambda j, i: (layer, cb + j, 0))
    else:
        w_spec = pl.BlockSpec((None, k, tn), lambda j, i: (layer, 0, cb + j))
    return pl.pallas_call(
        functools.partial(_matmul_kernel, w_is_nk=w_is_nk),
        grid=(n // tn, m // tm),
        in_specs=[pl.BlockSpec((tm, k), lambda j, i: (i, 0)), w_spec],
        out_specs=pl.BlockSpec((tm, tn), lambda j, i: (i, j)),
        out_shape=jax.ShapeDtypeStruct((m, n), out_dtype),
        scratch_shapes=[pltpu.VMEM((tn, k) if w_is_nk else (k, tn), BF16)],
        compiler_params=pltpu.CompilerParams(
            dimension_semantics=("arbitrary", "arbitrary"), vmem_limit_bytes=VMEM_LIMIT),
        name=name,
    )(x, w)


def _attn_kernel(lam_ref, subg_ref, q_ref, k_ref, v_ref, g_ref, o_ref, *, tq, tk, hp, lam_init):
    qi = pl.program_id(1)
    lp = lam_ref[...]
    lam = (jnp.exp(jnp.sum(lp[0:1] * lp[1:2], axis=1, keepdims=True))
           - jnp.exp(jnp.sum(lp[2:3] * lp[3:4], axis=1, keepdims=True)) + lam_init)

    lane = lax.broadcasted_iota(jnp.int32, (tq, DIFF_V_DIM), 1)
    row = lax.broadcasted_iota(jnp.int32, (2 * tq, 1), 0)
    q_chunk = (qi * tq + jnp.where(row >= tq, row - tq, row)) >> CHUNK_SHIFT

    def stacked_q(j):
        q = q_ref[:, j * DIFF_V_DIM:(j + 1) * DIFF_V_DIM] * (DIFF_HEAD_DIM ** -0.5)
        zero = jnp.zeros_like(q)
        return jnp.concatenate([jnp.where(lane < DIFF_HEAD_DIM, q, zero),
                                jnp.where(lane >= DIFF_HEAD_DIM, q, zero)], axis=0)

    qq = [stacked_q(j) for j in range(hp)]

    def kv(kb, j, size):
        ks = pl.multiple_of(kb * size, size)
        cols = slice(j * DIFF_V_DIM, (j + 1) * DIFF_V_DIM)
        k = k_ref[pl.ds(ks, size), cols]
        v1 = jnp.concatenate([v_ref[pl.ds(ks, size), cols],
                              jnp.ones((size, DIFF_V_DIM), BF16)], axis=1)
        return ks, k, v1

    def exact_block(kb, j, m, acc, masked):
        ks, k, v1 = kv(kb, j, tq)
        s = _dot_nt(qq[j], k)
        if masked:
            k_chunk = (ks + lax.broadcasted_iota(jnp.int32, (1, tq), 1)) >> CHUNK_SHIFT
            s = jnp.where(k_chunk <= q_chunk, s, NEG_BIG)
        m_new = jnp.maximum(m, jnp.max(s, axis=1, keepdims=True))
        p = jnp.exp(s - m_new).astype(BF16)
        return m_new, jnp.exp(m - m_new) * acc + _dot(p, v1)

    def fast_block(kb, j, ref, acc):
        ks, k, v1 = kv(kb, j, tk)
        before = (ks + lax.broadcasted_iota(jnp.int32, (1, tk), 1)) < qi * tq
        p = jnp.exp(jnp.where(before, _dot_nt(qq[j], k) - ref, NEG_BIG)).astype(BF16)
        return acc + _dot(p, v1)

    n_fast = (qi * tq + tk - 1) // tk
    first = [exact_block(qi, j, jnp.full((2 * tq, 1), NEG_BIG, F32),
                         jnp.zeros((2 * tq, 2 * DIFF_V_DIM), F32), True) for j in range(hp)]

    def finish(j, acc):
        cols = slice(j * DIFF_V_DIM, (j + 1) * DIFF_V_DIM)
        o = acc[:, :DIFF_V_DIM] * (1.0 / acc[:, DIFF_V_DIM:DIFF_V_DIM + 1])
        o = o[:tq] - lam * o[tq:]
        o = o * lax.rsqrt(jnp.mean(o * o, axis=1, keepdims=True) + RMS_EPS)
        o = o * subg_ref[...] * (1.0 - lam_init)
        o_ref[:, cols] = (o * _silu(g_ref[:, cols])).astype(o_ref.dtype)

    def fast_step(kb, accs):
        return tuple(fast_block(kb, j, first[j][0], accs[j]) for j in range(hp))

    accs = lax.fori_loop(0, n_fast, fast_step, tuple(f[1] for f in first))
    overflow = jnp.float32(0.0)
    for j in range(hp):
        finish(j, accs[j])
        overflow = jnp.maximum(overflow, jnp.max(jnp.where(jnp.abs(accs[j]) < FINITE_MAX, 0.0, 1.0)))

    @pl.when(overflow > 0.0)
    def _():
        def exact_step(kb, carry):
            return tuple(exact_block(kb, j, *carry[j], False) for j in range(hp))
        carry = lax.fori_loop(0, qi, exact_step, tuple(first))
        for j in range(hp):
            finish(j, carry[j][1])


def _attention(qkv, gzx, lam_params, subln_g, lam_init):
    s = qkv.shape[0]
    heads = qkv.shape[1] // (3 * DIFF_V_DIM)
    tq, tk, hp = ATTN_TQ, ATTN_TK, ATTN_HP
    groups = heads // hp
    bw = hp * DIFF_V_DIM
    kern = functools.partial(_attn_kernel, tq=tq, tk=tk, hp=hp, lam_init=lam_init)
    return pl.pallas_call(
        kern,
        grid=(groups, s // tq),
        in_specs=[
            pl.BlockSpec((4, DIFF_HEAD_DIM), lambda h, i: (0, 0)),
            pl.BlockSpec((1, DIFF_V_DIM), lambda h, i: (0, 0)),
            pl.BlockSpec((tq, bw), lambda h, i: (i, h)),
            pl.BlockSpec((s, bw), lambda h, i: (0, groups + h)),
            pl.BlockSpec((s, bw), lambda h, i: (0, 2 * groups + h)),
            pl.BlockSpec((tq, bw), lambda h, i: (i, h)),
        ],
        out_specs=pl.BlockSpec((tq, bw), lambda h, i: (i, h)),
        out_shape=jax.ShapeDtypeStruct((s, heads * DIFF_V_DIM), BF16),
        compiler_params=pltpu.CompilerParams(
            dimension_semantics=("parallel", "parallel"), vmem_limit_bytes=VMEM_LIMIT),
        name="diff_attention",
    )(lam_params, subln_g, qkv, qkv, qkv, gzx)


def _conv_silu(cur_ref, tail_ref, ext_ref, w_ref, b_ref, rows):
    ext_ref[0:SUBLANES, :] = tail_ref[...]
    ext_ref[SUBLANES:SUBLANES + rows, :] = cur_ref[...]
    tail_ref[...] = cur_ref[rows - SUBLANES:rows, :]
    acc = b_ref[...]
    for t in range(CONV_WIDTH):
        start = SUBLANES - (CONV_WIDTH - 1) + t
        acc = acc + w_ref[t:t + 1, :] * ext_ref[start:start + rows, :]
    return _silu(acc)


def _ssd_kernel(z_ref, xs_ref, bc_ref, dt_ref, cwx_ref, cbx_ref, cwbc_ref, cbbc_ref,
                dtb_ref, alog_ref, dskip_ref, ng_ref, y_ref,
                state_ref, tailx_ref, tailbc_ref, extx_ref, extbc_ref, *, rows):
    c = pl.program_id(0)
    width = xs_ref.shape[1]
    gw = width // SSM_GROUPS
    heads_per_pair = LANES // SSM_HEAD_DIM

    @pl.when(c == 0)
    def _():
        state_ref[...] = jnp.zeros_like(state_ref)
        tailx_ref[...] = jnp.zeros_like(tailx_ref)
        tailbc_ref[...] = jnp.zeros_like(tailbc_ref)

    xs = _conv_silu(xs_ref, tailx_ref, extx_ref, cwx_ref, cbx_ref, rows)
    bcv = _conv_silu(bc_ref, tailbc_ref, extbc_ref, cwbc_ref, cbbc_ref, rows)

    dtc = _softplus(dt_ref[...] + dtb_ref[...])
    adt = -jnp.exp(alog_ref[...]) * dtc
    ri = lax.broadcasted_iota(jnp.int32, (rows, rows), 0)
    ci = lax.broadcasted_iota(jnp.int32, (rows, rows), 1)
    causal = ci <= ri
    tri = jnp.where(causal, 1.0, 0.0).astype(BF16)
    csc = sum(_dot(tri, part) for part in _split3(adt))
    cs_t = csc.T

    er = lax.broadcasted_iota(jnp.int32, (LANES, width), 0)
    ec = lax.broadcasted_iota(jnp.int32, (LANES, width), 1)
    expand = jnp.where((ec >> HEAD_SHIFT) == er, 1.0, 0.0).astype(BF16)
    dt_e = sum(_dot(part, expand) for part in _split3(dtc))
    cs_e = sum(_dot(part, expand) for part in _split3(csc))
    cs_last = cs_e[rows - 1:rows, :]

    xdt = xs * dt_e
    xdt_b = xdt.astype(BF16)
    xd_b = (xdt * jnp.exp(cs_last - cs_e)).astype(BF16)
    ecs = jnp.exp(cs_e)
    chunk_decay = jnp.exp(cs_last)
    lane = lax.broadcasted_iota(jnp.int32, (rows, LANES), 1)

    for g in range(SSM_GROUPS):
        gsl = slice(g * gw, (g + 1) * gw)
        b_f = bcv[:, g * SSM_STATE:(g + 1) * SSM_STATE]
        c_b = bcv[:, (SSM_GROUPS + g) * SSM_STATE:(SSM_GROUPS + g + 1) * SSM_STATE].astype(BF16)
        scores = _dot_nt(c_b, b_f.astype(BF16))
        st = state_ref[g]
        y_off = _dot(c_b, st.astype(BF16)) * ecs[:, gsl]
        state_ref[g] = st * chunk_decay[:, gsl] + _dot(b_f.T.astype(BF16), xd_b[:, gsl])

        y_diag = []
        for pair in range(gw // LANES):
            col0 = g * gw + pair * LANES
            xpair = xdt_b[:, col0:col0 + LANES]
            parts = []
            for hh in range(heads_per_pair):
                h = col0 // SSM_HEAD_DIM + hh
                seg = cs_e[:, h * SSM_HEAD_DIM:h * SSM_HEAD_DIM + 1] - cs_t[h:h + 1, :]
                decay = jnp.exp(jnp.where(causal, seg, NEG_BIG))
                parts.append(_dot((scores * decay).astype(BF16), xpair))
            y_diag.append(jnp.where(lane < SSM_HEAD_DIM, parts[0], parts[1]))
        y = jnp.concatenate(y_diag, axis=1) + y_off + dskip_ref[:, gsl] * xs[:, gsl]
        y = y * _silu(z_ref[:, gsl])
        y = y * lax.rsqrt(jnp.mean(y * y, axis=1, keepdims=True) + RMS_EPS)
        y_ref[:, gsl] = (y * ng_ref[:, gsl]).astype(y_ref.dtype)


def _ssd(gzx, bc, dt_raw, conv_w, conv_b, dt_bias, a_log, d_skip, norm_g, width):
    s = gzx.shape[0]
    rows = SSD_L
    bcw = 2 * SSM_GROUPS * SSM_STATE
    heads = width // SSM_HEAD_DIM

    def pad_heads(p):
        return jnp.pad(p.astype(F32), (0, LANES - heads)).reshape(1, LANES)

    def per_channel(p):
        return jnp.repeat(p.astype(F32), SSM_HEAD_DIM).reshape(1, width)

    z_blk = 1
    xs_blk = 2
    const = lambda c: (0, 0)
    kern = functools.partial(_ssd_kernel, rows=rows)
    return pl.pallas_call(
        kern,
        grid=(s // rows,),
        in_specs=[
            pl.BlockSpec((rows, width), lambda c: (c, z_blk)),
            pl.BlockSpec((rows, width), lambda c: (c, xs_blk)),
            pl.BlockSpec((rows, bcw), lambda c: (c, 0)),
            pl.BlockSpec((rows, LANES), lambda c: (c, 0)),
            pl.BlockSpec((CONV_WIDTH, width), const),
            pl.BlockSpec((1, width), const),
            pl.BlockSpec((CONV_WIDTH, bcw), const),
            pl.BlockSpec((1, bcw), const),
            pl.BlockSpec((1, LANES), const),
            pl.BlockSpec((1, LANES), const),
            pl.BlockSpec((1, width), const),
            pl.BlockSpec((1, width), const),
        ],
        out_specs=pl.BlockSpec((rows, width), lambda c: (c, 0)),
        out_shape=jax.ShapeDtypeStruct((s, width), BF16),
        scratch_shapes=[
            pltpu.VMEM((SSM_GROUPS, SSM_STATE, width // SSM_GROUPS), F32),
            pltpu.VMEM((SUBLANES, width), F32),
            pltpu.VMEM((SUBLANES, bcw), F32),
            pltpu.VMEM((SUBLANES + rows, width), F32),
            pltpu.VMEM((SUBLANES + rows, bcw), F32),
        ],
        compiler_params=pltpu.CompilerParams(
            dimension_semantics=("arbitrary",), vmem_limit_bytes=VMEM_LIMIT),
        name="ssd_scan",
    )(gzx, gzx, bc, dt_raw,
      conv_w[:, :width], conv_b[:width].reshape(1, width),
      conv_w[:, width:], conv_b[width:].reshape(1, bcw),
      pad_heads(dt_bias), pad_heads(a_log), per_channel(d_skip), norm_g.reshape(1, width))


def _pool_kernel(v_ref, g_ref, wg_ref, bg_ref, sc_ref, y_ref, tail_ref, ext_ref, wb_ref,
                 pa_ref, pb_ref, *, rows):
    i = pl.program_id(0)
    gc = wg_ref.shape[1]
    top = POOL_HALO + rows

    @pl.when(i == 0)
    def _():
        tail_ref[...] = jnp.zeros_like(tail_ref)
        wb_ref[...] = wg_ref[...].astype(BF16)
        pa_ref[0:SUBLANES, :] = jnp.zeros((SUBLANES, gc), F32)
        pb_ref[0:SUBLANES, :] = jnp.zeros((SUBLANES, gc), F32)

    ext_ref[0:POOL_HALO, :] = tail_ref[...]
    ext_ref[POOL_HALO:top, :] = v_ref[...]
    tail_ref[...] = v_ref[rows - POOL_HALO:rows, :]
    pos = (i * rows + lax.broadcasted_iota(jnp.int32, (rows, 1), 0) + 1).astype(F32)

    for gi, w in enumerate(POOL_WINDOWS):
        cols = slice(gi * gc, (gi + 1) * gc)
        v = v_ref[:, cols]
        src, span = ext_ref.at[:, cols], 1
        for dst in (pa_ref, pb_ref, pa_ref, pb_ref):
            if span == w:
                break
            dst[SUBLANES:top, :] = (src[SUBLANES:top, :]
                                    + src[SUBLANES - span:top - span, :])
            src, span = dst, 2 * span
        acc = src[POOL_HALO:top, :]
        pooled = acc * (1.0 / jnp.minimum(pos, float(w))) - v
        m = _dot(pooled.astype(BF16), wb_ref[gi]) + bg_ref[:, cols]
        y_ref[:, cols] = (m * sc_ref[:, cols] * _silu(g_ref[:, cols])).astype(y_ref.dtype)


def _pool(vg, w_grp, layer, b_grp, scale):
    s = vg.shape[0]
    width = vg.shape[1] // 2
    rows = POOL_TM
    _, ng, gc, _ = w_grp.shape
    kern = functools.partial(_pool_kernel, rows=rows)
    return pl.pallas_call(
        kern,
        grid=(s // rows,),
        in_specs=[
            pl.BlockSpec((rows, width), lambda i: (i, 0)),
            pl.BlockSpec((rows, width), lambda i: (i, 1)),
            pl.BlockSpec((None, ng, gc, gc), lambda i: (layer, 0, 0, 0)),
            pl.BlockSpec((1, width), lambda i: (0, 0)),
            pl.BlockSpec((1, width), lambda i: (0, 0)),
        ],
        out_specs=pl.BlockSpec((rows, width), lambda i: (i, 0)),
        out_shape=jax.ShapeDtypeStruct((s, width), BF16),
        scratch_shapes=[pltpu.VMEM((POOL_HALO, width), F32),
                        pltpu.VMEM((POOL_HALO + rows, width), F32),
                        pltpu.VMEM((ng, gc, gc), BF16),
                        pltpu.VMEM((POOL_HALO + rows, gc), F32),
                        pltpu.VMEM((POOL_HALO + rows, gc), F32)],
        compiler_params=pltpu.CompilerParams(
            dimension_semantics=("arbitrary",), vmem_limit_bytes=VMEM_LIMIT),
        name="pool_mixer",
    )(vg, vg, w_grp, b_grp.reshape(1, width), scale.reshape(1, width))


def _out_ln_kernel(ya_ref, yb_ref, w_ref, x_ref, g_ref, b_ref, xo_ref, xob_ref):
    half = ya_ref.shape[1]
    y = _dot(ya_ref[...], w_ref[:half, :]) + _dot(yb_ref[...], w_ref[half:, :])
    h = DEEPNORM_ALPHA * x_ref[...] + y
    mu = jnp.mean(h, axis=1, keepdims=True)
    d = h - mu
    var = jnp.mean(d * d, axis=1, keepdims=True)
    out = d * lax.rsqrt(var + LN_EPS) * g_ref[...] + b_ref[...]
    xo_ref[...] = out
    xob_ref[...] = out.astype(BF16)


def _out_ln(ya, yb, ya_blk, yb_blk, w_out, x, ln_g, ln_b):
    s, d = x.shape
    half = w_out.shape[0] // 2
    tm = OUT_TM
    return pl.pallas_call(
        _out_ln_kernel,
        grid=(s // tm,),
        in_specs=[
            pl.BlockSpec((tm, half), lambda i: (i, ya_blk)),
            pl.BlockSpec((tm, half), lambda i: (i, yb_blk)),
            pl.BlockSpec((2 * half, d), lambda i: (0, 0), pipeline_mode=pl.Buffered(1)),
            pl.BlockSpec((tm, d), lambda i: (i, 0)),
            pl.BlockSpec((1, d), lambda i: (0, 0)),
            pl.BlockSpec((1, d), lambda i: (0, 0)),
        ],
        out_specs=[pl.BlockSpec((tm, d), lambda i: (i, 0)),
                   pl.BlockSpec((tm, d), lambda i: (i, 0))],
        out_shape=[jax.ShapeDtypeStruct((s, d), F32), jax.ShapeDtypeStruct((s, d), BF16)],
        compiler_params=pltpu.CompilerParams(
            dimension_semantics=("parallel",), vmem_limit_bytes=VMEM_LIMIT),
        name="out_proj_layernorm",
    )(ya, yb, w_out.astype(BF16), x, ln_g.reshape(1, d), ln_b.reshape(1, d))


def _even_layer(x, xb, i, w_in, conv_w, conv_b, dt_bias, a_log, d_skip, ssm_norm_g,
                lq1, lk1, lq2, lk2, subln_g, w_out, ln_g, ln_b, lam_init):
    d = x.shape[1]
    att_w = d // 2
    ssm_w = d // 2
    heads = ssm_w // SSM_HEAD_DIM
    n_qkv = 3 * att_w
    n_gzx = att_w + ssm_w + ssm_w
    n_bc = 2 * SSM_GROUPS * SSM_STATE
    qkv = _matmul(xb, w_in, i, 0, n_qkv, MM_TN, BF16, "even_in_qkv", True)
    gzx = _matmul(xb, w_in, i, n_qkv, n_gzx, MM_TN, F32, "even_in_gate_ssm", True)
    bc = _matmul(xb, w_in, i, n_qkv + n_gzx, n_bc, n_bc, F32, "even_in_bc", True)
    w_dt = jnp.pad(w_in[i, n_qkv + n_gzx + n_bc:, :], ((0, LANES - heads), (0, 0)))
    dt_raw = _matmul(xb, w_dt[None], 0, 0, LANES, LANES, F32, "even_in_dt", True)
    lam_params = jnp.stack([lq1, lk1, lq2, lk2]).astype(F32)
    y_att = _attention(qkv, gzx, lam_params, subln_g.reshape(1, DIFF_V_DIM), lam_init)
    y_ssm = _ssd(gzx, bc, dt_raw, conv_w, conv_b, dt_bias, a_log, d_skip, ssm_norm_g, ssm_w)
    return _out_ln(y_att, y_ssm, 0, 0, w_out, x, ln_g, ln_b)


def _odd_layer(x, xb, i, w_in, w_grp, b_grp, scale, w_out, ln_g, ln_b):
    vg = _matmul(xb, w_in, i, 0, w_in.shape[2], MM_TN, F32, "odd_in")
    y = _pool(vg, w_grp, i, b_grp, scale)
    return _out_ln(y, y, 0, 1, w_out, x, ln_g, ln_b)


def kernel(x, ev_w_in, ev_conv_w, ev_conv_b, ev_dt_bias, ev_a_log, ev_d_skip, ev_ssm_norm_g, ev_lambda_q1, ev_lambda_k1, ev_lambda_q2, ev_lambda_k2, ev_subln_g, ev_w_out, od_w_in, od_w_grp, od_b_grp, od_scale, od_w_out, ln_g, ln_b):
    bsz, s, d = x.shape
    ev_w_in = jnp.swapaxes(ev_w_in, 1, 2)
    outs = []
    for b in range(bsz):
        xf = x[b]
        xb = xf.astype(BF16)
        for l in range(DEPTH):
            i = l // 2
            if l % 2 == 0:
                lam_init = 0.8 - 0.6 * math.exp(-0.3 * l)
                xf, xb = _even_layer(
                    xf, xb, i, ev_w_in, ev_conv_w[i], ev_conv_b[i], ev_dt_bias[i], ev_a_log[i],
                    ev_d_skip[i], ev_ssm_norm_g[i], ev_lambda_q1[i], ev_lambda_k1[i],
                    ev_lambda_q2[i], ev_lambda_k2[i], ev_subln_g[i], ev_w_out[i],
                    ln_g[l], ln_b[l], lam_init)
            else:
                xf, xb = _odd_layer(xf, xb, i, od_w_in, od_w_grp, od_b_grp[i], od_scale[i],
                                    od_w_out[i], ln_g[l], ln_b[l])
        outs.append(xf)
    return jnp.stack(outs)
```

```python
import functools
import math

import jax
import jax.numpy as jnp
from jax import lax
from jax.experimental import pallas as pl
from jax.experimental.pallas import tpu as pltpu

F32 = jnp.float32
BF16 = jnp.bfloat16

DEPTH = 4
CHUNK = 64
CHUNK_SHIFT = CHUNK.bit_length() - 1
DIFF_HEAD_DIM = 64
DIFF_V_DIM = 2 * DIFF_HEAD_DIM
SSM_HEAD_DIM = 64
HEAD_SHIFT = SSM_HEAD_DIM.bit_length() - 1
SSM_GROUPS = 2
SSM_STATE = 128
CONV_WIDTH = 4
POOL_WINDOWS = (2, 4, 8, 16)
DEEPNORM_ALPHA = (2.0 * DEPTH) ** 0.25
LN_EPS = 1e-5
RMS_EPS = 1e-5

LANES = 128
SUBLANES = 8
NEG_BIG = -1e30
FINITE_MAX = 3.0e38
VMEM_LIMIT = 56 * 1024 * 1024

ATTN_TQ = 256
ATTN_TK = 1024
ATTN_HP = 4
SSD_L = 256
MM_TM = 1024
MM_TN = 1024
OUT_TM = 512
POOL_TM = 512
POOL_HALO = 24


def _silu(x):
    h = 0.5 * x
    return h + h * jnp.tanh(h)


def _softplus(x):
    return jnp.maximum(x, 0.0) + jnp.log(1.0 + jnp.exp(-jnp.abs(x)))


def _split3(a):
    hi = a.astype(BF16)
    r1 = a - hi.astype(F32)
    mid = r1.astype(BF16)
    lo = (r1 - mid.astype(F32)).astype(BF16)
    return hi, mid, lo


def _dot(a, b):
    return jnp.dot(a, b, preferred_element_type=F32)


def _dot_nt(a, b):
    return lax.dot_general(a, b, (((1,), (1,)), ((), ())), preferred_element_type=F32)


def _matmul_kernel(x_ref, w_ref, o_ref, wb_ref, *, w_is_nk):
    @pl.when(pl.program_id(1) == 0)
    def _():
        wb_ref[...] = w_ref[...].astype(BF16)

    dot = _dot_nt if w_is_nk else _dot
    o_ref[...] = dot(x_ref[...], wb_ref[...]).astype(o_ref.dtype)


def _matmul(x, w, layer, col0, n, tn, out_dtype, name, w_is_nk=False):
    m, k = x.shape
    tm = min(MM_TM, m)
    assert n % tn == 0 and col0 % tn == 0 and m % tm == 0
    cb = col0 // tn
    if w_is_nk:
        w_spec = pl.BlockSpec((None, tn, k), lambda j, i: (layer, cb + j, 0))
    else:
        w_spec = pl.BlockSpec((None, k, tn), lambda j, i: (layer, 0, cb + j))
    return pl.pallas_call(
        functools.partial(_matmul_kernel, w_is_nk=w_is_nk),
        grid=(n // tn, m // tm),
        in_specs=[pl.BlockSpec((tm, k), lambda j, i: (i, 0)), w_spec],
        out_specs=pl.BlockSpec((tm, tn), lambda j, i: (i, j)),
        out_shape=jax.ShapeDtypeStruct((m, n), out_dtype),
        scratch_shapes=[pltpu.VMEM((tn, k) if w_is_nk else (k, tn), BF16)],
        compiler_params=pltpu.CompilerParams(
            dimension_semantics=("arbitrary", "arbitrary"), vmem_limit_bytes=VMEM_LIMIT),
        name=name,
    )(x, w)


def _attn_kernel(lam_ref, subg_ref, q_ref, k_ref, v_ref, g_ref, o_ref, *, tq, tk, hp, lam_init):
    qi = pl.program_id(1)
    lp = lam_ref[...]
    lam = (jnp.exp(jnp.sum(lp[0:1] * lp[1:2], axis=1, keepdims=True))
           - jnp.exp(jnp.sum(lp[2:3] * lp[3:4], axis=1, keepdims=True)) + lam_init)

    lane = lax.broadcasted_iota(jnp.int32, (tq, DIFF_V_DIM), 1)
    row = lax.broadcasted_iota(jnp.int32, (2 * tq, 1), 0)
    q_chunk = (qi * tq + jnp.where(row >= tq, row - tq, row)) >> CHUNK_SHIFT

    def stacked_q(j):
        q = q_ref[:, j * DIFF_V_DIM:(j + 1) * DIFF_V_DIM] * (DIFF_HEAD_DIM ** -0.5)
        zero = jnp.zeros_like(q)
        return jnp.concatenate([jnp.where(lane < DIFF_HEAD_DIM, q, zero),
                                jnp.where(lane >= DIFF_HEAD_DIM, q, zero)], axis=0)

    qq = [stacked_q(j) for j in range(hp)]

    def kv(kb, j, size):
        ks = pl.multiple_of(kb * size, size)
        cols = slice(j * DIFF_V_DIM, (j + 1) * DIFF_V_DIM)
        k = k_ref[pl.ds(ks, size), cols]
        v1 = jnp.concatenate([v_ref[pl.ds(ks, size), cols],
                              jnp.ones((size, DIFF_V_DIM), BF16)], axis=1)
        return ks, k, v1

    def exact_block(kb, j, m, acc, masked):
        ks, k, v1 = kv(kb, j, tq)
        s = _dot_nt(qq[j], k)
        if masked:
            k_chunk = (ks + lax.broadcasted_iota(jnp.int32, (1, tq), 1)) >> CHUNK_SHIFT
            s = jnp.where(k_chunk <= q_chunk, s, NEG_BIG)
        m_new = jnp.maximum(m, jnp.max(s, axis=1, keepdims=True))
        p = jnp.exp(s - m_new).astype(BF16)
        return m_new, jnp.exp(m - m_new) * acc + _dot(p, v1)

    def fast_block(kb, j, ref, acc):
        ks, k, v1 = kv(kb, j, tk)
        before = (ks + lax.broadcasted_iota(jnp.int32, (1, tk), 1)) < qi * tq
        p = jnp.exp(jnp.where(before, _dot_nt(qq[j], k) - ref, NEG_BIG)).astype(BF16)
        return acc + _dot(p, v1)

    n_fast = (qi * tq + tk - 1) // tk
    first = [exact_block(qi, j, jnp.full((2 * tq, 1), NEG_BIG, F32),
                         jnp.zeros((2 * tq, 2 * DIFF_V_DIM), F32), True) for j in range(hp)]

    def finish(j, acc):
        cols = slice(j * DIFF_V_DIM, (j + 1) * DIFF_V_DIM)
        inv = 1.0 / acc[:, DIFF_V_DIM:]
        o = acc[:, :DIFF_V_DIM] * inv
        o = o[:tq] - lam * o[tq:]
        bad = jnp.maximum(jnp.max(jnp.where(jnp.abs(o) < FINITE_MAX, 0.0, 1.0)),
                          jnp.max(jnp.where(inv > 0.0, 0.0, 1.0)))
        o = o * lax.rsqrt(jnp.mean(o * o, axis=1, keepdims=True) + RMS_EPS)
        o = o * subg_ref[...] * (1.0 - lam_init)
        o_ref[:, cols] = (o * _silu(g_ref[:, cols])).astype(o_ref.dtype)
        return bad

    def fast_step(kb, accs):
        return tuple(fast_block(kb, j, first[j][0], accs[j]) for j in range(hp))

    accs = lax.fori_loop(0, n_fast, fast_step, tuple(f[1] for f in first))
    overflow = jnp.float32(0.0)
    for j in range(hp):
        overflow = jnp.maximum(overflow, finish(j, accs[j]))

    @pl.when(overflow > 0.0)
    def _():
        def exact_step(kb, carry):
            return tuple(exact_block(kb, j, *carry[j], False) for j in range(hp))
        carry = lax.fori_loop(0, qi, exact_step, tuple(first))
        for j in range(hp):
            finish(j, carry[j][1])


def _attention(qkv, gzx, lam_params, subln_g, lam_init):
    s = qkv.shape[0]
    heads = qkv.shape[1] // (3 * DIFF_V_DIM)
    tq, tk, hp = ATTN_TQ, ATTN_TK, ATTN_HP
    groups = heads // hp
    bw = hp * DIFF_V_DIM
    kern = functools.partial(_attn_kernel, tq=tq, tk=tk, hp=hp, lam_init=lam_init)
    return pl.pallas_call(
        kern,
        grid=(groups, s // tq),
        in_specs=[
            pl.BlockSpec((4, DIFF_HEAD_DIM), lambda h, i: (0, 0)),
            pl.BlockSpec((1, DIFF_V_DIM), lambda h, i: (0, 0)),
            pl.BlockSpec((tq, bw), lambda h, i: (i, h)),
            pl.BlockSpec((s, bw), lambda h, i: (0, groups + h)),
            pl.BlockSpec((s, bw), lambda h, i: (0, 2 * groups + h)),
            pl.BlockSpec((tq, bw), lambda h, i: (i, h)),
        ],
        out_specs=pl.BlockSpec((tq, bw), lambda h, i: (i, h)),
        out_shape=jax.ShapeDtypeStruct((s, heads * DIFF_V_DIM), BF16),
        compiler_params=pltpu.CompilerParams(
            dimension_semantics=("parallel", "parallel"), vmem_limit_bytes=VMEM_LIMIT),
        name="diff_attention",
    )(lam_params, subln_g, qkv, qkv, qkv, gzx)


def _conv_silu(cur_ref, tail_ref, ext_ref, w_ref, b_ref, rows):
    ext_ref[0:SUBLANES, :] = tail_ref[...]
    ext_ref[SUBLANES:SUBLANES + rows, :] = cur_ref[...]
    tail_ref[...] = cur_ref[rows - SUBLANES:rows, :]
    acc = b_ref[...]
    for t in range(CONV_WIDTH):
        start = SUBLANES - (CONV_WIDTH - 1) + t
        acc = acc + w_ref[t:t + 1, :] * ext_ref[start:start + rows, :]
    return _silu(acc)


def _ssd_kernel(z_ref, xs_ref, bc_ref, dt_ref, cwx_ref, cbx_ref, cwbc_ref, cbbc_ref,
                dtb_ref, alog_ref, dskip_ref, ng_ref, y_ref,
                state_ref, tailx_ref, tailbc_ref, extx_ref, extbc_ref, *, rows):
    c = pl.program_id(0)
    width = xs_ref.shape[1]
    gw = width // SSM_GROUPS
    heads_per_pair = LANES // SSM_HEAD_DIM

    @pl.when(c == 0)
    def _():
        state_ref[...] = jnp.zeros_like(state_ref)
        tailx_ref[...] = jnp.zeros_like(tailx_ref)
        tailbc_ref[...] = jnp.zeros_like(tailbc_ref)

    xs = _conv_silu(xs_ref, tailx_ref, extx_ref, cwx_ref, cbx_ref, rows)
    bcv = _conv_silu(bc_ref, tailbc_ref, extbc_ref, cwbc_ref, cbbc_ref, rows)

    dtc = _softplus(dt_ref[...] + dtb_ref[...])
    adt = -jnp.exp(alog_ref[...]) * dtc
    ri = lax.broadcasted_iota(jnp.int32, (rows, rows), 0)
    ci = lax.broadcasted_iota(jnp.int32, (rows, rows), 1)
    causal = ci <= ri
    tri = jnp.where(causal, 1.0, 0.0).astype(BF16)
    csc = sum(_dot(tri, part) for part in _split3(adt))
    cs_t = csc.T

    er = lax.broadcasted_iota(jnp.int32, (LANES, width), 0)
    ec = lax.broadcasted_iota(jnp.int32, (LANES, width), 1)
    expand = jnp.where((ec >> HEAD_SHIFT) == er, 1.0, 0.0).astype(BF16)
    dt_e = sum(_dot(part, expand) for part in _split3(dtc))
    cs_e = sum(_dot(part, expand) for part in _split3(csc))
    cs_last = cs_e[rows - 1:rows, :]

    xdt = xs * dt_e
    xdt_b = xdt.astype(BF16)
    xd_b = (xdt * jnp.exp(cs_last - cs_e)).astype(BF16)
    ecs = jnp.exp(cs_e)
    chunk_decay = jnp.exp(cs_last)
    lane = lax.broadcasted_iota(jnp.int32, (rows, LANES), 1)

    for g in range(SSM_GROUPS):
        gsl = slice(g * gw, (g + 1) * gw)
        b_f = bcv[:, g * SSM_STATE:(g + 1) * SSM_STATE]
        c_b = bcv[:, (SSM_GROUPS + g) * SSM_STATE:(SSM_GROUPS + g + 1) * SSM_STATE].astype(BF16)
        scores = _dot_nt(c_b, b_f.astype(BF16))
        st = state_ref[g]
        y_off = _dot(c_b, st.astype(BF16)) * ecs[:, gsl]
        state_ref[g] = st * chunk_decay[:, gsl] + _dot(b_f.T.astype(BF16), xd_b[:, gsl])

        y_diag = []
        for pair in range(gw // LANES):
            col0 = g * gw + pair * LANES
            xpair = xdt_b[:, col0:col0 + LANES]
            parts = []
            for hh in range(heads_per_pair):
                h = col0 // SSM_HEAD_DIM + hh
                seg = cs_e[:, h * SSM_HEAD_DIM:h * SSM_HEAD_DIM + 1] - cs_t[h:h + 1, :]
                decay = jnp.exp(jnp.where(causal, seg, NEG_BIG))
                parts.append(_dot((scores * decay).astype(BF16), xpair))
            y_diag.append(jnp.where(lane < SSM_HEAD_DIM, parts[0], parts[1]))
        y = jnp.concatenate(y_diag, axis=1) + y_off + dskip_ref[:, gsl] * xs[:, gsl]
        y = y * _silu(z_ref[:, gsl])
        y = y * lax.rsqrt(jnp.mean(y * y, axis=1, keepdims=True) + RMS_EPS)
        y_ref[:, gsl] = (y * ng_ref[:, gsl]).astype(y_ref.dtype)


def _ssd(gzx, bc, dt_raw, conv_w, conv_b, dt_bias, a_log, d_skip, norm_g, width):
    s = gzx.shape[0]
    rows = SSD_L
    bcw = 2 * SSM_GROUPS * SSM_STATE
    heads = width // SSM_HEAD_DIM

    def pad_heads(p):
        return jnp.pad(p.astype(F32), (0, LANES - heads)).reshape(1, LANES)

    def per_channel(p):
        return jnp.repeat(p.astype(F32), SSM_HEAD_DIM).reshape(1, width)

    z_blk = 1
    xs_blk = 2
    const = lambda c: (0, 0)
    kern = functools.partial(_ssd_kernel, rows=rows)
    return pl.pallas_call(
        kern,
        grid=(s // rows,),
        in_specs=[
            pl.BlockSpec((rows, width), lambda c: (c, z_blk)),
            pl.BlockSpec((rows, width), lambda c: (c, xs_blk)),
            pl.BlockSpec((rows, bcw), lambda c: (c, 0)),
            pl.BlockSpec((rows, LANES), lambda c: (c, 0)),
            pl.BlockSpec((CONV_WIDTH, width), const),
            pl.BlockSpec((1, width), const),
            pl.BlockSpec((CONV_WIDTH, bcw), const),
            pl.BlockSpec((1, bcw), const),
            pl.BlockSpec((1, LANES), const),
            pl.BlockSpec((1, LANES), const),
            pl.BlockSpec((1, width), const),
            pl.BlockSpec((1, width), const),
        ],
        out_specs=pl.BlockSpec((rows, width), lambda c: (c, 0)),
        out_shape=jax.ShapeDtypeStruct((s, width), BF16),
        scratch_shapes=[
            pltpu.VMEM((SSM_GROUPS, SSM_STATE, width // SSM_GROUPS), F32),
            pltpu.VMEM((SUBLANES, width), F32),
            pltpu.VMEM((SUBLANES, bcw), F32),
            pltpu.VMEM((SUBLANES + rows, width), F32),
            pltpu.VMEM((SUBLANES + rows, bcw), F32),
        ],
        compiler_params=pltpu.CompilerParams(
            dimension_semantics=("arbitrary",), vmem_limit_bytes=VMEM_LIMIT),
        name="ssd_scan",
    )(gzx, gzx, bc, dt_raw,
      conv_w[:, :width], conv_b[:width].reshape(1, width),
      conv_w[:, width:], conv_b[width:].reshape(1, bcw),
      pad_heads(dt_bias), pad_heads(a_log), per_channel(d_skip), norm_g.reshape(1, width))


def _pool_kernel(v_ref, g_ref, wg_ref, bg_ref, sc_ref, y_ref, tail_ref, ext_ref, wb_ref,
                 pa_ref, pb_ref, *, rows):
    i = pl.program_id(0)
    gc = wg_ref.shape[1]
    top = POOL_HALO + rows

    @pl.when(i == 0)
    def _():
        tail_ref[...] = jnp.zeros_like(tail_ref)
        wb_ref[...] = wg_ref[...].astype(BF16)
        pa_ref[0:SUBLANES, :] = jnp.zeros((SUBLANES, gc), F32)
        pb_ref[0:SUBLANES, :] = jnp.zeros((SUBLANES, gc), F32)

    ext_ref[0:POOL_HALO, :] = tail_ref[...]
    ext_ref[POOL_HALO:top, :] = v_ref[...]
    tail_ref[...] = v_ref[rows - POOL_HALO:rows, :]
    pos = (i * rows + lax.broadcasted_iota(jnp.int32, (rows, 1), 0) + 1).astype(F32)

    for gi, w in enumerate(POOL_WINDOWS):
        cols = slice(gi * gc, (gi + 1) * gc)
        v = v_ref[:, cols]
        src, span = ext_ref.at[:, cols], 1
        for dst in (pa_ref, pb_ref, pa_ref, pb_ref):
            if span == w:
                break
            dst[SUBLANES:top, :] = (src[SUBLANES:top, :]
                                    + src[SUBLANES - span:top - span, :])
            src, span = dst, 2 * span
        acc = src[POOL_HALO:top, :]
        pooled = acc * (1.0 / jnp.minimum(pos, float(w))) - v
        m = _dot(pooled.astype(BF16), wb_ref[gi]) + bg_ref[:, cols]
        y_ref[:, cols] = (m * sc_ref[:, cols] * _silu(g_ref[:, cols])).astype(y_ref.dtype)


def _pool(vg, w_grp, layer, b_grp, scale):
    s = vg.shape[0]
    width = vg.shape[1] // 2
    rows = POOL_TM
    _, ng, gc, _ = w_grp.shape
    kern = functools.partial(_pool_kernel, rows=rows)
    return pl.pallas_call(
        kern,
        grid=(s // rows,),
        in_specs=[
            pl.BlockSpec((rows, width), lambda i: (i, 0)),
            pl.BlockSpec((rows, width), lambda i: (i, 1)),
            pl.BlockSpec((None, ng, gc, gc), lambda i: (layer, 0, 0, 0)),
            pl.BlockSpec((1, width), lambda i: (0, 0)),
            pl.BlockSpec((1, width), lambda i: (0, 0)),
        ],
        out_specs=pl.BlockSpec((rows, width), lambda i: (i, 0)),
        out_shape=jax.ShapeDtypeStruct((s, width), BF16),
        scratch_shapes=[pltpu.VMEM((POOL_HALO, width), F32),
                        pltpu.VMEM((POOL_HALO + rows, width), F32),
                        pltpu.VMEM((ng, gc, gc), BF16),
                        pltpu.VMEM((POOL_HALO + rows, gc), F32),
                        pltpu.VMEM((POOL_HALO + rows, gc), F32)],
        compiler_params=pltpu.CompilerParams(
            dimension_semantics=("arbitrary",), vmem_limit_bytes=VMEM_LIMIT),
        name="pool_mixer",
    )(vg, vg, w_grp, b_grp.reshape(1, width), scale.reshape(1, width))


def _out_ln_kernel(ya_ref, yb_ref, w_ref, x_ref, g_ref, b_ref, xo_ref, xob_ref):
    half = ya_ref.shape[1]
    y = _dot(ya_ref[...], w_ref[:half, :]) + _dot(yb_ref[...], w_ref[half:, :])
    h = DEEPNORM_ALPHA * x_ref[...] + y
    mu = jnp.mean(h, axis=1, keepdims=True)
    d = h - mu
    var = jnp.mean(d * d, axis=1, keepdims=True)
    out = d * lax.rsqrt(var + LN_EPS) * g_ref[...] + b_ref[...]
    xo_ref[...] = out
    xob_ref[...] = out.astype(BF16)


def _out_ln(ya, yb, ya_blk, yb_blk, w_out, x, ln_g, ln_b):
    s, d = x.shape
    half = w_out.shape[0] // 2
    tm = OUT_TM
    return pl.pallas_call(
        _out_ln_kernel,
        grid=(s // tm,),
        in_specs=[
            pl.BlockSpec((tm, half), lambda i: (i, ya_blk)),
            pl.BlockSpec((tm, half), lambda i: (i, yb_blk)),
            pl.BlockSpec((2 * half, d), lambda i: (0, 0), pipeline_mode=pl.Buffered(1)),
            pl.BlockSpec((tm, d), lambda i: (i, 0)),
            pl.BlockSpec((1, d), lambda i: (0, 0)),
            pl.BlockSpec((1, d), lambda i: (0, 0)),
        ],
        out_specs=[pl.BlockSpec((tm, d), lambda i: (i, 0)),
                   pl.BlockSpec((tm, d), lambda i: (i, 0))],
        out_shape=[jax.ShapeDtypeStruct((s, d), F32), jax.ShapeDtypeStruct((s, d), BF16)],
        compiler_params=pltpu.CompilerParams(
            dimension_semantics=("parallel",), vmem_limit_bytes=VMEM_LIMIT),
        name="out_proj_layernorm",
    )(ya, yb, w_out.astype(BF16), x, ln_g.reshape(1, d), ln_b.reshape(1, d))


def _even_layer(x, xb, i, w_in, conv_w, conv_b, dt_bias, a_log, d_skip, ssm_norm_g,
                lq1, lk1, lq2, lk2, subln_g, w_out, ln_g, ln_b, lam_init):
    d = x.shape[1]
    att_w = d // 2
    ssm_w = d // 2
    heads = ssm_w // SSM_HEAD_DIM
    n_qkv = 3 * att_w
    n_gzx = att_w + ssm_w + ssm_w
    n_bc = 2 * SSM_GROUPS * SSM_STATE
    qkv = _matmul(xb, w_in, i, 0, n_qkv, MM_TN, BF16, "even_in_qkv", True)
    gzx = _matmul(xb, w_in, i, n_qkv, n_gzx, MM_TN, F32, "even_in_gate_ssm", True)
    bc = _matmul(xb, w_in, i, n_qkv + n_gzx, n_bc, n_bc, F32, "even_in_bc", True)
    w_dt = jnp.pad(w_in[i, n_qkv + n_gzx + n_bc:, :], ((0, LANES - heads), (0, 0)))
    dt_raw = _matmul(xb, w_dt[None], 0, 0, LANES, LANES, F32, "even_in_dt", True)
    lam_params = jnp.stack([lq1, lk1, lq2, lk2]).astype(F32)
    y_att = _attention(qkv, gzx, lam_params, subln_g.reshape(1, DIFF_V_DIM), lam_init)
    y_ssm = _ssd(gzx, bc, dt_raw, conv_w, conv_b, dt_bias, a_log, d_skip, ssm_norm_g, ssm_w)
    return _out_ln(y_att, y_ssm, 0, 0, w_out, x, ln_g, ln_b)


def _odd_layer(x, xb, i, w_in, w_grp, b_grp, scale, w_out, ln_g, ln_b):
    vg = _matmul(xb, w_in, i, 0, w_in.shape[2], MM_TN, F32, "odd_in")
    y = _pool(vg, w_grp, i, b_grp, scale)
    return _out_ln(y, y, 0, 1, w_out, x, ln_g, ln_b)


def kernel(x, ev_w_in, ev_conv_w, ev_conv_b, ev_dt_bias, ev_a_log, ev_d_skip, ev_ssm_norm_g, ev_lambda_q1, ev_lambda_k1, ev_lambda_q2, ev_lambda_k2, ev_subln_g, ev_w_out, od_w_in, od_w_grp, od_b_grp, od_scale, od_w_out, ln_g, ln_b):
    bsz, s, d = x.shape
    ev_w_in = jnp.swapaxes(ev_w_in, 1, 2)
    outs = []
    for b in range(bsz):
        xf = x[b]
        xb = xf.astype(BF16)
        for l in range(DEPTH):
            i = l // 2
            if l % 2 == 0:
                lam_init = 0.8 - 0.6 * math.exp(-0.3 * l)
                xf, xb = _even_layer(
                    xf, xb, i, ev_w_in, ev_conv_w[i], ev_conv_b[i], ev_dt_bias[i], ev_a_log[i],
                    ev_d_skip[i], ev_ssm_norm_g[i], ev_lambda_q1[i], ev_lambda_k1[i],
                    ev_lambda_q2[i], ev_lambda_k2[i], ev_subln_g[i], ev_w_out[i],
                    ln_g[l], ln_b[l], lam_init)
            else:
                xf, xb = _odd_layer(xf, xb, i, od_w_in, od_w_grp, od_b_grp[i], od_scale[i],
                                    od_w_out[i], ln_g[l], ln_b[l])
        outs.append(xf)
    return jnp.stack(outs)
```

```python
import functools
import math

import jax
import jax.numpy as jnp
from jax import lax
from jax.experimental import pallas as pl
from jax.experimental.pallas import tpu as pltpu

F32 = jnp.float32
BF16 = jnp.bfloat16

DEPTH = 4
CHUNK = 64
CHUNK_SHIFT = CHUNK.bit_length() - 1
DIFF_HEAD_DIM = 64
DIFF_V_DIM = 2 * DIFF_HEAD_DIM
SSM_HEAD_DIM = 64
HEAD_SHIFT = SSM_HEAD_DIM.bit_length() - 1
SSM_GROUPS = 2
SSM_STATE = 128
CONV_WIDTH = 4
POOL_WINDOWS = (2, 4, 8, 16)
DEEPNORM_ALPHA = (2.0 * DEPTH) ** 0.25
LN_EPS = 1e-5
RMS_EPS = 1e-5

LANES = 128
SUBLANES = 8
NEG_BIG = -1e30
FINITE_MAX = 3.0e38
VMEM_LIMIT = 56 * 1024 * 1024

ATTN_TQ = 256
ATTN_TK = 1024
ATTN_HP = 4
SSD_L = 256
MM_TM = 1024
MM_TN = 1024
OUT_TM = 512
POOL_TM = 512
POOL_HALO = 24


def _silu(x):
    h = 0.5 * x
    return h + h * jnp.tanh(h)


def _softplus(x):
    return jnp.maximum(x, 0.0) + jnp.log(1.0 + jnp.exp(-jnp.abs(x)))


def _split3(a):
    hi = a.astype(BF16)
    r1 = a - hi.astype(F32)
    mid = r1.astype(BF16)
    lo = (r1 - mid.astype(F32)).astype(BF16)
    return hi, mid, lo


def _dot(a, b):
    return jnp.dot(a, b, preferred_element_type=F32)


def _dot_nt(a, b):
    return lax.dot_general(a, b, (((1,), (1,)), ((), ())), preferred_element_type=F32)


def _conv_silu(raw, tail_ref, ext_ref, w_ref, b_ref):
    rows = raw.shape[0]
    ext_ref[0:SUBLANES, :] = tail_ref[...]
    ext_ref[SUBLANES:SUBLANES + rows, :] = raw
    tail_ref[...] = raw[rows - SUBLANES:rows, :]
    acc = b_ref[...]
    for t in range(CONV_WIDTH):
        start = SUBLANES - (CONV_WIDTH - 1) + t
        acc = acc + w_ref[t:t + 1, :] * ext_ref[start:start + rows, :]
    return _silu(acc)


def _matmul_kernel(*refs, w_is_nk, epilogue):
    if epilogue == "conv_silu":
        x_ref, w_ref, cw_ref, cb_ref, o_ref, wb_ref, tail_ref, ext_ref = refs
    else:
        x_ref, w_ref, o_ref, wb_ref = refs

    @pl.when(pl.program_id(1) == 0)
    def _():
        wb_ref[...] = w_ref[...].astype(BF16)
        if epilogue == "conv_silu":
            tail_ref[...] = jnp.zeros_like(tail_ref)

    dot = _dot_nt if w_is_nk else _dot
    y = dot(x_ref[...], wb_ref[...])
    if epilogue == "silu":
        y = _silu(y)
    elif epilogue == "conv_silu":
        y = _conv_silu(y, tail_ref, ext_ref, cw_ref, cb_ref)
    o_ref[...] = y.astype(o_ref.dtype)


def _matmul(x, w, layer, col0, n, tn, out_dtype, name, w_is_nk=False, epilogue=None, conv=None):
    m, k = x.shape
    tm = min(MM_TM, m)
    assert n % tn == 0 and col0 % tn == 0 and m % tm == 0
    cb = col0 // tn
    if w_is_nk:
        w_spec = pl.BlockSpec((None, tn, k), lambda j, i: (layer, cb + j, 0))
    else:
        w_spec = pl.BlockSpec((None, k, tn), lambda j, i: (layer, 0, cb + j))
    in_specs = [pl.BlockSpec((tm, k), lambda j, i: (i, 0)), w_spec]
    scratch = [pltpu.VMEM((tn, k) if w_is_nk else (k, tn), BF16)]
    operands = [x, w]
    if epilogue == "conv_silu":
        in_specs += [pl.BlockSpec((CONV_WIDTH, tn), lambda j, i: (0, j)),
                     pl.BlockSpec((1, tn), lambda j, i: (0, j))]
        scratch += [pltpu.VMEM((SUBLANES, tn), F32), pltpu.VMEM((SUBLANES + tm, tn), F32)]
        operands += list(conv)
    return pl.pallas_call(
        functools.partial(_matmul_kernel, w_is_nk=w_is_nk, epilogue=epilogue),
        grid=(n // tn, m // tm),
        in_specs=in_specs,
        out_specs=pl.BlockSpec((tm, tn), lambda j, i: (i, j)),
        out_shape=jax.ShapeDtypeStruct((m, n), out_dtype),
        scratch_shapes=scratch,
        compiler_params=pltpu.CompilerParams(
            dimension_semantics=("arbitrary", "arbitrary"), vmem_limit_bytes=VMEM_LIMIT),
        name=name,
    )(*operands)


def _attn_kernel(lam_ref, subg_ref, q_ref, k_ref, v_ref, g_ref, o_ref, *, tq, tk, hp, lam_init):
    qi = pl.program_id(1)
    lp = lam_ref[...]
    lam = (jnp.exp(jnp.sum(lp[0:1] * lp[1:2], axis=1, keepdims=True))
           - jnp.exp(jnp.sum(lp[2:3] * lp[3:4], axis=1, keepdims=True)) + lam_init)

    lane = lax.broadcasted_iota(jnp.int32, (tq, DIFF_V_DIM), 1)
    row = lax.broadcasted_iota(jnp.int32, (2 * tq, 1), 0)
    q_chunk = (qi * tq + jnp.where(row >= tq, row - tq, row)) >> CHUNK_SHIFT

    def stacked_q(j):
        q = q_ref[:, j * DIFF_V_DIM:(j + 1) * DIFF_V_DIM] * (DIFF_HEAD_DIM ** -0.5)
        zero = jnp.zeros_like(q)
        return jnp.concatenate([jnp.where(lane < DIFF_HEAD_DIM, q, zero),
                                jnp.where(lane >= DIFF_HEAD_DIM, q, zero)], axis=0)

    qq = [stacked_q(j) for j in range(hp)]

    def kv(kb, j, size):
        ks = pl.multiple_of(kb * size, size)
        cols = slice(j * DIFF_V_DIM, (j + 1) * DIFF_V_DIM)
        k = k_ref[pl.ds(ks, size), cols]
        v1 = jnp.concatenate([v_ref[pl.ds(ks, size), cols],
                              jnp.ones((size, DIFF_V_DIM), BF16)], axis=1)
        return ks, k, v1

    def exact_block(kb, j, m, acc, masked):
        ks, k, v1 = kv(kb, j, tq)
        s = _dot_nt(qq[j], k)
        if masked:
            k_chunk = (ks + lax.broadcasted_iota(jnp.int32, (1, tq), 1)) >> CHUNK_SHIFT
            s = jnp.where(k_chunk <= q_chunk, s, NEG_BIG)
        m_new = jnp.maximum(m, jnp.max(s, axis=1, keepdims=True))
        p = jnp.exp(s - m_new).astype(BF16)
        return m_new, jnp.exp(m - m_new) * acc + _dot(p, v1)

    def fast_block(kb, j, ref, acc):
        ks, k, v1 = kv(kb, j, tk)
        before = (ks + lax.broadcasted_iota(jnp.int32, (1, tk), 1)) < qi * tq
        p = jnp.exp(jnp.where(before, _dot_nt(qq[j], k) - ref, NEG_BIG)).astype(BF16)
        return acc + _dot(p, v1)

    n_fast = (qi * tq + tk - 1) // tk
    first = [exact_block(qi, j, jnp.full((2 * tq, 1), NEG_BIG, F32),
                         jnp.zeros((2 * tq, 2 * DIFF_V_DIM), F32), True) for j in range(hp)]

    def finish(j, acc):
        cols = slice(j * DIFF_V_DIM, (j + 1) * DIFF_V_DIM)
        inv = 1.0 / acc[:, DIFF_V_DIM:]
        o = acc[:, :DIFF_V_DIM] * inv
        o = o[:tq] - lam * o[tq:]
        bad = jnp.maximum(jnp.max(jnp.where(jnp.abs(o) < FINITE_MAX, 0.0, 1.0)),
                          jnp.max(jnp.where(inv > 0.0, 0.0, 1.0)))
        o = o * lax.rsqrt(jnp.mean(o * o, axis=1, keepdims=True) + RMS_EPS)
        o = o * subg_ref[...] * (1.0 - lam_init)
        o_ref[:, cols] = (o * g_ref[:, cols]).astype(o_ref.dtype)
        return bad

    def fast_step(kb, accs):
        return tuple(fast_block(kb, j, first[j][0], accs[j]) for j in range(hp))

    def fast_pair(kp, accs):
        return fast_step(2 * kp + 1, fast_step(2 * kp, accs))

    accs = lax.fori_loop(0, n_fast // 2, fast_pair, tuple(f[1] for f in first))
    accs = lax.fori_loop(2 * (n_fast // 2), n_fast, fast_step, accs)
    overflow = jnp.float32(0.0)
    for j in range(hp):
        overflow = jnp.maximum(overflow, finish(j, accs[j]))

    @pl.when(overflow > 0.0)
    def _():
        def exact_step(kb, carry):
            return tuple(exact_block(kb, j, *carry[j], False) for j in range(hp))
        carry = lax.fori_loop(0, qi, exact_step, tuple(first))
        for j in range(hp):
            finish(j, carry[j][1])


def _attention(qkv, gates, lam_params, subln_g, lam_init):
    s = qkv.shape[0]
    heads = qkv.shape[1] // (3 * DIFF_V_DIM)
    tq, tk, hp = ATTN_TQ, ATTN_TK, ATTN_HP
    groups = heads // hp
    bw = hp * DIFF_V_DIM
    kern = functools.partial(_attn_kernel, tq=tq, tk=tk, hp=hp, lam_init=lam_init)
    return pl.pallas_call(
        kern,
        grid=(groups, s // tq),
        in_specs=[
            pl.BlockSpec((4, DIFF_HEAD_DIM), lambda h, i: (0, 0)),
            pl.BlockSpec((1, DIFF_V_DIM), lambda h, i: (0, 0)),
            pl.BlockSpec((tq, bw), lambda h, i: (i, h)),
            pl.BlockSpec((s, bw), lambda h, i: (0, groups + h)),
            pl.BlockSpec((s, bw), lambda h, i: (0, 2 * groups + h)),
            pl.BlockSpec((tq, bw), lambda h, i: (i, h)),
        ],
        out_specs=pl.BlockSpec((tq, bw), lambda h, i: (i, h)),
        out_shape=jax.ShapeDtypeStruct((s, heads * DIFF_V_DIM), BF16),
        compiler_params=pltpu.CompilerParams(
            dimension_semantics=("parallel", "parallel"), vmem_limit_bytes=VMEM_LIMIT),
        name="diff_attention",
    )(lam_params, subln_g, qkv, qkv, qkv, gates)


def _ssd_kernel(zs_ref, xs_ref, bc_ref, dt_ref, dtb_ref, alog_ref, dskip_ref, ng_ref, y_ref,
                state_ref, *, rows):
    c = pl.program_id(0)
    width = xs_ref.shape[1]
    gw = width // SSM_GROUPS
    heads_per_pair = LANES // SSM_HEAD_DIM

    @pl.when(c == 0)
    def _():
        state_ref[...] = jnp.zeros_like(state_ref)

    xs = xs_ref[...]
    bcv = bc_ref[...]

    dtc = _softplus(dt_ref[...] + dtb_ref[...])
    adt = -jnp.exp(alog_ref[...]) * dtc
    ri = lax.broadcasted_iota(jnp.int32, (rows, rows), 0)
    ci = lax.broadcasted_iota(jnp.int32, (rows, rows), 1)
    causal = ci <= ri
    tri = jnp.where(causal, 1.0, 0.0).astype(BF16)
    csc = sum(_dot(tri, part) for part in _split3(adt))
    cs_t = csc.T

    er = lax.broadcasted_iota(jnp.int32, (LANES, width), 0)
    ec = lax.broadcasted_iota(jnp.int32, (LANES, width), 1)
    expand = jnp.where((ec >> HEAD_SHIFT) == er, 1.0, 0.0).astype(BF16)
    dt_e = sum(_dot(part, expand) for part in _split3(dtc))
    cs_e = sum(_dot(part, expand) for part in _split3(csc))
    cs_last = cs_e[rows - 1:rows, :]

    xdt = xs * dt_e
    xdt_b = xdt.astype(BF16)
    xd_b = (xdt * jnp.exp(cs_last - cs_e)).astype(BF16)
    ecs = jnp.exp(cs_e)
    chunk_decay = jnp.exp(cs_last)
    lane = lax.broadcasted_iota(jnp.int32, (rows, LANES), 1)

    for g in range(SSM_GROUPS):
        gsl = slice(g * gw, (g + 1) * gw)
        b_f = bcv[:, g * SSM_STATE:(g + 1) * SSM_STATE]
        c_b = bcv[:, (SSM_GROUPS + g) * SSM_STATE:(SSM_GROUPS + g + 1) * SSM_STATE].astype(BF16)
        scores = _dot_nt(c_b, b_f.astype(BF16))
        st = state_ref[g]
        y_off = _dot(c_b, st.astype(BF16)) * ecs[:, gsl]
        state_ref[g] = st * chunk_decay[:, gsl] + _dot(b_f.T.astype(BF16), xd_b[:, gsl])

        y_diag = []
        for pair in range(gw // LANES):
            col0 = g * gw + pair * LANES
            xpair = xdt_b[:, col0:col0 + LANES]
            parts = []
            for hh in range(heads_per_pair):
                h = col0 // SSM_HEAD_DIM + hh
                seg = cs_e[:, h * SSM_HEAD_DIM:h * SSM_HEAD_DIM + 1] - cs_t[h:h + 1, :]
                decay = jnp.exp(jnp.where(causal, seg, NEG_BIG))
                parts.append(_dot((scores * decay).astype(BF16), xpair))
            y_diag.append(jnp.where(lane < SSM_HEAD_DIM, parts[0], parts[1]))
        y = jnp.concatenate(y_diag, axis=1) + y_off + dskip_ref[:, gsl] * xs[:, gsl]
        y = y * zs_ref[:, gsl]
        y = y * lax.rsqrt(jnp.mean(y * y, axis=1, keepdims=True) + RMS_EPS)
        y_ref[:, gsl] = (y * ng_ref[:, gsl]).astype(y_ref.dtype)


def _ssd(gates, xs, bc, dt_raw, dt_bias, a_log, d_skip, norm_g, width):
    s = xs.shape[0]
    rows = SSD_L
    bcw = 2 * SSM_GROUPS * SSM_STATE
    heads = width // SSM_HEAD_DIM

    def pad_heads(p):
        return jnp.pad(p.astype(F32), (0, LANES - heads)).reshape(1, LANES)

    def per_channel(p):
        return jnp.repeat(p.astype(F32), SSM_HEAD_DIM).reshape(1, width)

    z_blk = 1
    const = lambda c: (0, 0)
    kern = functools.partial(_ssd_kernel, rows=rows)
    return pl.pallas_call(
        kern,
        grid=(s // rows,),
        in_specs=[
            pl.BlockSpec((rows, width), lambda c: (c, z_blk)),
            pl.BlockSpec((rows, width), lambda c: (c, 0)),
            pl.BlockSpec((rows, bcw), lambda c: (c, 0)),
            pl.BlockSpec((rows, LANES), lambda c: (c, 0)),
            pl.BlockSpec((1, LANES), const),
            pl.BlockSpec((1, LANES), const),
            pl.BlockSpec((1, width), const),
            pl.BlockSpec((1, width), const),
        ],
        out_specs=pl.BlockSpec((rows, width), lambda c: (c, 0)),
        out_shape=jax.ShapeDtypeStruct((s, width), BF16),
        scratch_shapes=[pltpu.VMEM((SSM_GROUPS, SSM_STATE, width // SSM_GROUPS), F32)],
        compiler_params=pltpu.CompilerParams(
            dimension_semantics=("arbitrary",), vmem_limit_bytes=VMEM_LIMIT),
        name="ssd_scan",
    )(gates, xs, bc, dt_raw,
      pad_heads(dt_bias), pad_heads(a_log), per_channel(d_skip), norm_g.reshape(1, width))


def _pool_kernel(v_ref, g_ref, wg_ref, bg_ref, sc_ref, y_ref, tail_ref, ext_ref, wb_ref,
                 pa_ref, pb_ref, *, rows):
    i = pl.program_id(0)
    gc = wg_ref.shape[1]
    top = POOL_HALO + rows

    @pl.when(i == 0)
    def _():
        tail_ref[...] = jnp.zeros_like(tail_ref)
        wb_ref[...] = wg_ref[...].astype(BF16)
        pa_ref[0:SUBLANES, :] = jnp.zeros((SUBLANES, gc), F32)
        pb_ref[0:SUBLANES, :] = jnp.zeros((SUBLANES, gc), F32)

    ext_ref[0:POOL_HALO, :] = tail_ref[...]
    ext_ref[POOL_HALO:top, :] = v_ref[...]
    tail_ref[...] = v_ref[rows - POOL_HALO:rows, :]
    pos = (i * rows + lax.broadcasted_iota(jnp.int32, (rows, 1), 0) + 1).astype(F32)

    for gi, w in enumerate(POOL_WINDOWS):
        cols = slice(gi * gc, (gi + 1) * gc)
        v = v_ref[:, cols]
        src, span = ext_ref.at[:, cols], 1
        for dst in (pa_ref, pb_ref, pa_ref, pb_ref):
            if span == w:
                break
            dst[SUBLANES:top, :] = (src[SUBLANES:top, :]
                                    + src[SUBLANES - span:top - span, :])
            src, span = dst, 2 * span
        acc = src[POOL_HALO:top, :]
        pooled = acc * (1.0 / jnp.minimum(pos, float(w))) - v
        m = _dot(pooled.astype(BF16), wb_ref[gi]) + bg_ref[:, cols]
        y_ref[:, cols] = (m * sc_ref[:, cols] * g_ref[:, cols]).astype(y_ref.dtype)


def _pool(v, gs, w_grp, layer, b_grp, scale):
    s, width = v.shape
    rows = POOL_TM
    _, ng, gc, _ = w_grp.shape
    kern = functools.partial(_pool_kernel, rows=rows)
    return pl.pallas_call(
        kern,
        grid=(s // rows,),
        in_specs=[
            pl.BlockSpec((rows, width), lambda i: (i, 0)),
            pl.BlockSpec((rows, width), lambda i: (i, 0)),
            pl.BlockSpec((None, ng, gc, gc), lambda i: (layer, 0, 0, 0)),
            pl.BlockSpec((1, width), lambda i: (0, 0)),
            pl.BlockSpec((1, width), lambda i: (0, 0)),
        ],
        out_specs=pl.BlockSpec((rows, width), lambda i: (i, 0)),
        out_shape=jax.ShapeDtypeStruct((s, width), BF16),
        scratch_shapes=[pltpu.VMEM((POOL_HALO, width), F32),
                        pltpu.VMEM((POOL_HALO + rows, width), F32),
                        pltpu.VMEM((ng, gc, gc), BF16),
                        pltpu.VMEM((POOL_HALO + rows, gc), F32),
                        pltpu.VMEM((POOL_HALO + rows, gc), F32)],
        compiler_params=pltpu.CompilerParams(
            dimension_semantics=("arbitrary",), vmem_limit_bytes=VMEM_LIMIT),
        name="pool_mixer",
    )(v, gs, w_grp, b_grp.reshape(1, width), scale.reshape(1, width))


def _out_ln_kernel(ya_ref, yb_ref, w_ref, x_ref, g_ref, b_ref, xo_ref, xob_ref):
    half = ya_ref.shape[1]
    y = _dot(ya_ref[...], w_ref[:half, :]) + _dot(yb_ref[...], w_ref[half:, :])
    h = DEEPNORM_ALPHA * x_ref[...] + y
    mu = jnp.mean(h, axis=1, keepdims=True)
    d = h - mu
    var = jnp.mean(d * d, axis=1, keepdims=True)
    out = d * lax.rsqrt(var + LN_EPS) * g_ref[...] + b_ref[...]
    xo_ref[...] = out
    xob_ref[...] = out.astype(BF16)


def _out_ln(ya, yb, ya_blk, yb_blk, w_out, x, ln_g, ln_b):
    s, d = x.shape
    half = w_out.shape[0] // 2
    tm = OUT_TM
    return pl.pallas_call(
        _out_ln_kernel,
        grid=(s // tm,),
        in_specs=[
            pl.BlockSpec((tm, half), lambda i: (i, ya_blk)),
            pl.BlockSpec((tm, half), lambda i: (i, yb_blk)),
            pl.BlockSpec((2 * half, d), lambda i: (0, 0), pipeline_mode=pl.Buffered(1)),
            pl.BlockSpec((tm, d), lambda i: (i, 0)),
            pl.BlockSpec((1, d), lambda i: (0, 0)),
            pl.BlockSpec((1, d), lambda i: (0, 0)),
        ],
        out_specs=[pl.BlockSpec((tm, d), lambda i: (i, 0)),
                   pl.BlockSpec((tm, d), lambda i: (i, 0))],
        out_shape=[jax.ShapeDtypeStruct((s, d), F32), jax.ShapeDtypeStruct((s, d), BF16)],
        compiler_params=pltpu.CompilerParams(
            dimension_semantics=("parallel",), vmem_limit_bytes=VMEM_LIMIT),
        name="out_proj_layernorm",
    )(ya, yb, w_out.astype(BF16), x, ln_g.reshape(1, d), ln_b.reshape(1, d))


def _even_layer(x, xb, i, w_in, conv_w, conv_b, dt_bias, a_log, d_skip, ssm_norm_g,
                lq1, lk1, lq2, lk2, subln_g, w_out, ln_g, ln_b, lam_init):
    d = x.shape[1]
    att_w = d // 2
    ssm_w = d // 2
    heads = ssm_w // SSM_HEAD_DIM
    n_qkv = 3 * att_w
    n_gates = att_w + ssm_w
    n_bc = 2 * SSM_GROUPS * SSM_STATE
    c_x = n_qkv + n_gates
    c_bc = c_x + ssm_w
    conv_x = (conv_w[:, :ssm_w], conv_b[:ssm_w].reshape(1, ssm_w))
    conv_bc = (conv_w[:, ssm_w:], conv_b[ssm_w:].reshape(1, n_bc))
    qkv = _matmul(xb, w_in, i, 0, n_qkv, MM_TN, BF16, "even_in_qkv", True)
    gates = _matmul(xb, w_in, i, n_qkv, n_gates, MM_TN, F32, "even_in_gates", True, "silu")
    xs = _matmul(xb, w_in, i, c_x, ssm_w, MM_TN, F32, "even_in_x", True, "conv_silu", conv_x)
    bc = _matmul(xb, w_in, i, c_bc, n_bc, n_bc, F32, "even_in_bc", True, "conv_silu", conv_bc)
    w_dt = jnp.pad(w_in[i, c_bc + n_bc:, :], ((0, LANES - heads), (0, 0)))
    dt_raw = _matmul(xb, w_dt[None], 0, 0, LANES, LANES, F32, "even_in_dt", True)
    lam_params = jnp.stack([lq1, lk1, lq2, lk2]).astype(F32)
    y_att = _attention(qkv, gates, lam_params, subln_g.reshape(1, DIFF_V_DIM), lam_init)
    y_ssm = _ssd(gates, xs, bc, dt_raw, dt_bias, a_log, d_skip, ssm_norm_g, ssm_w)
    return _out_ln(y_att, y_ssm, 0, 0, w_out, x, ln_g, ln_b)


def _odd_layer(x, xb, i, w_in, w_grp, b_grp, scale, w_out, ln_g, ln_b):
    width = w_in.shape[2] // 2
    v = _matmul(xb, w_in, i, 0, width, MM_TN, F32, "odd_in_v")
    gs = _matmul(xb, w_in, i, width, width, MM_TN, F32, "odd_in_gate", epilogue="silu")
    y = _pool(v, gs, w_grp, i, b_grp, scale)
    return _out_ln(y, y, 0, 1, w_out, x, ln_g, ln_b)


def kernel(x, ev_w_in, ev_conv_w, ev_conv_b, ev_dt_bias, ev_a_log, ev_d_skip, ev_ssm_norm_g, ev_lambda_q1, ev_lambda_k1, ev_lambda_q2, ev_lambda_k2, ev_subln_g, ev_w_out, od_w_in, od_w_grp, od_b_grp, od_scale, od_w_out, ln_g, ln_b):
    bsz, s, d = x.shape
    ev_w_in = jnp.swapaxes(ev_w_in, 1, 2)
    outs = []
    for b in range(bsz):
        xf = x[b]
        xb = xf.astype(BF16)
        for l in range(DEPTH):
            i = l // 2
            if l % 2 == 0:
                lam_init = 0.8 - 0.6 * math.exp(-0.3 * l)
                xf, xb = _even_layer(
                    xf, xb, i, ev_w_in, ev_conv_w[i], ev_conv_b[i], ev_dt_bias[i], ev_a_log[i],
                    ev_d_skip[i], ev_ssm_norm_g[i], ev_lambda_q1[i], ev_lambda_k1[i],
                    ev_lambda_q2[i], ev_lambda_k2[i], ev_subln_g[i], ev_w_out[i],
                    ln_g[l], ln_b[l], lam_init)
            else:
                xf, xb = _odd_layer(xf, xb, i, od_w_in, od_w_grp, od_b_grp[i], od_scale[i],
                                    od_w_out[i], ln_g[l], ln_b[l])
        outs.append(xf)
    return jnp.stack(outs)
```

```python
import functools
import math

import jax
import jax.numpy as jnp
from jax import lax
from jax.experimental import pallas as pl
from jax.experimental.pallas import tpu as pltpu

F32 = jnp.float32
BF16 = jnp.bfloat16

DEPTH = 4
CHUNK = 64
CHUNK_SHIFT = CHUNK.bit_length() - 1
DIFF_HEAD_DIM = 64
DIFF_V_DIM = 2 * DIFF_HEAD_DIM
SSM_HEAD_DIM = 64
HEAD_SHIFT = SSM_HEAD_DIM.bit_length() - 1
SSM_GROUPS = 2
SSM_STATE = 128
CONV_WIDTH = 4
POOL_WINDOWS = (2, 4, 8, 16)
DEEPNORM_ALPHA = (2.0 * DEPTH) ** 0.25
LN_EPS = 1e-5
RMS_EPS = 1e-5

LANES = 128
SUBLANES = 8
NEG_BIG = -1e30
FINITE_MAX = 3.0e38
VMEM_LIMIT = 56 * 1024 * 1024

ATTN_TQ = 256
ATTN_TK = 1024
ATTN_HP = 4
SSD_L = 256
MM_TM = 1024
MM_TN = 1024
OUT_TM = 512
POOL_TM = 512
POOL_HALO = 24


def _silu(x):
    h = 0.5 * x
    return h + h * jnp.tanh(h)


def _softplus(x):
    return jnp.maximum(x, 0.0) + jnp.log(1.0 + jnp.exp(-jnp.abs(x)))


def _split3(a):
    hi = a.astype(BF16)
    r1 = a - hi.astype(F32)
    mid = r1.astype(BF16)
    lo = (r1 - mid.astype(F32)).astype(BF16)
    return hi, mid, lo


def _dot(a, b):
    return jnp.dot(a, b, preferred_element_type=F32)


def _dot_nt(a, b):
    return lax.dot_general(a, b, (((1,), (1,)), ((), ())), preferred_element_type=F32)


def _conv_silu(raw, tail_ref, ext_ref, w_ref, b_ref):
    rows = raw.shape[0]
    ext_ref[0:SUBLANES, :] = tail_ref[...]
    ext_ref[SUBLANES:SUBLANES + rows, :] = raw
    tail_ref[...] = raw[rows - SUBLANES:rows, :]
    acc = b_ref[...]
    for t in range(CONV_WIDTH):
        start = SUBLANES - (CONV_WIDTH - 1) + t
        acc = acc + w_ref[t:t + 1, :] * ext_ref[start:start + rows, :]
    return _silu(acc)


def _matmul_kernel(*refs, w_is_nk, epilogue):
    if epilogue == "conv_silu":
        x_ref, w_ref, cw_ref, cb_ref, o_ref, wb_ref, tail_ref, ext_ref = refs
    else:
        x_ref, w_ref, o_ref, wb_ref = refs

    @pl.when(pl.program_id(1) == 0)
    def _():
        wb_ref[...] = w_ref[...].astype(BF16)
        if epilogue == "conv_silu":
            tail_ref[...] = jnp.zeros_like(tail_ref)

    dot = _dot_nt if w_is_nk else _dot
    y = dot(x_ref[...], wb_ref[...])
    if epilogue == "silu":
        y = _silu(y)
    elif epilogue == "conv_silu":
        y = _conv_silu(y, tail_ref, ext_ref, cw_ref, cb_ref)
    o_ref[...] = y.astype(o_ref.dtype)


def _matmul(x, w, layer, col0, n, tn, out_dtype, name, w_is_nk=False, epilogue=None, conv=None):
    m, k = x.shape
    tm = min(MM_TM, m)
    assert n % tn == 0 and col0 % tn == 0 and m % tm == 0
    cb = col0 // tn
    if w_is_nk:
        w_spec = pl.BlockSpec((None, tn, k), lambda j, i: (layer, cb + j, 0))
    else:
        w_spec = pl.BlockSpec((None, k, tn), lambda j, i: (layer, 0, cb + j))
    in_specs = [pl.BlockSpec((tm, k), lambda j, i: (i, 0)), w_spec]
    scratch = [pltpu.VMEM((tn, k) if w_is_nk else (k, tn), BF16)]
    operands = [x, w]
    if epilogue == "conv_silu":
        in_specs += [pl.BlockSpec((CONV_WIDTH, tn), lambda j, i: (0, j)),
                     pl.BlockSpec((1, tn), lambda j, i: (0, j))]
        scratch += [pltpu.VMEM((SUBLANES, tn), F32), pltpu.VMEM((SUBLANES + tm, tn), F32)]
        operands += list(conv)
    return pl.pallas_call(
        functools.partial(_matmul_kernel, w_is_nk=w_is_nk, epilogue=epilogue),
        grid=(n // tn, m // tm),
        in_specs=in_specs,
        out_specs=pl.BlockSpec((tm, tn), lambda j, i: (i, j)),
        out_shape=jax.ShapeDtypeStruct((m, n), out_dtype),
        scratch_shapes=scratch,
        compiler_params=pltpu.CompilerParams(
            dimension_semantics=("arbitrary", "arbitrary"), vmem_limit_bytes=VMEM_LIMIT),
        name=name,
    )(*operands)


def _attn_kernel(lam_ref, subg_ref, q_ref, k_ref, v_ref, g_ref, o_ref, acc_ref,
                 *, tq, tk, hp, lam_init):
    qi = pl.program_id(1)
    lp = lam_ref[...]
    lam = (jnp.exp(jnp.sum(lp[0:1] * lp[1:2], axis=1, keepdims=True))
           - jnp.exp(jnp.sum(lp[2:3] * lp[3:4], axis=1, keepdims=True)) + lam_init)

    lane = lax.broadcasted_iota(jnp.int32, (tq, DIFF_V_DIM), 1)
    row = lax.broadcasted_iota(jnp.int32, (2 * tq, 1), 0)
    q_chunk = (qi * tq + jnp.where(row >= tq, row - tq, row)) >> CHUNK_SHIFT

    def stacked_q(j):
        q = q_ref[:, j * DIFF_V_DIM:(j + 1) * DIFF_V_DIM] * (DIFF_HEAD_DIM ** -0.5)
        zero = jnp.zeros_like(q)
        return jnp.concatenate([jnp.where(lane < DIFF_HEAD_DIM, q, zero),
                                jnp.where(lane >= DIFF_HEAD_DIM, q, zero)], axis=0)

    qq = [stacked_q(j) for j in range(hp)]

    def kv(kb, j, size):
        ks = pl.multiple_of(kb * size, size)
        cols = slice(j * DIFF_V_DIM, (j + 1) * DIFF_V_DIM)
        k = k_ref[pl.ds(ks, size), cols]
        v1 = jnp.concatenate([v_ref[pl.ds(ks, size), cols],
                              jnp.ones((size, DIFF_V_DIM), BF16)], axis=1)
        return ks, k, v1

    def exact_block(kb, j, m, acc, masked):
        ks, k, v1 = kv(kb, j, tq)
        s = _dot_nt(qq[j], k)
        if masked:
            k_chunk = (ks + lax.broadcasted_iota(jnp.int32, (1, tq), 1)) >> CHUNK_SHIFT
            s = jnp.where(k_chunk <= q_chunk, s, NEG_BIG)
        m_new = jnp.maximum(m, jnp.max(s, axis=1, keepdims=True))
        p = jnp.exp(s - m_new).astype(BF16)
        return m_new, jnp.exp(m - m_new) * acc + _dot(p, v1)

    def own_block(j):
        return exact_block(qi, j, jnp.full((2 * tq, 1), NEG_BIG, F32),
                           jnp.zeros((2 * tq, 2 * DIFF_V_DIM), F32), True)

    def fast_block(kb, j, ref):
        ks, k, v1 = kv(kb, j, tk)
        before = (ks + lax.broadcasted_iota(jnp.int32, (1, tk), 1)) < qi * tq
        p = jnp.exp(jnp.where(before, _dot_nt(qq[j], k) - ref, NEG_BIG)).astype(BF16)
        return _dot(p, v1)

    def finish(j, acc):
        cols = slice(j * DIFF_V_DIM, (j + 1) * DIFF_V_DIM)
        inv = 1.0 / acc[:, DIFF_V_DIM:]
        o = acc[:, :DIFF_V_DIM] * inv
        o = o[:tq] - lam * o[tq:]
        bad = jnp.maximum(jnp.max(jnp.where(jnp.abs(o) < FINITE_MAX, 0.0, 1.0)),
                          jnp.max(jnp.where(inv > 0.0, 0.0, 1.0)))
        o = o * lax.rsqrt(jnp.mean(o * o, axis=1, keepdims=True) + RMS_EPS)
        o = o * subg_ref[...] * (1.0 - lam_init)
        o_ref[:, cols] = (o * g_ref[:, cols]).astype(o_ref.dtype)
        return bad

    refs = []
    for j in range(hp):
        m, acc = own_block(j)
        refs.append(m)
        acc_ref[j] = acc

    n_fast = (qi * tq + tk - 1) // tk

    def fast_pair(kp, carry):
        for j in range(hp):
            acc_ref[j] = (acc_ref[j] + fast_block(2 * kp, j, refs[j])
                          + fast_block(2 * kp + 1, j, refs[j]))
        return carry

    lax.fori_loop(0, n_fast // 2, fast_pair, 0)

    @pl.when(n_fast % 2 == 1)
    def _():
        for j in range(hp):
            acc_ref[j] = acc_ref[j] + fast_block(n_fast - 1, j, refs[j])

    overflow = jnp.float32(0.0)
    for j in range(hp):
        overflow = jnp.maximum(overflow, finish(j, acc_ref[j]))

    @pl.when(overflow > 0.0)
    def _():
        def exact_step(kb, carry):
            return tuple(exact_block(kb, j, *carry[j], False) for j in range(hp))
        carry = lax.fori_loop(0, qi, exact_step, tuple(own_block(j) for j in range(hp)))
        for j in range(hp):
            finish(j, carry[j][1])


def _attention(qkv, gates, lam_params, subln_g, lam_init):
    s = qkv.shape[0]
    heads = qkv.shape[1] // (3 * DIFF_V_DIM)
    tq, tk, hp = ATTN_TQ, ATTN_TK, ATTN_HP
    groups = heads // hp
    bw = hp * DIFF_V_DIM
    kern = functools.partial(_attn_kernel, tq=tq, tk=tk, hp=hp, lam_init=lam_init)
    return pl.pallas_call(
        kern,
        grid=(groups, s // tq),
        in_specs=[
            pl.BlockSpec((4, DIFF_HEAD_DIM), lambda h, i: (0, 0)),
            pl.BlockSpec((1, DIFF_V_DIM), lambda h, i: (0, 0)),
            pl.BlockSpec((tq, bw), lambda h, i: (i, h)),
            pl.BlockSpec((s, bw), lambda h, i: (0, groups + h)),
            pl.BlockSpec((s, bw), lambda h, i: (0, 2 * groups + h)),
            pl.BlockSpec((tq, bw), lambda h, i: (i, h)),
        ],
        out_specs=pl.BlockSpec((tq, bw), lambda h, i: (i, h)),
        out_shape=jax.ShapeDtypeStruct((s, heads * DIFF_V_DIM), BF16),
        scratch_shapes=[pltpu.VMEM((hp, 2 * tq, 2 * DIFF_V_DIM), F32)],
        compiler_params=pltpu.CompilerParams(
            dimension_semantics=("parallel", "parallel"), vmem_limit_bytes=VMEM_LIMIT),
        name="diff_attention",
    )(lam_params, subln_g, qkv, qkv, qkv, gates)


def _ssd_kernel(zs_ref, xs_ref, bc_ref, dt_ref, dtb_ref, alog_ref, dskip_ref, ng_ref, y_ref,
                state_ref, *, rows):
    c = pl.program_id(0)
    width = xs_ref.shape[1]
    gw = width // SSM_GROUPS
    heads_per_pair = LANES // SSM_HEAD_DIM

    @pl.when(c == 0)
    def _():
        state_ref[...] = jnp.zeros_like(state_ref)

    xs = xs_ref[...]
    bcv = bc_ref[...]

    dtc = _softplus(dt_ref[...] + dtb_ref[...])
    adt = -jnp.exp(alog_ref[...]) * dtc
    ri = lax.broadcasted_iota(jnp.int32, (rows, rows), 0)
    ci = lax.broadcasted_iota(jnp.int32, (rows, rows), 1)
    causal = ci <= ri
    tri = jnp.where(causal, 1.0, 0.0).astype(BF16)
    csc = sum(_dot(tri, part) for part in _split3(adt))
    cs_t = csc.T

    er = lax.broadcasted_iota(jnp.int32, (LANES, width), 0)
    ec = lax.broadcasted_iota(jnp.int32, (LANES, width), 1)
    expand = jnp.where((ec >> HEAD_SHIFT) == er, 1.0, 0.0).astype(BF16)
    dt_e = sum(_dot(part, expand) for part in _split3(dtc))
    cs_e = sum(_dot(part, expand) for part in _split3(csc))
    cs_last = cs_e[rows - 1:rows, :]

    xdt = xs * dt_e
    xdt_b = xdt.astype(BF16)
    xd_b = (xdt * jnp.exp(cs_last - cs_e)).astype(BF16)
    ecs = jnp.exp(cs_e)
    chunk_decay = jnp.exp(cs_last)
    lane = lax.broadcasted_iota(jnp.int32, (rows, LANES), 1)

    for g in range(SSM_GROUPS):
        gsl = slice(g * gw, (g + 1) * gw)
        b_f = bcv[:, g * SSM_STATE:(g + 1) * SSM_STATE]
        c_b = bcv[:, (SSM_GROUPS + g) * SSM_STATE:(SSM_GROUPS + g + 1) * SSM_STATE].astype(BF16)
        scores = _dot_nt(c_b, b_f.astype(BF16))
        st = state_ref[g]
        y_off = _dot(c_b, st.astype(BF16)) * ecs[:, gsl]
        state_ref[g] = st * chunk_decay[:, gsl] + _dot(b_f.T.astype(BF16), xd_b[:, gsl])

        y_diag = []
        for pair in range(gw // LANES):
            col0 = g * gw + pair * LANES
            xpair = xdt_b[:, col0:col0 + LANES]
            parts = []
            for hh in range(heads_per_pair):
                h = col0 // SSM_HEAD_DIM + hh
                seg = cs_e[:, h * SSM_HEAD_DIM:h * SSM_HEAD_DIM + 1] - cs_t[h:h + 1, :]
                decay = jnp.exp(jnp.where(causal, seg, NEG_BIG))
                parts.append(_dot((scores * decay).astype(BF16), xpair))
            y_diag.append(jnp.where(lane < SSM_HEAD_DIM, parts[0], parts[1]))
        y = jnp.concatenate(y_diag, axis=1) + y_off + dskip_ref[:, gsl] * xs[:, gsl]
        y = y * zs_ref[:, gsl]
        y = y * lax.rsqrt(jnp.mean(y * y, axis=1, keepdims=True) + RMS_EPS)
        y_ref[:, gsl] = (y * ng_ref[:, gsl]).astype(y_ref.dtype)


def _ssd(gates, xs, bc, dt_raw, dt_bias, a_log, d_skip, norm_g, width):
    s = xs.shape[0]
    rows = SSD_L
    bcw = 2 * SSM_GROUPS * SSM_STATE
    heads = width // SSM_HEAD_DIM

    def pad_heads(p):
        return jnp.pad(p.astype(F32), (0, LANES - heads)).reshape(1, LANES)

    def per_channel(p):
        return jnp.repeat(p.astype(F32), SSM_HEAD_DIM).reshape(1, width)

    z_blk = 1
    const = lambda c: (0, 0)
    kern = functools.partial(_ssd_kernel, rows=rows)
    return pl.pallas_call(
        kern,
        grid=(s // rows,),
        in_specs=[
            pl.BlockSpec((rows, width), lambda c: (c, z_blk)),
            pl.BlockSpec((rows, width), lambda c: (c, 0)),
            pl.BlockSpec((rows, bcw), lambda c: (c, 0)),
            pl.BlockSpec((rows, LANES), lambda c: (c, 0)),
            pl.BlockSpec((1, LANES), const),
            pl.BlockSpec((1, LANES), const),
            pl.BlockSpec((1, width), const),
            pl.BlockSpec((1, width), const),
        ],
        out_specs=pl.BlockSpec((rows, width), lambda c: (c, 0)),
        out_shape=jax.ShapeDtypeStruct((s, width), BF16),
        scratch_shapes=[pltpu.VMEM((SSM_GROUPS, SSM_STATE, width // SSM_GROUPS), F32)],
        compiler_params=pltpu.CompilerParams(
            dimension_semantics=("arbitrary",), vmem_limit_bytes=VMEM_LIMIT),
        name="ssd_scan",
    )(gates, xs, bc, dt_raw,
      pad_heads(dt_bias), pad_heads(a_log), per_channel(d_skip), norm_g.reshape(1, width))


def _pool_kernel(v_ref, g_ref, wg_ref, bg_ref, sc_ref, y_ref, tail_ref, ext_ref, wb_ref,
                 pa_ref, pb_ref, *, rows):
    i = pl.program_id(0)
    gc = wg_ref.shape[1]
    top = POOL_HALO + rows

    @pl.when(i == 0)
    def _():
        tail_ref[...] = jnp.zeros_like(tail_ref)
        wb_ref[...] = wg_ref[...].astype(BF16)
        pa_ref[0:SUBLANES, :] = jnp.zeros((SUBLANES, gc), F32)
        pb_ref[0:SUBLANES, :] = jnp.zeros((SUBLANES, gc), F32)

    ext_ref[0:POOL_HALO, :] = tail_ref[...]
    ext_ref[POOL_HALO:top, :] = v_ref[...]
    tail_ref[...] = v_ref[rows - POOL_HALO:rows, :]
    pos = (i * rows + lax.broadcasted_iota(jnp.int32, (rows, 1), 0) + 1).astype(F32)

    for gi, w in enumerate(POOL_WINDOWS):
        cols = slice(gi * gc, (gi + 1) * gc)
        v = v_ref[:, cols]
        src, span = ext_ref.at[:, cols], 1
        for dst in (pa_ref, pb_ref, pa_ref, pb_ref):
            if span == w:
                break
            dst[SUBLANES:top, :] = (src[SUBLANES:top, :]
                                    + src[SUBLANES - span:top - span, :])
            src, span = dst, 2 * span
        acc = src[POOL_HALO:top, :]
        pooled = acc * (1.0 / jnp.minimum(pos, float(w))) - v
        m = _dot(pooled.astype(BF16), wb_ref[gi]) + bg_ref[:, cols]
        y_ref[:, cols] = (m * sc_ref[:, cols] * g_ref[:, cols]).astype(y_ref.dtype)


def _pool(v, gs, w_grp, layer, b_grp, scale):
    s, width = v.shape
    rows = POOL_TM
    _, ng, gc, _ = w_grp.shape
    kern = functools.partial(_pool_kernel, rows=rows)
    return pl.pallas_call(
        kern,
        grid=(s // rows,),
        in_specs=[
            pl.BlockSpec((rows, width), lambda i: (i, 0)),
            pl.BlockSpec((rows, width), lambda i: (i, 0)),
            pl.BlockSpec((None, ng, gc, gc), lambda i: (layer, 0, 0, 0)),
            pl.BlockSpec((1, width), lambda i: (0, 0)),
            pl.BlockSpec((1, width), lambda i: (0, 0)),
        ],
        out_specs=pl.BlockSpec((rows, width), lambda i: (i, 0)),
        out_shape=jax.ShapeDtypeStruct((s, width), BF16),
        scratch_shapes=[pltpu.VMEM((POOL_HALO, width), F32),
                        pltpu.VMEM((POOL_HALO + rows, width), F32),
                        pltpu.VMEM((ng, gc, gc), BF16),
                        pltpu.VMEM((POOL_HALO + rows, gc), F32),
                        pltpu.VMEM((POOL_HALO + rows, gc), F32)],
        compiler_params=pltpu.CompilerParams(
            dimension_semantics=("arbitrary",), vmem_limit_bytes=VMEM_LIMIT),
        name="pool_mixer",
    )(v, gs, w_grp, b_grp.reshape(1, width), scale.reshape(1, width))


def _out_ln_kernel(ya_ref, yb_ref, w_ref, x_ref, g_ref, b_ref, xo_ref, xob_ref):
    half = ya_ref.shape[1]
    y = _dot(ya_ref[...], w_ref[:half, :]) + _dot(yb_ref[...], w_ref[half:, :])
    h = DEEPNORM_ALPHA * x_ref[...] + y
    mu = jnp.mean(h, axis=1, keepdims=True)
    d = h - mu
    var = jnp.mean(d * d, axis=1, keepdims=True)
    out = d * lax.rsqrt(var + LN_EPS) * g_ref[...] + b_ref[...]
    xo_ref[...] = out
    xob_ref[...] = out.astype(BF16)


def _out_ln(ya, yb, ya_blk, yb_blk, w_out, x, ln_g, ln_b):
    s, d = x.shape
    half = w_out.shape[0] // 2
    tm = OUT_TM
    return pl.pallas_call(
        _out_ln_kernel,
        grid=(s // tm,),
        in_specs=[
            pl.BlockSpec((tm, half), lambda i: (i, ya_blk)),
            pl.BlockSpec((tm, half), lambda i: (i, yb_blk)),
            pl.BlockSpec((2 * half, d), lambda i: (0, 0), pipeline_mode=pl.Buffered(1)),
            pl.BlockSpec((tm, d), lambda i: (i, 0)),
            pl.BlockSpec((1, d), lambda i: (0, 0)),
            pl.BlockSpec((1, d), lambda i: (0, 0)),
        ],
        out_specs=[pl.BlockSpec((tm, d), lambda i: (i, 0)),
                   pl.BlockSpec((tm, d), lambda i: (i, 0))],
        out_shape=[jax.ShapeDtypeStruct((s, d), F32), jax.ShapeDtypeStruct((s, d), BF16)],
        compiler_params=pltpu.CompilerParams(
            dimension_semantics=("parallel",), vmem_limit_bytes=VMEM_LIMIT),
        name="out_proj_layernorm",
    )(ya, yb, w_out.astype(BF16), x, ln_g.reshape(1, d), ln_b.reshape(1, d))


def _even_layer(x, xb, i, w_in, conv_w, conv_b, dt_bias, a_log, d_skip, ssm_norm_g,
                lq1, lk1, lq2, lk2, subln_g, w_out, ln_g, ln_b, lam_init):
    d = x.shape[1]
    att_w = d // 2
    ssm_w = d // 2
    heads = ssm_w // SSM_HEAD_DIM
    n_qkv = 3 * att_w
    n_gates = att_w + ssm_w
    n_bc = 2 * SSM_GROUPS * SSM_STATE
    c_x = n_qkv + n_gates
    c_bc = c_x + ssm_w
    conv_x = (conv_w[:, :ssm_w], conv_b[:ssm_w].reshape(1, ssm_w))
    conv_bc = (conv_w[:, ssm_w:], conv_b[ssm_w:].reshape(1, n_bc))
    qkv = _matmul(xb, w_in, i, 0, n_qkv, MM_TN, BF16, "even_in_qkv", True)
    gates = _matmul(xb, w_in, i, n_qkv, n_gates, MM_TN, F32, "even_in_gates", True, "silu")
    xs = _matmul(xb, w_in, i, c_x, ssm_w, MM_TN, F32, "even_in_x", True, "conv_silu", conv_x)
    bc = _matmul(xb, w_in, i, c_bc, n_bc, n_bc, F32, "even_in_bc", True, "conv_silu", conv_bc)
    w_dt = jnp.pad(w_in[i, c_bc + n_bc:, :], ((0, LANES - heads), (0, 0)))
    dt_raw = _matmul(xb, w_dt[None], 0, 0, LANES, LANES, F32, "even_in_dt", True)
    lam_params = jnp.stack([lq1, lk1, lq2, lk2]).astype(F32)
    y_att = _attention(qkv, gates, lam_params, subln_g.reshape(1, DIFF_V_DIM), lam_init)
    y_ssm = _ssd(gates, xs, bc, dt_raw, dt_bias, a_log, d_skip, ssm_norm_g, ssm_w)
    return _out_ln(y_att, y_ssm, 0, 0, w_out, x, ln_g, ln_b)


def _odd_layer(x, xb, i, w_in, w_grp, b_grp, scale, w_out, ln_g, ln_b):
    width = w_in.shape[2] // 2
    v = _matmul(xb, w_in, i, 0, width, MM_TN, F32, "odd_in_v")
    gs = _matmul(xb, w_in, i, width, width, MM_TN, F32, "odd_in_gate", epilogue="silu")
    y = _pool(v, gs, w_grp, i, b_grp, scale)
    return _out_ln(y, y, 0, 1, w_out, x, ln_g, ln_b)


def kernel(x, ev_w_in, ev_conv_w, ev_conv_b, ev_dt_bias, ev_a_log, ev_d_skip, ev_ssm_norm_g, ev_lambda_q1, ev_lambda_k1, ev_lambda_q2, ev_lambda_k2, ev_subln_g, ev_w_out, od_w_in, od_w_grp, od_b_grp, od_scale, od_w_out, ln_g, ln_b):
    bsz, s, d = x.shape
    ev_w_in = jnp.swapaxes(ev_w_in, 1, 2)
    outs = []
    for b in range(bsz):
        xf = x[b]
        xb = xf.astype(BF16)
        for l in range(DEPTH):
            i = l // 2
            if l % 2 == 0:
                lam_init = 0.8 - 0.6 * math.exp(-0.3 * l)
                xf, xb = _even_layer(
                    xf, xb, i, ev_w_in, ev_conv_w[i], ev_conv_b[i], ev_dt_bias[i], ev_a_log[i],
                    ev_d_skip[i], ev_ssm_norm_g[i], ev_lambda_q1[i], ev_lambda_k1[i],
                    ev_lambda_q2[i], ev_lambda_k2[i], ev_subln_g[i], ev_w_out[i],
                    ln_g[l], ln_b[l], lam_init)
            else:
                xf, xb = _odd_layer(xf, xb, i, od_w_in, od_w_grp, od_b_grp[i], od_scale[i],
                                    od_w_out[i], ln_g[l], ln_b[l])
        outs.append(xf)
    return jnp.stack(outs)
```

```python
import functools
import math

import jax
import jax.numpy as jnp
from jax import lax
from jax.experimental import pallas as pl
from jax.experimental.pallas import tpu as pltpu

F32 = jnp.float32
BF16 = jnp.bfloat16

DEPTH = 4
CHUNK = 64
CHUNK_SHIFT = CHUNK.bit_length() - 1
DIFF_HEAD_DIM = 64
DIFF_V_DIM = 2 * DIFF_HEAD_DIM
SSM_HEAD_DIM = 64
HEAD_SHIFT = SSM_HEAD_DIM.bit_length() - 1
SSM_GROUPS = 2
SSM_STATE = 128
CONV_WIDTH = 4
POOL_WINDOWS = (2, 4, 8, 16)
DEEPNORM_ALPHA = (2.0 * DEPTH) ** 0.25
LN_EPS = 1e-5
RMS_EPS = 1e-5

LANES = 128
SUBLANES = 8
NEG_BIG = -1e30
FINITE_MAX = 3.0e38
VMEM_LIMIT = 56 * 1024 * 1024

ATTN_TQ = 256
ATTN_TK = 1024
ATTN_HP = 4
SSD_L = 256
MM_TM = 1024
MM_TN = 1024
OUT_TM = 512
POOL_TM = 512
POOL_HALO = 24


def _silu(x):
    h = 0.5 * x
    return h + h * jnp.tanh(h)


def _softplus(x):
    return jnp.maximum(x, 0.0) + jnp.log(1.0 + jnp.exp(-jnp.abs(x)))


def _split3(a):
    hi = a.astype(BF16)
    r1 = a - hi.astype(F32)
    mid = r1.astype(BF16)
    lo = (r1 - mid.astype(F32)).astype(BF16)
    return hi, mid, lo


def _dot(a, b):
    return jnp.dot(a, b, preferred_element_type=F32)


def _dot_nt(a, b):
    return lax.dot_general(a, b, (((1,), (1,)), ((), ())), preferred_element_type=F32)


def _conv_silu(raw, tail_ref, ext_ref, w_ref, b_ref):
    rows = raw.shape[0]
    ext_ref[0:SUBLANES, :] = tail_ref[...]
    ext_ref[SUBLANES:SUBLANES + rows, :] = raw
    tail_ref[...] = raw[rows - SUBLANES:rows, :]
    acc = b_ref[...]
    for t in range(CONV_WIDTH):
        start = SUBLANES - (CONV_WIDTH - 1) + t
        acc = acc + w_ref[t:t + 1, :] * ext_ref[start:start + rows, :]
    return _silu(acc)


def _matmul_kernel(*refs, w_is_nk, epilogue):
    if epilogue == "conv_silu":
        x_ref, w_ref, cw_ref, cb_ref, o_ref, wb_ref, tail_ref, ext_ref = refs
    else:
        x_ref, w_ref, o_ref, wb_ref = refs

    @pl.when(pl.program_id(1) == 0)
    def _():
        wb_ref[...] = w_ref[...].astype(BF16)
        if epilogue == "conv_silu":
            tail_ref[...] = jnp.zeros_like(tail_ref)

    dot = _dot_nt if w_is_nk else _dot
    y = dot(x_ref[...], wb_ref[...])
    if epilogue == "silu":
        y = _silu(y)
    elif epilogue == "conv_silu":
        y = _conv_silu(y, tail_ref, ext_ref, cw_ref, cb_ref)
    o_ref[...] = y.astype(o_ref.dtype)


def _matmul(x, w, layer, col0, n, tn, out_dtype, name, w_is_nk=False, epilogue=None, conv=None):
    m, k = x.shape
    tm = min(MM_TM, m)
    assert n % tn == 0 and col0 % tn == 0 and m % tm == 0
    cb = col0 // tn
    if w_is_nk:
        w_spec = pl.BlockSpec((None, tn, k), lambda j, i: (layer, cb + j, 0))
    else:
        w_spec = pl.BlockSpec((None, k, tn), lambda j, i: (layer, 0, cb + j))
    in_specs = [pl.BlockSpec((tm, k), lambda j, i: (i, 0)), w_spec]
    scratch = [pltpu.VMEM((tn, k) if w_is_nk else (k, tn), BF16)]
    operands = [x, w]
    if epilogue == "conv_silu":
        in_specs += [pl.BlockSpec((CONV_WIDTH, tn), lambda j, i: (0, j)),
                     pl.BlockSpec((1, tn), lambda j, i: (0, j))]
        scratch += [pltpu.VMEM((SUBLANES, tn), F32), pltpu.VMEM((SUBLANES + tm, tn), F32)]
        operands += list(conv)
    return pl.pallas_call(
        functools.partial(_matmul_kernel, w_is_nk=w_is_nk, epilogue=epilogue),
        grid=(n // tn, m // tm),
        in_specs=in_specs,
        out_specs=pl.BlockSpec((tm, tn), lambda j, i: (i, j)),
        out_shape=jax.ShapeDtypeStruct((m, n), out_dtype),
        scratch_shapes=scratch,
        compiler_params=pltpu.CompilerParams(
            dimension_semantics=("arbitrary", "arbitrary"), vmem_limit_bytes=VMEM_LIMIT),
        name=name,
    )(*operands)


def _xconv_dt_kernel(x_ref, w_ref, wdt_ref, cw_ref, cb_ref, o_ref, dt_ref,
                     wb_ref, wdtb_ref, tail_ref, ext_ref):
    @pl.when(pl.program_id(0) == 0)
    def _():
        wb_ref[...] = w_ref[...].astype(BF16)
        wdtb_ref[...] = wdt_ref[...].astype(BF16)
        tail_ref[...] = jnp.zeros_like(tail_ref)

    x = x_ref[...]
    o_ref[...] = _conv_silu(_dot_nt(x, wb_ref[...]), tail_ref, ext_ref, cw_ref, cb_ref)
    dt_ref[...] = _dot_nt(x, wdtb_ref[...])


def _xconv_dt(x, w, layer, row0, n, w_dt, conv):
    m, k = x.shape
    tm = min(MM_TM, m)
    assert row0 % n == 0 and m % tm == 0
    const = lambda i: (0, 0)
    return pl.pallas_call(
        _xconv_dt_kernel,
        grid=(m // tm,),
        in_specs=[pl.BlockSpec((tm, k), lambda i: (i, 0)),
                  pl.BlockSpec((None, n, k), lambda i: (layer, row0 // n, 0)),
                  pl.BlockSpec((LANES, k), const),
                  pl.BlockSpec((CONV_WIDTH, n), const),
                  pl.BlockSpec((1, n), const)],
        out_specs=[pl.BlockSpec((tm, n), lambda i: (i, 0)),
                   pl.BlockSpec((tm, LANES), lambda i: (i, 0))],
        out_shape=[jax.ShapeDtypeStruct((m, n), F32), jax.ShapeDtypeStruct((m, LANES), F32)],
        scratch_shapes=[pltpu.VMEM((n, k), BF16), pltpu.VMEM((LANES, k), BF16),
                        pltpu.VMEM((SUBLANES, n), F32), pltpu.VMEM((SUBLANES + tm, n), F32)],
        compiler_params=pltpu.CompilerParams(
            dimension_semantics=("arbitrary",), vmem_limit_bytes=VMEM_LIMIT),
        name="even_in_x_dt",
    )(x, w, w_dt, *conv)


def _attn_kernel(lam_ref, subg_ref, q_ref, k_ref, v_ref, g_ref, o_ref, acc_ref,
                 *, tq, tk, hp, lam_init):
    qi = pl.program_id(1)
    lp = lam_ref[...]
    lam = (jnp.exp(jnp.sum(lp[0:1] * lp[1:2], axis=1, keepdims=True))
           - jnp.exp(jnp.sum(lp[2:3] * lp[3:4], axis=1, keepdims=True)) + lam_init)

    lane = lax.broadcasted_iota(jnp.int32, (tq, DIFF_V_DIM), 1)
    row = lax.broadcasted_iota(jnp.int32, (2 * tq, 1), 0)
    q_chunk = (qi * tq + jnp.where(row >= tq, row - tq, row)) >> CHUNK_SHIFT

    def stacked_q(j):
        q = q_ref[:, j * DIFF_V_DIM:(j + 1) * DIFF_V_DIM] * (DIFF_HEAD_DIM ** -0.5)
        zero = jnp.zeros_like(q)
        return jnp.concatenate([jnp.where(lane < DIFF_HEAD_DIM, q, zero),
                                jnp.where(lane >= DIFF_HEAD_DIM, q, zero)], axis=0)

    qq = [stacked_q(j) for j in range(hp)]

    def kv(kb, j, size):
        ks = pl.multiple_of(kb * size, size)
        cols = slice(j * DIFF_V_DIM, (j + 1) * DIFF_V_DIM)
        k = k_ref[pl.ds(ks, size), cols]
        v1 = jnp.concatenate([v_ref[pl.ds(ks, size), cols],
                              jnp.ones((size, DIFF_V_DIM), BF16)], axis=1)
        return ks, k, v1

    def exact_block(kb, j, m, acc):
        _, k, v1 = kv(kb, j, tq)
        s = _dot_nt(qq[j], k)
        m_new = jnp.maximum(m, jnp.max(s, axis=1, keepdims=True))
        p = jnp.exp(s - m_new).astype(BF16)
        return m_new, jnp.exp(m - m_new) * acc + _dot(p, v1)

    def own_block(j):
        ks, k, v1 = kv(qi, j, tq)
        k_chunk = (ks + lax.broadcasted_iota(jnp.int32, (1, tq), 1)) >> CHUNK_SHIFT
        s = jnp.where(k_chunk <= q_chunk, _dot_nt(qq[j], k), NEG_BIG)
        m = jnp.max(s, axis=1, keepdims=True)
        return m, _dot(jnp.exp(s - m).astype(BF16), v1)

    def fast_block(kb, j, size, straddles):
        ks, k, v1 = kv(kb, j, size)
        x = _dot_nt(qq[j], k) - refs[j]
        if straddles:
            before = (ks + lax.broadcasted_iota(jnp.int32, (1, size), 1)) < qi * tq
            x = jnp.where(before, x, NEG_BIG)
        return _dot(jnp.exp(x).astype(BF16), v1)

    def accumulate(blocks):
        for j in range(hp):
            total = acc_ref[j]
            for kb, size, straddles in blocks:
                total = total + fast_block(kb, j, size, straddles)
            acc_ref[j] = total

    def finish(j, acc):
        cols = slice(j * DIFF_V_DIM, (j + 1) * DIFF_V_DIM)
        inv = 1.0 / acc[:, DIFF_V_DIM:]
        o = acc[:, :DIFF_V_DIM] * inv
        o = o[:tq] - lam * o[tq:]
        bad = jnp.maximum(jnp.max(jnp.where(jnp.abs(o) < FINITE_MAX, 0.0, 1.0)),
                          jnp.max(jnp.where(inv > 0.0, 0.0, 1.0)))
        o = o * lax.rsqrt(jnp.mean(o * o, axis=1, keepdims=True) + RMS_EPS)
        o = o * subg_ref[...] * (1.0 - lam_init)
        o_ref[:, cols] = (o * g_ref[:, cols]).astype(o_ref.dtype)
        return bad

    refs = []
    for j in range(hp):
        m, acc = own_block(j)
        refs.append(m)
        acc_ref[j] = acc

    n_full = (qi * tq) // tk
    rest = qi * tq - n_full * tk

    def fast_pair(kp, carry):
        accumulate([(2 * kp, tk, False), (2 * kp + 1, tk, False)])
        return carry

    lax.fori_loop(0, n_full // 2, fast_pair, 0)

    @pl.when(n_full % 2 == 1)
    def _():
        accumulate([(n_full - 1, tk, False)])

    @pl.when(jnp.logical_and(rest > 0, rest <= tk // 2))
    def _():
        accumulate([(2 * n_full, tk // 2, True)])

    @pl.when(rest > tk // 2)
    def _():
        accumulate([(n_full, tk, True)])

    overflow = jnp.float32(0.0)
    for j in range(hp):
        overflow = jnp.maximum(overflow, finish(j, acc_ref[j]))

    @pl.when(overflow > 0.0)
    def _():
        def exact_step(kb, carry):
            return tuple(exact_block(kb, j, *carry[j]) for j in range(hp))
        carry = lax.fori_loop(0, qi, exact_step, tuple(own_block(j) for j in range(hp)))
        for j in range(hp):
            finish(j, carry[j][1])


def _attention(qkv, gates, lam_params, subln_g, lam_init):
    s = qkv.shape[0]
    heads = qkv.shape[1] // (3 * DIFF_V_DIM)
    tq, tk, hp = ATTN_TQ, ATTN_TK, ATTN_HP
    groups = heads // hp
    bw = hp * DIFF_V_DIM
    kern = functools.partial(_attn_kernel, tq=tq, tk=tk, hp=hp, lam_init=lam_init)
    return pl.pallas_call(
        kern,
        grid=(groups, s // tq),
        in_specs=[
            pl.BlockSpec((4, DIFF_HEAD_DIM), lambda h, i: (0, 0)),
            pl.BlockSpec((1, DIFF_V_DIM), lambda h, i: (0, 0)),
            pl.BlockSpec((tq, bw), lambda h, i: (i, h)),
            pl.BlockSpec((s, bw), lambda h, i: (0, groups + h)),
            pl.BlockSpec((s, bw), lambda h, i: (0, 2 * groups + h)),
            pl.BlockSpec((tq, bw), lambda h, i: (i, h)),
        ],
        out_specs=pl.BlockSpec((tq, bw), lambda h, i: (i, h)),
        out_shape=jax.ShapeDtypeStruct((s, heads * DIFF_V_DIM), BF16),
        scratch_shapes=[pltpu.VMEM((hp, 2 * tq, 2 * DIFF_V_DIM), F32)],
        compiler_params=pltpu.CompilerParams(
            dimension_semantics=("parallel", "parallel"), vmem_limit_bytes=VMEM_LIMIT),
        name="diff_attention",
    )(lam_params, subln_g, qkv, qkv, qkv, gates)


def _ssd_kernel(zs_ref, xs_ref, bc_ref, dt_ref, dtb_ref, alog_ref, dskip_ref, ng_ref, y_ref,
                state_ref, *, rows):
    c = pl.program_id(0)
    width = xs_ref.shape[1]
    gw = width // SSM_GROUPS
    heads_per_pair = LANES // SSM_HEAD_DIM

    @pl.when(c == 0)
    def _():
        state_ref[...] = jnp.zeros_like(state_ref)

    xs = xs_ref[...]
    bcv = bc_ref[...]

    dtc = _softplus(dt_ref[...] + dtb_ref[...])
    adt = -jnp.exp(alog_ref[...]) * dtc
    ri = lax.broadcasted_iota(jnp.int32, (rows, rows), 0)
    ci = lax.broadcasted_iota(jnp.int32, (rows, rows), 1)
    causal = ci <= ri
    tri = jnp.where(causal, 1.0, 0.0).astype(BF16)
    csc = sum(_dot(tri, part) for part in _split3(adt))
    cs_t = csc.T

    er = lax.broadcasted_iota(jnp.int32, (LANES, width), 0)
    ec = lax.broadcasted_iota(jnp.int32, (LANES, width), 1)
    expand = jnp.where((ec >> HEAD_SHIFT) == er, 1.0, 0.0).astype(BF16)
    dt_e = sum(_dot(part, expand) for part in _split3(dtc))
    cs_e = sum(_dot(part, expand) for part in _split3(csc))
    cs_last = cs_e[rows - 1:rows, :]

    xdt = xs * dt_e
    xdt_b = xdt.astype(BF16)
    xd_b = (xdt * jnp.exp(cs_last - cs_e)).astype(BF16)
    ecs = jnp.exp(cs_e)
    chunk_decay = jnp.exp(cs_last)
    lane = lax.broadcasted_iota(jnp.int32, (rows, LANES), 1)

    for g in range(SSM_GROUPS):
        gsl = slice(g * gw, (g + 1) * gw)
        b_f = bcv[:, g * SSM_STATE:(g + 1) * SSM_STATE]
        c_b = bcv[:, (SSM_GROUPS + g) * SSM_STATE:(SSM_GROUPS + g + 1) * SSM_STATE].astype(BF16)
        scores = _dot_nt(c_b, b_f.astype(BF16))
        st = state_ref[g]
        y_off = _dot(c_b, st.astype(BF16)) * ecs[:, gsl]
        state_ref[g] = st * chunk_decay[:, gsl] + _dot(b_f.T.astype(BF16), xd_b[:, gsl])

        y_diag = []
        for pair in range(gw // LANES):
            col0 = g * gw + pair * LANES
            xpair = xdt_b[:, col0:col0 + LANES]
            parts = []
            for hh in range(heads_per_pair):
                h = col0 // SSM_HEAD_DIM + hh
                seg = cs_e[:, h * SSM_HEAD_DIM:h * SSM_HEAD_DIM + 1] - cs_t[h:h + 1, :]
                decay = jnp.exp(jnp.where(causal, seg, NEG_BIG))
                parts.append(_dot((scores * decay).astype(BF16), xpair))
            y_diag.append(jnp.where(lane < SSM_HEAD_DIM, parts[0], parts[1]))
        y = jnp.concatenate(y_diag, axis=1) + y_off + dskip_ref[:, gsl] * xs[:, gsl]
        y = y * zs_ref[:, gsl]
        y = y * lax.rsqrt(jnp.mean(y * y, axis=1, keepdims=True) + RMS_EPS)
        y_ref[:, gsl] = (y * ng_ref[:, gsl]).astype(y_ref.dtype)


def _ssd(gates, xs, bc, dt_raw, dt_bias, a_log, d_skip, norm_g, width):
    s = xs.shape[0]
    rows = SSD_L
    bcw = 2 * SSM_GROUPS * SSM_STATE
    heads = width // SSM_HEAD_DIM

    def pad_heads(p):
        return jnp.pad(p.astype(F32), (0, LANES - heads)).reshape(1, LANES)

    def per_channel(p):
        return jnp.repeat(p.astype(F32), SSM_HEAD_DIM).reshape(1, width)

    z_blk = 1
    const = lambda c: (0, 0)
    kern = functools.partial(_ssd_kernel, rows=rows)
    return pl.pallas_call(
        kern,
        grid=(s // rows,),
        in_specs=[
            pl.BlockSpec((rows, width), lambda c: (c, z_blk)),
            pl.BlockSpec((rows, width), lambda c: (c, 0)),
            pl.BlockSpec((rows, bcw), lambda c: (c, 0)),
            pl.BlockSpec((rows, LANES), lambda c: (c, 0)),
            pl.BlockSpec((1, LANES), const),
            pl.BlockSpec((1, LANES), const),
            pl.BlockSpec((1, width), const),
            pl.BlockSpec((1, width), const),
        ],
        out_specs=pl.BlockSpec((rows, width), lambda c: (c, 0)),
        out_shape=jax.ShapeDtypeStruct((s, width), BF16),
        scratch_shapes=[pltpu.VMEM((SSM_GROUPS, SSM_STATE, width // SSM_GROUPS), F32)],
        compiler_params=pltpu.CompilerParams(
            dimension_semantics=("arbitrary",), vmem_limit_bytes=VMEM_LIMIT),
        name="ssd_scan",
    )(gates, xs, bc, dt_raw,
      pad_heads(dt_bias), pad_heads(a_log), per_channel(d_skip), norm_g.reshape(1, width))


def _pool_kernel(v_ref, g_ref, wg_ref, bg_ref, sc_ref, y_ref, tail_ref, ext_ref, wb_ref,
                 pa_ref, pb_ref, *, rows):
    i = pl.program_id(0)
    gc = wg_ref.shape[1]
    top = POOL_HALO + rows

    @pl.when(i == 0)
    def _():
        tail_ref[...] = jnp.zeros_like(tail_ref)
        wb_ref[...] = wg_ref[...].astype(BF16)
        pa_ref[0:SUBLANES, :] = jnp.zeros((SUBLANES, gc), F32)
        pb_ref[0:SUBLANES, :] = jnp.zeros((SUBLANES, gc), F32)

    ext_ref[0:POOL_HALO, :] = tail_ref[...]
    ext_ref[POOL_HALO:top, :] = v_ref[...]
    tail_ref[...] = v_ref[rows - POOL_HALO:rows, :]
    pos = (i * rows + lax.broadcasted_iota(jnp.int32, (rows, 1), 0) + 1).astype(F32)

    for gi, w in enumerate(POOL_WINDOWS):
        cols = slice(gi * gc, (gi + 1) * gc)
        v = v_ref[:, cols]
        src, span = ext_ref.at[:, cols], 1
        for dst in (pa_ref, pb_ref, pa_ref, pb_ref):
            if span == w:
                break
            dst[SUBLANES:top, :] = (src[SUBLANES:top, :]
                                    + src[SUBLANES - span:top - span, :])
            src, span = dst, 2 * span
        acc = src[POOL_HALO:top, :]
        pooled = acc * (1.0 / jnp.minimum(pos, float(w))) - v
        m = _dot(pooled.astype(BF16), wb_ref[gi]) + bg_ref[:, cols]
        y_ref[:, cols] = (m * sc_ref[:, cols] * g_ref[:, cols]).astype(y_ref.dtype)


def _pool(v, gs, w_grp, layer, b_grp, scale):
    s, width = v.shape
    rows = POOL_TM
    _, ng, gc, _ = w_grp.shape
    kern = functools.partial(_pool_kernel, rows=rows)
    return pl.pallas_call(
        kern,
        grid=(s // rows,),
        in_specs=[
            pl.BlockSpec((rows, width), lambda i: (i, 0)),
            pl.BlockSpec((rows, width), lambda i: (i, 0)),
            pl.BlockSpec((None, ng, gc, gc), lambda i: (layer, 0, 0, 0)),
            pl.BlockSpec((1, width), lambda i: (0, 0)),
            pl.BlockSpec((1, width), lambda i: (0, 0)),
        ],
        out_specs=pl.BlockSpec((rows, width), lambda i: (i, 0)),
        out_shape=jax.ShapeDtypeStruct((s, width), BF16),
        scratch_shapes=[pltpu.VMEM((POOL_HALO, width), F32),
                        pltpu.VMEM((POOL_HALO + rows, width), F32),
                        pltpu.VMEM((ng, gc, gc), BF16),
                        pltpu.VMEM((POOL_HALO + rows, gc), F32),
                        pltpu.VMEM((POOL_HALO + rows, gc), F32)],
        compiler_params=pltpu.CompilerParams(
            dimension_semantics=("arbitrary",), vmem_limit_bytes=VMEM_LIMIT),
        name="pool_mixer",
    )(v, gs, w_grp, b_grp.reshape(1, width), scale.reshape(1, width))


def _out_ln_kernel(ya_ref, yb_ref, w_ref, x_ref, g_ref, b_ref, xo_ref, xob_ref):
    half = ya_ref.shape[1]
    y = _dot(ya_ref[...], w_ref[:half, :]) + _dot(yb_ref[...], w_ref[half:, :])
    h = DEEPNORM_ALPHA * x_ref[...] + y
    mu = jnp.mean(h, axis=1, keepdims=True)
    d = h - mu
    var = jnp.mean(d * d, axis=1, keepdims=True)
    out = d * lax.rsqrt(var + LN_EPS) * g_ref[...] + b_ref[...]
    xo_ref[...] = out
    xob_ref[...] = out.astype(BF16)


def _out_ln(ya, yb, ya_blk, yb_blk, w_out, x, ln_g, ln_b):
    s, d = x.shape
    half = w_out.shape[0] // 2
    tm = OUT_TM
    return pl.pallas_call(
        _out_ln_kernel,
        grid=(s // tm,),
        in_specs=[
            pl.BlockSpec((tm, half), lambda i: (i, ya_blk)),
            pl.BlockSpec((tm, half), lambda i: (i, yb_blk)),
            pl.BlockSpec((2 * half, d), lambda i: (0, 0), pipeline_mode=pl.Buffered(1)),
            pl.BlockSpec((tm, d), lambda i: (i, 0)),
            pl.BlockSpec((1, d), lambda i: (0, 0)),
            pl.BlockSpec((1, d), lambda i: (0, 0)),
        ],
        out_specs=[pl.BlockSpec((tm, d), lambda i: (i, 0)),
                   pl.BlockSpec((tm, d), lambda i: (i, 0))],
        out_shape=[jax.ShapeDtypeStruct((s, d), F32), jax.ShapeDtypeStruct((s, d), BF16)],
        compiler_params=pltpu.CompilerParams(
            dimension_semantics=("parallel",), vmem_limit_bytes=VMEM_LIMIT),
        name="out_proj_layernorm",
    )(ya, yb, w_out.astype(BF16), x, ln_g.reshape(1, d), ln_b.reshape(1, d))


def _even_layer(x, xb, i, w_in, conv_w, conv_b, dt_bias, a_log, d_skip, ssm_norm_g,
                lq1, lk1, lq2, lk2, subln_g, w_out, ln_g, ln_b, lam_init):
    d = x.shape[1]
    att_w = d // 2
    ssm_w = d // 2
    heads = ssm_w // SSM_HEAD_DIM
    n_qkv = 3 * att_w
    n_gates = att_w + ssm_w
    n_bc = 2 * SSM_GROUPS * SSM_STATE
    c_x = n_qkv + n_gates
    c_bc = c_x + ssm_w
    conv_x = (conv_w[:, :ssm_w], conv_b[:ssm_w].reshape(1, ssm_w))
    conv_bc = (conv_w[:, ssm_w:], conv_b[ssm_w:].reshape(1, n_bc))
    qkv = _matmul(xb, w_in, i, 0, n_qkv, MM_TN, BF16, "even_in_qkv", True)
    gates = _matmul(xb, w_in, i, n_qkv, n_gates, MM_TN, F32, "even_in_gates", True, "silu")
    w_dt = jnp.pad(w_in[i, c_bc + n_bc:, :], ((0, LANES - heads), (0, 0)))
    xs, dt_raw = _xconv_dt(xb, w_in, i, c_x, ssm_w, w_dt, conv_x)
    bc = _matmul(xb, w_in, i, c_bc, n_bc, n_bc, F32, "even_in_bc", True, "conv_silu", conv_bc)
    lam_params = jnp.stack([lq1, lk1, lq2, lk2]).astype(F32)
    y_att = _attention(qkv, gates, lam_params, subln_g.reshape(1, DIFF_V_DIM), lam_init)
    y_ssm = _ssd(gates, xs, bc, dt_raw, dt_bias, a_log, d_skip, ssm_norm_g, ssm_w)
    return _out_ln(y_att, y_ssm, 0, 0, w_out, x, ln_g, ln_b)


def _odd_layer(x, xb, i, w_in, w_grp, b_grp, scale, w_out, ln_g, ln_b):
    width = w_in.shape[2] // 2
    v = _matmul(xb, w_in, i, 0, width, MM_TN, F32, "odd_in_v")
    gs = _matmul(xb, w_in, i, width, width, MM_TN, F32, "odd_in_gate", epilogue="silu")
    y = _pool(v, gs, w_grp, i, b_grp, scale)
    return _out_ln(y, y, 0, 1, w_out, x, ln_g, ln_b)


def kernel(x, ev_w_in, ev_conv_w, ev_conv_b, ev_dt_bias, ev_a_log, ev_d_skip, ev_ssm_norm_g, ev_lambda_q1, ev_lambda_k1, ev_lambda_q2, ev_lambda_k2, ev_subln_g, ev_w_out, od_w_in, od_w_grp, od_b_grp, od_scale, od_w_out, ln_g, ln_b):
    bsz, s, d = x.shape
    ev_w_in = jnp.swapaxes(ev_w_in, 1, 2)
    outs = []
    for b in range(bsz):
        xf = x[b]
        xb = xf.astype(BF16)
        for l in range(DEPTH):
            i = l // 2
            if l % 2 == 0:
                lam_init = 0.8 - 0.6 * math.exp(-0.3 * l)
                xf, xb = _even_layer(
                    xf, xb, i, ev_w_in, ev_conv_w[i], ev_conv_b[i], ev_dt_bias[i], ev_a_log[i],
                    ev_d_skip[i], ev_ssm_norm_g[i], ev_lambda_q1[i], ev_lambda_k1[i],
                    ev_lambda_q2[i], ev_lambda_k2[i], ev_subln_g[i], ev_w_out[i],
                    ln_g[l], ln_b[l], lam_init)
            else:
                xf, xb = _odd_layer(xf, xb, i, od_w_in, od_w_grp, od_b_grp[i], od_scale[i],
                                    od_w_out[i], ln_g[l], ln_b[l])
        outs.append(xf)
    return jnp.stack(outs)
```

```python
import functools
import math

import jax
import jax.numpy as jnp
from jax import lax
from jax.experimental import pallas as pl
from jax.experimental.pallas import tpu as pltpu

F32 = jnp.float32
BF16 = jnp.bfloat16

DEPTH = 4
CHUNK = 64
CHUNK_SHIFT = CHUNK.bit_length() - 1
DIFF_HEAD_DIM = 64
DIFF_V_DIM = 2 * DIFF_HEAD_DIM
SSM_HEAD_DIM = 64
HEAD_SHIFT = SSM_HEAD_DIM.bit_length() - 1
SSM_GROUPS = 2
SSM_STATE = 128
CONV_WIDTH = 4
POOL_WINDOWS = (2, 4, 8, 16)
DEEPNORM_ALPHA = (2.0 * DEPTH) ** 0.25
LN_EPS = 1e-5
RMS_EPS = 1e-5

LANES = 128
SUBLANES = 8
NEG_BIG = -1e30
FINITE_MAX = 3.0e38
VMEM_LIMIT = 56 * 1024 * 1024

ATTN_TQ = 256
ATTN_TK = 1024
ATTN_HP = 4
SSD_L = 256
MM_TM = 1024
MM_TN = 1024
OUT_TM = 512
POOL_TM = 512
POOL_HALO = 24


def _silu(x):
    h = 0.5 * x
    return h + h * jnp.tanh(h)


def _softplus(x):
    return jnp.maximum(x, 0.0) + jnp.log(1.0 + jnp.exp(-jnp.abs(x)))


def _split3(a):
    hi = a.astype(BF16)
    r1 = a - hi.astype(F32)
    mid = r1.astype(BF16)
    lo = (r1 - mid.astype(F32)).astype(BF16)
    return hi, mid, lo


def _dot(a, b):
    return jnp.dot(a, b, preferred_element_type=F32)


def _dot_nt(a, b):
    return lax.dot_general(a, b, (((1,), (1,)), ((), ())), preferred_element_type=F32)


def _conv_silu(raw, tail_ref, ext_ref, w_ref, b_ref):
    rows = raw.shape[0]
    ext_ref[0:SUBLANES, :] = tail_ref[...]
    ext_ref[SUBLANES:SUBLANES + rows, :] = raw
    tail_ref[...] = raw[rows - SUBLANES:rows, :]
    acc = b_ref[...]
    for t in range(CONV_WIDTH):
        start = SUBLANES - (CONV_WIDTH - 1) + t
        acc = acc + w_ref[t:t + 1, :] * ext_ref[start:start + rows, :]
    return _silu(acc)


def _matmul_kernel(*refs, w_is_nk, epilogue):
    if epilogue == "conv_silu":
        x_ref, w_ref, cw_ref, cb_ref, o_ref, wb_ref, tail_ref, ext_ref = refs
    else:
        x_ref, w_ref, o_ref, wb_ref = refs

    @pl.when(pl.program_id(1) == 0)
    def _():
        wb_ref[...] = w_ref[...].astype(BF16)
        if epilogue == "conv_silu":
            tail_ref[...] = jnp.zeros_like(tail_ref)

    dot = _dot_nt if w_is_nk else _dot
    y = dot(x_ref[...].astype(BF16), wb_ref[...])
    if epilogue == "silu":
        y = _silu(y)
    elif epilogue == "conv_silu":
        y = _conv_silu(y, tail_ref, ext_ref, cw_ref, cb_ref)
    o_ref[...] = y.astype(o_ref.dtype)


def _matmul(x, w, layer, col0, n, tn, out_dtype, name, w_is_nk=False, epilogue=None, conv=None):
    m, k = x.shape
    tm = min(MM_TM, m)
    assert n % tn == 0 and col0 % tn == 0 and m % tm == 0
    cb = col0 // tn
    if w_is_nk:
        w_spec = pl.BlockSpec((None, tn, k), lambda j, i: (layer, cb + j, 0))
    else:
        w_spec = pl.BlockSpec((None, k, tn), lambda j, i: (layer, 0, cb + j))
    in_specs = [pl.BlockSpec((tm, k), lambda j, i: (i, 0)), w_spec]
    scratch = [pltpu.VMEM((tn, k) if w_is_nk else (k, tn), BF16)]
    operands = [x, w]
    if epilogue == "conv_silu":
        in_specs += [pl.BlockSpec((CONV_WIDTH, tn), lambda j, i: (0, j)),
                     pl.BlockSpec((1, tn), lambda j, i: (0, j))]
        scratch += [pltpu.VMEM((SUBLANES, tn), F32), pltpu.VMEM((SUBLANES + tm, tn), F32)]
        operands += list(conv)
    return pl.pallas_call(
        functools.partial(_matmul_kernel, w_is_nk=w_is_nk, epilogue=epilogue),
        grid=(n // tn, m // tm),
        in_specs=in_specs,
        out_specs=pl.BlockSpec((tm, tn), lambda j, i: (i, j)),
        out_shape=jax.ShapeDtypeStruct((m, n), out_dtype),
        scratch_shapes=scratch,
        compiler_params=pltpu.CompilerParams(
            dimension_semantics=("arbitrary", "arbitrary"), vmem_limit_bytes=VMEM_LIMIT),
        name=name,
    )(*operands)


def _xconv_dt_kernel(x_ref, w_ref, wdt_ref, cw_ref, cb_ref, o_ref, dt_ref,
                     wb_ref, wdtb_ref, tail_ref, ext_ref):
    @pl.when(pl.program_id(0) == 0)
    def _():
        wb_ref[...] = w_ref[...].astype(BF16)
        wdtb_ref[...] = wdt_ref[...].astype(BF16)
        tail_ref[...] = jnp.zeros_like(tail_ref)

    x = x_ref[...].astype(BF16)
    o_ref[...] = _conv_silu(_dot_nt(x, wb_ref[...]), tail_ref, ext_ref, cw_ref, cb_ref)
    dt_ref[...] = _dot_nt(x, wdtb_ref[...])


def _xconv_dt(x, w, layer, row0, n, w_dt, conv):
    m, k = x.shape
    tm = min(MM_TM, m)
    assert row0 % n == 0 and m % tm == 0
    const = lambda i: (0, 0)
    return pl.pallas_call(
        _xconv_dt_kernel,
        grid=(m // tm,),
        in_specs=[pl.BlockSpec((tm, k), lambda i: (i, 0)),
                  pl.BlockSpec((None, n, k), lambda i: (layer, row0 // n, 0)),
                  pl.BlockSpec((LANES, k), const),
                  pl.BlockSpec((CONV_WIDTH, n), const),
                  pl.BlockSpec((1, n), const)],
        out_specs=[pl.BlockSpec((tm, n), lambda i: (i, 0)),
                   pl.BlockSpec((tm, LANES), lambda i: (i, 0))],
        out_shape=[jax.ShapeDtypeStruct((m, n), F32), jax.ShapeDtypeStruct((m, LANES), F32)],
        scratch_shapes=[pltpu.VMEM((n, k), BF16), pltpu.VMEM((LANES, k), BF16),
                        pltpu.VMEM((SUBLANES, n), F32), pltpu.VMEM((SUBLANES + tm, n), F32)],
        compiler_params=pltpu.CompilerParams(
            dimension_semantics=("arbitrary",), vmem_limit_bytes=VMEM_LIMIT),
        name="even_in_x_dt",
    )(x, w, w_dt, *conv)


def _attn_kernel(lam_ref, subg_ref, q_ref, k_ref, v_ref, g_ref, o_ref, acc_ref,
                 *, tq, tk, hp, lam_init):
    qi = pl.program_id(1)
    lp = lam_ref[...]
    lam = (jnp.exp(jnp.sum(lp[0:1] * lp[1:2], axis=1, keepdims=True))
           - jnp.exp(jnp.sum(lp[2:3] * lp[3:4], axis=1, keepdims=True)) + lam_init)

    lane = lax.broadcasted_iota(jnp.int32, (tq, DIFF_V_DIM), 1)
    row = lax.broadcasted_iota(jnp.int32, (2 * tq, 1), 0)
    q_chunk = (qi * tq + jnp.where(row >= tq, row - tq, row)) >> CHUNK_SHIFT

    def stacked_q(j):
        q = q_ref[:, j * DIFF_V_DIM:(j + 1) * DIFF_V_DIM] * (DIFF_HEAD_DIM ** -0.5)
        zero = jnp.zeros_like(q)
        return jnp.concatenate([jnp.where(lane < DIFF_HEAD_DIM, q, zero),
                                jnp.where(lane >= DIFF_HEAD_DIM, q, zero)], axis=0)

    qq = [stacked_q(j) for j in range(hp)]

    def kv(kb, j, size):
        ks = pl.multiple_of(kb * size, size)
        cols = slice(j * DIFF_V_DIM, (j + 1) * DIFF_V_DIM)
        k = k_ref[pl.ds(ks, size), cols]
        v1 = jnp.concatenate([v_ref[pl.ds(ks, size), cols],
                              jnp.ones((size, DIFF_V_DIM), BF16)], axis=1)
        return ks, k, v1

    def exact_block(kb, j, m, acc):
        _, k, v1 = kv(kb, j, tq)
        s = _dot_nt(qq[j], k)
        m_new = jnp.maximum(m, jnp.max(s, axis=1, keepdims=True))
        p = jnp.exp(s - m_new).astype(BF16)
        return m_new, jnp.exp(m - m_new) * acc + _dot(p, v1)

    def own_block(j):
        ks, k, v1 = kv(qi, j, tq)
        k_chunk = (ks + lax.broadcasted_iota(jnp.int32, (1, tq), 1)) >> CHUNK_SHIFT
        s = jnp.where(k_chunk <= q_chunk, _dot_nt(qq[j], k), NEG_BIG)
        m = jnp.max(s, axis=1, keepdims=True)
        return m, _dot(jnp.exp(s - m).astype(BF16), v1)

    def fast_block(kb, j, size, straddles):
        ks, k, v1 = kv(kb, j, size)
        x = _dot_nt(qq[j], k) - refs[j]
        if straddles:
            before = (ks + lax.broadcasted_iota(jnp.int32, (1, size), 1)) < qi * tq
            x = jnp.where(before, x, NEG_BIG)
        return _dot(jnp.exp(x).astype(BF16), v1)

    def accumulate(blocks):
        for j in range(hp):
            total = acc_ref[j]
            for kb, size, straddles in blocks:
                total = total + fast_block(kb, j, size, straddles)
            acc_ref[j] = total

    def finish(j, acc):
        cols = slice(j * DIFF_V_DIM, (j + 1) * DIFF_V_DIM)
        inv = 1.0 / acc[:, DIFF_V_DIM:]
        o = acc[:, :DIFF_V_DIM] * inv
        o = o[:tq] - lam * o[tq:]
        bad = jnp.maximum(jnp.max(jnp.where(jnp.abs(o) < FINITE_MAX, 0.0, 1.0)),
                          jnp.max(jnp.where(inv > 0.0, 0.0, 1.0)))
        o = o * lax.rsqrt(jnp.mean(o * o, axis=1, keepdims=True) + RMS_EPS)
        o = o * subg_ref[...] * (1.0 - lam_init)
        o_ref[:, cols] = (o * g_ref[:, cols]).astype(o_ref.dtype)
        return bad

    refs = []
    for j in range(hp):
        m, acc = own_block(j)
        refs.append(m)
        acc_ref[j] = acc

    n_full = (qi * tq) // tk
    rest = qi * tq - n_full * tk

    def fast_pair(kp, carry):
        accumulate([(2 * kp, tk, False), (2 * kp + 1, tk, False)])
        return carry

    lax.fori_loop(0, n_full // 2, fast_pair, 0)

    @pl.when(n_full % 2 == 1)
    def _():
        accumulate([(n_full - 1, tk, False)])

    @pl.when(jnp.logical_and(rest > 0, rest <= tk // 2))
    def _():
        accumulate([(2 * n_full, tk // 2, True)])

    @pl.when(rest > tk // 2)
    def _():
        accumulate([(n_full, tk, True)])

    overflow = jnp.float32(0.0)
    for j in range(hp):
        overflow = jnp.maximum(overflow, finish(j, acc_ref[j]))

    @pl.when(overflow > 0.0)
    def _():
        def exact_step(kb, carry):
            return tuple(exact_block(kb, j, *carry[j]) for j in range(hp))
        carry = lax.fori_loop(0, qi, exact_step, tuple(own_block(j) for j in range(hp)))
        for j in range(hp):
            finish(j, carry[j][1])


def _attention(qkv, gates, lam_params, subln_g, lam_init):
    s = qkv.shape[0]
    heads = qkv.shape[1] // (3 * DIFF_V_DIM)
    tq, tk, hp = ATTN_TQ, ATTN_TK, ATTN_HP
    groups = heads // hp
    bw = hp * DIFF_V_DIM
    kern = functools.partial(_attn_kernel, tq=tq, tk=tk, hp=hp, lam_init=lam_init)
    return pl.pallas_call(
        kern,
        grid=(groups, s // tq),
        in_specs=[
            pl.BlockSpec((4, DIFF_HEAD_DIM), lambda h, i: (0, 0)),
            pl.BlockSpec((1, DIFF_V_DIM), lambda h, i: (0, 0)),
            pl.BlockSpec((tq, bw), lambda h, i: (i, h)),
            pl.BlockSpec((s, bw), lambda h, i: (0, groups + h)),
            pl.BlockSpec((s, bw), lambda h, i: (0, 2 * groups + h)),
            pl.BlockSpec((tq, bw), lambda h, i: (i, h)),
        ],
        out_specs=pl.BlockSpec((tq, bw), lambda h, i: (i, h)),
        out_shape=jax.ShapeDtypeStruct((s, heads * DIFF_V_DIM), BF16),
        scratch_shapes=[pltpu.VMEM((hp, 2 * tq, 2 * DIFF_V_DIM), F32)],
        compiler_params=pltpu.CompilerParams(
            dimension_semantics=("parallel", "parallel"), vmem_limit_bytes=VMEM_LIMIT),
        name="diff_attention",
    )(lam_params, subln_g, qkv, qkv, qkv, gates)


def _ssd_kernel(zs_ref, xs_ref, bc_ref, dt_ref, dtb_ref, alog_ref, dskip_ref, ng_ref, y_ref,
                state_ref, *, rows):
    c = pl.program_id(0)
    width = xs_ref.shape[1]
    gw = width // SSM_GROUPS
    heads_per_pair = LANES // SSM_HEAD_DIM

    @pl.when(c == 0)
    def _():
        state_ref[...] = jnp.zeros_like(state_ref)

    xs = xs_ref[...]
    bcv = bc_ref[...]

    dtc = _softplus(dt_ref[...] + dtb_ref[...])
    adt = -jnp.exp(alog_ref[...]) * dtc
    ri = lax.broadcasted_iota(jnp.int32, (rows, rows), 0)
    ci = lax.broadcasted_iota(jnp.int32, (rows, rows), 1)
    causal = ci <= ri
    tri = jnp.where(causal, 1.0, 0.0).astype(BF16)
    csc = sum(_dot(tri, part) for part in _split3(adt))
    cs_t = csc.T

    er = lax.broadcasted_iota(jnp.int32, (LANES, width), 0)
    ec = lax.broadcasted_iota(jnp.int32, (LANES, width), 1)
    expand = jnp.where((ec >> HEAD_SHIFT) == er, 1.0, 0.0).astype(BF16)
    dt_e = sum(_dot(part, expand) for part in _split3(dtc))
    cs_e = sum(_dot(part, expand) for part in _split3(csc))
    cs_last = cs_e[rows - 1:rows, :]

    xdt = xs * dt_e
    xdt_b = xdt.astype(BF16)
    xd_b = (xdt * jnp.exp(cs_last - cs_e)).astype(BF16)
    ecs = jnp.exp(cs_e)
    chunk_decay = jnp.exp(cs_last)
    lane = lax.broadcasted_iota(jnp.int32, (rows, LANES), 1)

    for g in range(SSM_GROUPS):
        gsl = slice(g * gw, (g + 1) * gw)
        b_f = bcv[:, g * SSM_STATE:(g + 1) * SSM_STATE]
        c_b = bcv[:, (SSM_GROUPS + g) * SSM_STATE:(SSM_GROUPS + g + 1) * SSM_STATE].astype(BF16)
        scores = _dot_nt(c_b, b_f.astype(BF16))
        st = state_ref[g]
        y_off = _dot(c_b, st.astype(BF16)) * ecs[:, gsl]
        state_ref[g] = st * chunk_decay[:, gsl] + _dot(b_f.T.astype(BF16), xd_b[:, gsl])

        y_diag = []
        for pair in range(gw // LANES):
            col0 = g * gw + pair * LANES
            xpair = xdt_b[:, col0:col0 + LANES]
            parts = []
            for hh in range(heads_per_pair):
                h = col0 // SSM_HEAD_DIM + hh
                seg = cs_e[:, h * SSM_HEAD_DIM:h * SSM_HEAD_DIM + 1] - cs_t[h:h + 1, :]
                decay = jnp.exp(jnp.where(causal, seg, NEG_BIG))
                parts.append(_dot((scores * decay).astype(BF16), xpair))
            y_diag.append(jnp.where(lane < SSM_HEAD_DIM, parts[0], parts[1]))
        y = jnp.concatenate(y_diag, axis=1) + y_off + dskip_ref[:, gsl] * xs[:, gsl]
        y = y * zs_ref[:, gsl]
        y = y * lax.rsqrt(jnp.mean(y * y, axis=1, keepdims=True) + RMS_EPS)
        y_ref[:, gsl] = (y * ng_ref[:, gsl]).astype(y_ref.dtype)


def _ssd(gates, xs, bc, dt_raw, dt_bias, a_log, d_skip, norm_g, width):
    s = xs.shape[0]
    rows = SSD_L
    bcw = 2 * SSM_GROUPS * SSM_STATE
    heads = width // SSM_HEAD_DIM

    def pad_heads(p):
        return jnp.pad(p.astype(F32), (0, LANES - heads)).reshape(1, LANES)

    def per_channel(p):
        return jnp.repeat(p.astype(F32), SSM_HEAD_DIM).reshape(1, width)

    z_blk = 1
    const = lambda c: (0, 0)
    kern = functools.partial(_ssd_kernel, rows=rows)
    return pl.pallas_call(
        kern,
        grid=(s // rows,),
        in_specs=[
            pl.BlockSpec((rows, width), lambda c: (c, z_blk)),
            pl.BlockSpec((rows, width), lambda c: (c, 0)),
            pl.BlockSpec((rows, bcw), lambda c: (c, 0)),
            pl.BlockSpec((rows, LANES), lambda c: (c, 0)),
            pl.BlockSpec((1, LANES), const),
            pl.BlockSpec((1, LANES), const),
            pl.BlockSpec((1, width), const),
            pl.BlockSpec((1, width), const),
        ],
        out_specs=pl.BlockSpec((rows, width), lambda c: (c, 0)),
        out_shape=jax.ShapeDtypeStruct((s, width), BF16),
        scratch_shapes=[pltpu.VMEM((SSM_GROUPS, SSM_STATE, width // SSM_GROUPS), F32)],
        compiler_params=pltpu.CompilerParams(
            dimension_semantics=("arbitrary",), vmem_limit_bytes=VMEM_LIMIT),
        name="ssd_scan",
    )(gates, xs, bc, dt_raw,
      pad_heads(dt_bias), pad_heads(a_log), per_channel(d_skip), norm_g.reshape(1, width))


def _pool_kernel(v_ref, g_ref, wg_ref, bg_ref, sc_ref, y_ref, tail_ref, ext_ref, wb_ref,
                 pa_ref, pb_ref, *, rows):
    i = pl.program_id(0)
    gc = wg_ref.shape[1]
    top = POOL_HALO + rows

    @pl.when(i == 0)
    def _():
        tail_ref[...] = jnp.zeros_like(tail_ref)
        wb_ref[...] = wg_ref[...].astype(BF16)
        pa_ref[0:SUBLANES, :] = jnp.zeros((SUBLANES, gc), F32)
        pb_ref[0:SUBLANES, :] = jnp.zeros((SUBLANES, gc), F32)

    ext_ref[0:POOL_HALO, :] = tail_ref[...]
    ext_ref[POOL_HALO:top, :] = v_ref[...]
    tail_ref[...] = v_ref[rows - POOL_HALO:rows, :]
    pos = (i * rows + lax.broadcasted_iota(jnp.int32, (rows, 1), 0) + 1).astype(F32)

    for gi, w in enumerate(POOL_WINDOWS):
        cols = slice(gi * gc, (gi + 1) * gc)
        v = v_ref[:, cols]
        src, span = ext_ref.at[:, cols], 1
        for dst in (pa_ref, pb_ref, pa_ref, pb_ref):
            if span == w:
                break
            dst[SUBLANES:top, :] = (src[SUBLANES:top, :]
                                    + src[SUBLANES - span:top - span, :])
            src, span = dst, 2 * span
        acc = src[POOL_HALO:top, :]
        pooled = acc * (1.0 / jnp.minimum(pos, float(w))) - v
        m = _dot(pooled.astype(BF16), wb_ref[gi]) + bg_ref[:, cols]
        y_ref[:, cols] = (m * sc_ref[:, cols] * g_ref[:, cols]).astype(y_ref.dtype)


def _pool(v, gs, w_grp, layer, b_grp, scale):
    s, width = v.shape
    rows = POOL_TM
    _, ng, gc, _ = w_grp.shape
    kern = functools.partial(_pool_kernel, rows=rows)
    return pl.pallas_call(
        kern,
        grid=(s // rows,),
        in_specs=[
            pl.BlockSpec((rows, width), lambda i: (i, 0)),
            pl.BlockSpec((rows, width), lambda i: (i, 0)),
            pl.BlockSpec((None, ng, gc, gc), lambda i: (layer, 0, 0, 0)),
            pl.BlockSpec((1, width), lambda i: (0, 0)),
            pl.BlockSpec((1, width), lambda i: (0, 0)),
        ],
        out_specs=pl.BlockSpec((rows, width), lambda i: (i, 0)),
        out_shape=jax.ShapeDtypeStruct((s, width), BF16),
        scratch_shapes=[pltpu.VMEM((POOL_HALO, width), F32),
                        pltpu.VMEM((POOL_HALO + rows, width), F32),
                        pltpu.VMEM((ng, gc, gc), BF16),
                        pltpu.VMEM((POOL_HALO + rows, gc), F32),
                        pltpu.VMEM((POOL_HALO + rows, gc), F32)],
        compiler_params=pltpu.CompilerParams(
            dimension_semantics=("arbitrary",), vmem_limit_bytes=VMEM_LIMIT),
        name="pool_mixer",
    )(v, gs, w_grp, b_grp.reshape(1, width), scale.reshape(1, width))


def _out_ln_kernel(ya_ref, yb_ref, w_ref, x_ref, g_ref, b_ref, xo_ref, xob_ref):
    half = ya_ref.shape[1]
    y = _dot(ya_ref[...], w_ref[:half, :]) + _dot(yb_ref[...], w_ref[half:, :])
    h = DEEPNORM_ALPHA * x_ref[...] + y
    mu = jnp.mean(h, axis=1, keepdims=True)
    d = h - mu
    var = jnp.mean(d * d, axis=1, keepdims=True)
    out = d * lax.rsqrt(var + LN_EPS) * g_ref[...] + b_ref[...]
    xo_ref[...] = out
    xob_ref[...] = out.astype(BF16)


def _out_ln(ya, yb, ya_blk, yb_blk, w_out, x, ln_g, ln_b):
    s, d = x.shape
    half = w_out.shape[0] // 2
    tm = OUT_TM
    return pl.pallas_call(
        _out_ln_kernel,
        grid=(s // tm,),
        in_specs=[
            pl.BlockSpec((tm, half), lambda i: (i, ya_blk)),
            pl.BlockSpec((tm, half), lambda i: (i, yb_blk)),
            pl.BlockSpec((2 * half, d), lambda i: (0, 0), pipeline_mode=pl.Buffered(1)),
            pl.BlockSpec((tm, d), lambda i: (i, 0)),
            pl.BlockSpec((1, d), lambda i: (0, 0)),
            pl.BlockSpec((1, d), lambda i: (0, 0)),
        ],
        out_specs=[pl.BlockSpec((tm, d), lambda i: (i, 0)),
                   pl.BlockSpec((tm, d), lambda i: (i, 0))],
        out_shape=[jax.ShapeDtypeStruct((s, d), F32), jax.ShapeDtypeStruct((s, d), BF16)],
        compiler_params=pltpu.CompilerParams(
            dimension_semantics=("parallel",), vmem_limit_bytes=VMEM_LIMIT),
        name="out_proj_layernorm",
    )(ya, yb, w_out.astype(BF16), x, ln_g.reshape(1, d), ln_b.reshape(1, d))


def _even_layer(x, xb, i, w_in, conv_w, conv_b, dt_bias, a_log, d_skip, ssm_norm_g,
                lq1, lk1, lq2, lk2, subln_g, w_out, ln_g, ln_b, lam_init):
    d = x.shape[1]
    att_w = d // 2
    ssm_w = d // 2
    heads = ssm_w // SSM_HEAD_DIM
    n_qkv = 3 * att_w
    n_gates = att_w + ssm_w
    n_bc = 2 * SSM_GROUPS * SSM_STATE
    c_x = n_qkv + n_gates
    c_bc = c_x + ssm_w
    conv_x = (conv_w[:, :ssm_w], conv_b[:ssm_w].reshape(1, ssm_w))
    conv_bc = (conv_w[:, ssm_w:], conv_b[ssm_w:].reshape(1, n_bc))
    qkv = _matmul(xb, w_in, i, 0, n_qkv, MM_TN, BF16, "even_in_qkv", True)
    gates = _matmul(xb, w_in, i, n_qkv, n_gates, MM_TN, F32, "even_in_gates", True, "silu")
    w_dt = jnp.pad(w_in[i, c_bc + n_bc:, :], ((0, LANES - heads), (0, 0)))
    xs, dt_raw = _xconv_dt(xb, w_in, i, c_x, ssm_w, w_dt, conv_x)
    bc = _matmul(xb, w_in, i, c_bc, n_bc, n_bc, F32, "even_in_bc", True, "conv_silu", conv_bc)
    lam_params = jnp.stack([lq1, lk1, lq2, lk2]).astype(F32)
    y_att = _attention(qkv, gates, lam_params, subln_g.reshape(1, DIFF_V_DIM), lam_init)
    y_ssm = _ssd(gates, xs, bc, dt_raw, dt_bias, a_log, d_skip, ssm_norm_g, ssm_w)
    return _out_ln(y_att, y_ssm, 0, 0, w_out, x, ln_g, ln_b)


def _odd_layer(x, xb, i, w_in, w_grp, b_grp, scale, w_out, ln_g, ln_b):
    width = w_in.shape[2] // 2
    v = _matmul(xb, w_in, i, 0, width, MM_TN, F32, "odd_in_v")
    gs = _matmul(xb, w_in, i, width, width, MM_TN, F32, "odd_in_gate", epilogue="silu")
    y = _pool(v, gs, w_grp, i, b_grp, scale)
    return _out_ln(y, y, 0, 1, w_out, x, ln_g, ln_b)


def kernel(x, ev_w_in, ev_conv_w, ev_conv_b, ev_dt_bias, ev_a_log, ev_d_skip, ev_ssm_norm_g, ev_lambda_q1, ev_lambda_k1, ev_lambda_q2, ev_lambda_k2, ev_subln_g, ev_w_out, od_w_in, od_w_grp, od_b_grp, od_scale, od_w_out, ln_g, ln_b):
    bsz, s, d = x.shape
    ev_w_in = jnp.swapaxes(ev_w_in, 1, 2)
    outs = []
    for b in range(bsz):
        xf = x[b]
        xb = xf
        for l in range(DEPTH):
            i = l // 2
            if l % 2 == 0:
                lam_init = 0.8 - 0.6 * math.exp(-0.3 * l)
                xf, xb = _even_layer(
                    xf, xb, i, ev_w_in, ev_conv_w[i], ev_conv_b[i], ev_dt_bias[i], ev_a_log[i],
                    ev_d_skip[i], ev_ssm_norm_g[i], ev_lambda_q1[i], ev_lambda_k1[i],
                    ev_lambda_q2[i], ev_lambda_k2[i], ev_subln_g[i], ev_w_out[i],
                    ln_g[l], ln_b[l], lam_init)
            else:
                xf, xb = _odd_layer(xf, xb, i, od_w_in, od_w_grp, od_b_grp[i], od_scale[i],
                                    od_w_out[i], ln_g[l], ln_b[l])
        outs.append(xf)
    return jnp.stack(outs)
```

```python
import functools
import math

import jax
import jax.numpy as jnp
from jax import lax
from jax.experimental import pallas as pl
from jax.experimental.pallas import tpu as pltpu

F32 = jnp.float32
BF16 = jnp.bfloat16

DEPTH = 4
CHUNK = 64
CHUNK_SHIFT = CHUNK.bit_length() - 1
DIFF_HEAD_DIM = 64
DIFF_V_DIM = 2 * DIFF_HEAD_DIM
SSM_HEAD_DIM = 64
HEAD_SHIFT = SSM_HEAD_DIM.bit_length() - 1
SSM_GROUPS = 2
SSM_STATE = 128
CONV_WIDTH = 4
POOL_WINDOWS = (2, 4, 8, 16)
DEEPNORM_ALPHA = (2.0 * DEPTH) ** 0.25
LN_EPS = 1e-5
RMS_EPS = 1e-5

LANES = 128
SUBLANES = 8
NEG_BIG = -1e30
FINITE_MAX = 3.0e38
VMEM_LIMIT = 56 * 1024 * 1024

ATTN_TQ = 256
ATTN_TK = 1024
ATTN_HP = 4
SSD_L = 256
MM_TM = 1024
MM_TN = 1024
OUT_TM = 512
POOL_TM = 512
POOL_HALO = 24


def _silu(x):
    h = 0.5 * x
    return h + h * jnp.tanh(h)


def _softplus(x):
    return jnp.maximum(x, 0.0) + jnp.log(1.0 + jnp.exp(-jnp.abs(x)))


def _split3(a):
    hi = a.astype(BF16)
    r1 = a - hi.astype(F32)
    mid = r1.astype(BF16)
    lo = (r1 - mid.astype(F32)).astype(BF16)
    return hi, mid, lo


def _dot(a, b):
    return jnp.dot(a, b, preferred_element_type=F32)


def _dot_nt(a, b):
    return lax.dot_general(a, b, (((1,), (1,)), ((), ())), preferred_element_type=F32)


def _conv_silu(raw, tail_ref, ext_ref, w_ref, b_ref):
    rows = raw.shape[0]
    ext_ref[0:SUBLANES, :] = tail_ref[...]
    ext_ref[SUBLANES:SUBLANES + rows, :] = raw
    tail_ref[...] = raw[rows - SUBLANES:rows, :]
    acc = b_ref[...]
    for t in range(CONV_WIDTH):
        start = SUBLANES - (CONV_WIDTH - 1) + t
        acc = acc + w_ref[t:t + 1, :] * ext_ref[start:start + rows, :]
    return _silu(acc)


def _matmul_kernel(*refs, w_is_nk, epilogue):
    if epilogue == "conv_silu":
        x_ref, w_ref, cw_ref, cb_ref, o_ref, wb_ref, tail_ref, ext_ref = refs
    else:
        x_ref, w_ref, o_ref, wb_ref = refs

    @pl.when(pl.program_id(1) == 0)
    def _():
        wb_ref[...] = w_ref[...].astype(BF16)
        if epilogue == "conv_silu":
            tail_ref[...] = jnp.zeros_like(tail_ref)

    dot = _dot_nt if w_is_nk else _dot
    y = dot(x_ref[...].astype(BF16), wb_ref[...])
    if epilogue == "silu":
        y = _silu(y)
    elif epilogue == "conv_silu":
        y = _conv_silu(y, tail_ref, ext_ref, cw_ref, cb_ref)
    o_ref[...] = y.astype(o_ref.dtype)


def _matmul(x, w, layer, col0, n, tn, out_dtype, name, w_is_nk=False, epilogue=None, conv=None):
    m, k = x.shape
    tm = min(MM_TM, m)
    assert n % tn == 0 and col0 % tn == 0 and m % tm == 0
    cb = col0 // tn
    if w_is_nk:
        w_spec = pl.BlockSpec((None, tn, k), lambda j, i: (layer, cb + j, 0))
    else:
        w_spec = pl.BlockSpec((None, k, tn), lambda j, i: (layer, 0, cb + j))
    in_specs = [pl.BlockSpec((tm, k), lambda j, i: (i, 0)), w_spec]
    scratch = [pltpu.VMEM((tn, k) if w_is_nk else (k, tn), BF16)]
    operands = [x, w]
    if epilogue == "conv_silu":
        in_specs += [pl.BlockSpec((CONV_WIDTH, tn), lambda j, i: (0, j)),
                     pl.BlockSpec((1, tn), lambda j, i: (0, j))]
        scratch += [pltpu.VMEM((SUBLANES, tn), F32), pltpu.VMEM((SUBLANES + tm, tn), F32)]
        operands += list(conv)
    return pl.pallas_call(
        functools.partial(_matmul_kernel, w_is_nk=w_is_nk, epilogue=epilogue),
        grid=(n // tn, m // tm),
        in_specs=in_specs,
        out_specs=pl.BlockSpec((tm, tn), lambda j, i: (i, j)),
        out_shape=jax.ShapeDtypeStruct((m, n), out_dtype),
        scratch_shapes=scratch,
        compiler_params=pltpu.CompilerParams(
            dimension_semantics=("arbitrary", "arbitrary"), vmem_limit_bytes=VMEM_LIMIT),
        name=name,
    )(*operands)


def _xconv_dt_kernel(x_ref, w_ref, wdt_ref, cw_ref, cb_ref, o_ref, dt_ref,
                     wb_ref, wdtb_ref, tail_ref, ext_ref):
    @pl.when(pl.program_id(0) == 0)
    def _():
        wb_ref[...] = w_ref[...].astype(BF16)
        wdtb_ref[...] = wdt_ref[...].astype(BF16)
        tail_ref[...] = jnp.zeros_like(tail_ref)

    x = x_ref[...].astype(BF16)
    o_ref[...] = _conv_silu(_dot_nt(x, wb_ref[...]), tail_ref, ext_ref, cw_ref, cb_ref)
    dt_ref[...] = _dot_nt(x, wdtb_ref[...])


def _xconv_dt(x, w, layer, row0, n, w_dt, conv):
    m, k = x.shape
    tm = min(MM_TM, m)
    assert row0 % n == 0 and m % tm == 0
    const = lambda i: (0, 0)
    return pl.pallas_call(
        _xconv_dt_kernel,
        grid=(m // tm,),
        in_specs=[pl.BlockSpec((tm, k), lambda i: (i, 0)),
                  pl.BlockSpec((None, n, k), lambda i: (layer, row0 // n, 0)),
                  pl.BlockSpec((LANES, k), const),
                  pl.BlockSpec((CONV_WIDTH, n), const),
                  pl.BlockSpec((1, n), const)],
        out_specs=[pl.BlockSpec((tm, n), lambda i: (i, 0)),
                   pl.BlockSpec((tm, LANES), lambda i: (i, 0))],
        out_shape=[jax.ShapeDtypeStruct((m, n), F32), jax.ShapeDtypeStruct((m, LANES), F32)],
        scratch_shapes=[pltpu.VMEM((n, k), BF16), pltpu.VMEM((LANES, k), BF16),
                        pltpu.VMEM((SUBLANES, n), F32), pltpu.VMEM((SUBLANES + tm, n), F32)],
        compiler_params=pltpu.CompilerParams(
            dimension_semantics=("arbitrary",), vmem_limit_bytes=VMEM_LIMIT),
        name="even_in_x_dt",
    )(x, w, w_dt, *conv)


def _attn_kernel(lam_ref, subg_ref, q_ref, k_ref, v_ref, g_ref, o_ref, acc_ref,
                 *, tq, tk, hp, lam_init):
    qi = pl.program_id(1)
    lp = lam_ref[...]
    lam = (jnp.exp(jnp.sum(lp[0:1] * lp[1:2], axis=1, keepdims=True))
           - jnp.exp(jnp.sum(lp[2:3] * lp[3:4], axis=1, keepdims=True)) + lam_init)

    lane = lax.broadcasted_iota(jnp.int32, (tq, DIFF_V_DIM), 1)
    row = lax.broadcasted_iota(jnp.int32, (2 * tq, 1), 0)
    q_chunk = (qi * tq + jnp.where(row >= tq, row - tq, row)) >> CHUNK_SHIFT

    def stacked_q(j):
        q = q_ref[:, j * DIFF_V_DIM:(j + 1) * DIFF_V_DIM] * (DIFF_HEAD_DIM ** -0.5)
        zero = jnp.zeros_like(q)
        return jnp.concatenate([jnp.where(lane < DIFF_HEAD_DIM, q, zero),
                                jnp.where(lane >= DIFF_HEAD_DIM, q, zero)], axis=0)

    qq = [stacked_q(j) for j in range(hp)]

    def kv(kb, j, size):
        ks = pl.multiple_of(kb * size, size)
        cols = slice(j * DIFF_V_DIM, (j + 1) * DIFF_V_DIM)
        k = k_ref[pl.ds(ks, size), cols]
        v1 = jnp.concatenate([v_ref[pl.ds(ks, size), cols],
                              jnp.ones((size, DIFF_V_DIM), BF16)], axis=1)
        return ks, k, v1

    def exact_block(kb, j, m, acc):
        _, k, v1 = kv(kb, j, tq)
        s = _dot_nt(qq[j], k)
        m_new = jnp.maximum(m, jnp.max(s, axis=1, keepdims=True))
        p = jnp.exp(s - m_new).astype(BF16)
        return m_new, jnp.exp(m - m_new) * acc + _dot(p, v1)

    def own_block(j):
        ks, k, v1 = kv(qi, j, tq)
        k_chunk = (ks + lax.broadcasted_iota(jnp.int32, (1, tq), 1)) >> CHUNK_SHIFT
        s = jnp.where(k_chunk <= q_chunk, _dot_nt(qq[j], k), NEG_BIG)
        m = jnp.max(s, axis=1, keepdims=True)
        return m, _dot(jnp.exp(s - m).astype(BF16), v1)

    def fast_block(kb, j, size, straddles):
        ks, k, v1 = kv(kb, j, size)
        x = _dot_nt(qq[j], k) - refs[j]
        if straddles:
            before = (ks + lax.broadcasted_iota(jnp.int32, (1, size), 1)) < qi * tq
            x = jnp.where(before, x, NEG_BIG)
        return _dot(jnp.exp(x).astype(BF16), v1)

    def accumulate(blocks):
        for j in range(hp):
            total = acc_ref[j]
            for kb, size, straddles in blocks:
                total = total + fast_block(kb, j, size, straddles)
            acc_ref[j] = total

    def finish(j, acc):
        cols = slice(j * DIFF_V_DIM, (j + 1) * DIFF_V_DIM)
        inv = 1.0 / acc[:, DIFF_V_DIM:]
        o = acc[:, :DIFF_V_DIM] * inv
        o = o[:tq] - lam * o[tq:]
        bad = jnp.maximum(jnp.max(jnp.where(jnp.abs(o) < FINITE_MAX, 0.0, 1.0)),
                          jnp.max(jnp.where(inv > 0.0, 0.0, 1.0)))
        o = o * lax.rsqrt(jnp.mean(o * o, axis=1, keepdims=True) + RMS_EPS)
        o = o * subg_ref[...] * (1.0 - lam_init)
        o_ref[:, cols] = (o * g_ref[:, cols]).astype(o_ref.dtype)
        return bad

    refs = []
    for j in range(hp):
        m, acc = own_block(j)
        refs.append(m)
        acc_ref[j] = acc

    n_full = (qi * tq) // tk
    rest = qi * tq - n_full * tk

    def fast_pair(kp, carry):
        accumulate([(2 * kp, tk, False), (2 * kp + 1, tk, False)])
        return carry

    lax.fori_loop(0, n_full // 2, fast_pair, 0)

    @pl.when(n_full % 2 == 1)
    def _():
        accumulate([(n_full - 1, tk, False)])

    @pl.when(jnp.logical_and(rest > 0, rest <= tk // 2))
    def _():
        accumulate([(2 * n_full, tk // 2, True)])

    @pl.when(rest > tk // 2)
    def _():
        accumulate([(n_full, tk, True)])

    overflow = jnp.float32(0.0)
    for j in range(hp):
        overflow = jnp.maximum(overflow, finish(j, acc_ref[j]))

    @pl.when(overflow > 0.0)
    def _():
        def exact_step(kb, carry):
            return tuple(exact_block(kb, j, *carry[j]) for j in range(hp))
        carry = lax.fori_loop(0, qi, exact_step, tuple(own_block(j) for j in range(hp)))
        for j in range(hp):
            finish(j, carry[j][1])


def _attention(qkv, gates, lam_params, subln_g, lam_init):
    s = qkv.shape[0]
    heads = qkv.shape[1] // (3 * DIFF_V_DIM)
    tq, tk, hp = ATTN_TQ, ATTN_TK, ATTN_HP
    groups = heads // hp
    bw = hp * DIFF_V_DIM
    kern = functools.partial(_attn_kernel, tq=tq, tk=tk, hp=hp, lam_init=lam_init)
    return pl.pallas_call(
        kern,
        grid=(groups, s // tq),
        in_specs=[
            pl.BlockSpec((4, DIFF_HEAD_DIM), lambda h, i: (0, 0)),
            pl.BlockSpec((1, DIFF_V_DIM), lambda h, i: (0, 0)),
            pl.BlockSpec((tq, bw), lambda h, i: (i, h)),
            pl.BlockSpec((s, bw), lambda h, i: (0, groups + h)),
            pl.BlockSpec((s, bw), lambda h, i: (0, 2 * groups + h)),
            pl.BlockSpec((tq, bw), lambda h, i: (i, h)),
        ],
        out_specs=pl.BlockSpec((tq, bw), lambda h, i: (i, h)),
        out_shape=jax.ShapeDtypeStruct((s, heads * DIFF_V_DIM), BF16),
        scratch_shapes=[pltpu.VMEM((hp, 2 * tq, 2 * DIFF_V_DIM), F32)],
        compiler_params=pltpu.CompilerParams(
            dimension_semantics=("parallel", "parallel"), vmem_limit_bytes=VMEM_LIMIT),
        name="diff_attention",
    )(lam_params, subln_g, qkv, qkv, qkv, gates)


def _ssd_kernel(zs_ref, xs_ref, bc_ref, dt_ref, dtb_ref, alog_ref, dskip_ref, ng_ref, y_ref,
                state_ref, *, rows):
    c = pl.program_id(0)
    width = xs_ref.shape[1]
    gw = width // SSM_GROUPS
    heads_per_pair = LANES // SSM_HEAD_DIM

    @pl.when(c == 0)
    def _():
        state_ref[...] = jnp.zeros_like(state_ref)

    xs = xs_ref[...]
    bcv = bc_ref[...]

    dtc = _softplus(dt_ref[...] + dtb_ref[...])
    adt = -jnp.exp(alog_ref[...]) * dtc
    ri = lax.broadcasted_iota(jnp.int32, (rows, rows), 0)
    ci = lax.broadcasted_iota(jnp.int32, (rows, rows), 1)
    causal = ci <= ri
    tri = jnp.where(causal, 1.0, 0.0).astype(BF16)
    csc = sum(_dot(tri, part) for part in _split3(adt))
    cs_t = csc.T

    er = lax.broadcasted_iota(jnp.int32, (LANES, width), 0)
    ec = lax.broadcasted_iota(jnp.int32, (LANES, width), 1)
    expand = jnp.where((ec >> HEAD_SHIFT) == er, 1.0, 0.0).astype(BF16)
    dt_e = sum(_dot(part, expand) for part in _split3(dtc))
    cs_e = sum(_dot(part, expand) for part in _split3(csc))
    cs_last = cs_e[rows - 1:rows, :]

    xdt = xs * dt_e
    xdt_b = xdt.astype(BF16)
    xd_b = (xdt * jnp.exp(cs_last - cs_e)).astype(BF16)
    ecs = jnp.exp(cs_e)
    chunk_decay = jnp.exp(cs_last)
    lane = lax.broadcasted_iota(jnp.int32, (rows, LANES), 1)

    for g in range(SSM_GROUPS):
        gsl = slice(g * gw, (g + 1) * gw)
        b_f = bcv[:, g * SSM_STATE:(g + 1) * SSM_STATE]
        c_b = bcv[:, (SSM_GROUPS + g) * SSM_STATE:(SSM_GROUPS + g + 1) * SSM_STATE].astype(BF16)
        scores = _dot_nt(c_b, b_f.astype(BF16))
        st = state_ref[g]
        y_off = _dot(c_b, st.astype(BF16)) * ecs[:, gsl]
        state_ref[g] = st * chunk_decay[:, gsl] + _dot(b_f.T.astype(BF16), xd_b[:, gsl])

        y_diag = []
        for pair in range(gw // LANES):
            col0 = g * gw + pair * LANES
            xpair = xdt_b[:, col0:col0 + LANES]
            parts = []
            for hh in range(heads_per_pair):
                h = col0 // SSM_HEAD_DIM + hh
                seg = cs_e[:, h * SSM_HEAD_DIM:h * SSM_HEAD_DIM + 1] - cs_t[h:h + 1, :]
                decay = jnp.exp(jnp.where(causal, seg, NEG_BIG))
                parts.append(_dot((scores * decay).astype(BF16), xpair))
            y_diag.append(jnp.where(lane < SSM_HEAD_DIM, parts[0], parts[1]))
        y = jnp.concatenate(y_diag, axis=1) + y_off + dskip_ref[:, gsl] * xs[:, gsl]
        y = y * zs_ref[:, gsl]
        y = y * lax.rsqrt(jnp.mean(y * y, axis=1, keepdims=True) + RMS_EPS)
        y_ref[:, gsl] = (y * ng_ref[:, gsl]).astype(y_ref.dtype)


def _ssd(gates, xs, bc, dt_raw, dt_bias, a_log, d_skip, norm_g, width):
    s = xs.shape[0]
    rows = SSD_L
    bcw = 2 * SSM_GROUPS * SSM_STATE
    heads = width // SSM_HEAD_DIM

    def pad_heads(p):
        return jnp.pad(p.astype(F32), (0, LANES - heads)).reshape(1, LANES)

    def per_channel(p):
        return jnp.repeat(p.astype(F32), SSM_HEAD_DIM).reshape(1, width)

    z_blk = 1
    const = lambda c: (0, 0)
    kern = functools.partial(_ssd_kernel, rows=rows)
    return pl.pallas_call(
        kern,
        grid=(s // rows,),
        in_specs=[
            pl.BlockSpec((rows, width), lambda c: (c, z_blk)),
            pl.BlockSpec((rows, width), lambda c: (c, 0)),
            pl.BlockSpec((rows, bcw), lambda c: (c, 0)),
            pl.BlockSpec((rows, LANES), lambda c: (c, 0)),
            pl.BlockSpec((1, LANES), const),
            pl.BlockSpec((1, LANES), const),
            pl.BlockSpec((1, width), const),
            pl.BlockSpec((1, width), const),
        ],
        out_specs=pl.BlockSpec((rows, width), lambda c: (c, 0)),
        out_shape=jax.ShapeDtypeStruct((s, width), BF16),
        scratch_shapes=[pltpu.VMEM((SSM_GROUPS, SSM_STATE, width // SSM_GROUPS), F32)],
        compiler_params=pltpu.CompilerParams(
            dimension_semantics=("arbitrary",), vmem_limit_bytes=VMEM_LIMIT),
        name="ssd_scan",
    )(gates, xs, bc, dt_raw,
      pad_heads(dt_bias), pad_heads(a_log), per_channel(d_skip), norm_g.reshape(1, width))


def _pool_kernel(v_ref, x_ref, wgate_ref, wg_ref, bg_ref, sc_ref, y_ref, tail_ref, ext_ref,
                 wb_ref, pa_ref, pb_ref, *, rows):
    i = pl.program_id(0)
    gc = wg_ref.shape[1]
    top = POOL_HALO + rows

    @pl.when(i == 0)
    def _():
        tail_ref[...] = jnp.zeros_like(tail_ref)
        wb_ref[...] = wg_ref[...].astype(BF16)
        pa_ref[0:SUBLANES, :] = jnp.zeros((SUBLANES, gc), F32)
        pb_ref[0:SUBLANES, :] = jnp.zeros((SUBLANES, gc), F32)

    ext_ref[0:POOL_HALO, :] = tail_ref[...]
    ext_ref[POOL_HALO:top, :] = v_ref[...]
    tail_ref[...] = v_ref[rows - POOL_HALO:rows, :]
    pos = (i * rows + lax.broadcasted_iota(jnp.int32, (rows, 1), 0) + 1).astype(F32)

    for gi, w in enumerate(POOL_WINDOWS):
        cols = slice(gi * gc, (gi + 1) * gc)
        v = v_ref[:, cols]
        src, span = ext_ref.at[:, cols], 1
        for dst in (pa_ref, pb_ref, pa_ref, pb_ref):
            if span == w:
                break
            dst[SUBLANES:top, :] = (src[SUBLANES:top, :]
                                    + src[SUBLANES - span:top - span, :])
            src, span = dst, 2 * span
        acc = src[POOL_HALO:top, :]
        pooled = acc * (1.0 / jnp.minimum(pos, float(w))) - v
        m = _dot(pooled.astype(BF16), wb_ref[gi]) + bg_ref[:, cols]
        gate = _silu(_dot(x_ref[...], wgate_ref[:, cols]))
        y_ref[:, cols] = (m * sc_ref[:, cols] * gate).astype(y_ref.dtype)


def _pool(v, x, w_gate, w_grp, layer, b_grp, scale):
    s, width = v.shape
    k = x.shape[1]
    rows = POOL_TM
    _, ng, gc, _ = w_grp.shape
    kern = functools.partial(_pool_kernel, rows=rows)
    return pl.pallas_call(
        kern,
        grid=(s // rows,),
        in_specs=[
            pl.BlockSpec((rows, width), lambda i: (i, 0)),
            pl.BlockSpec((rows, k), lambda i: (i, 0)),
            pl.BlockSpec((k, width), lambda i: (0, 0), pipeline_mode=pl.Buffered(1)),
            pl.BlockSpec((None, ng, gc, gc), lambda i: (layer, 0, 0, 0)),
            pl.BlockSpec((1, width), lambda i: (0, 0)),
            pl.BlockSpec((1, width), lambda i: (0, 0)),
        ],
        out_specs=pl.BlockSpec((rows, width), lambda i: (i, 0)),
        out_shape=jax.ShapeDtypeStruct((s, width), BF16),
        scratch_shapes=[pltpu.VMEM((POOL_HALO, width), F32),
                        pltpu.VMEM((POOL_HALO + rows, width), F32),
                        pltpu.VMEM((ng, gc, gc), BF16),
                        pltpu.VMEM((POOL_HALO + rows, gc), F32),
                        pltpu.VMEM((POOL_HALO + rows, gc), F32)],
        compiler_params=pltpu.CompilerParams(
            dimension_semantics=("arbitrary",), vmem_limit_bytes=VMEM_LIMIT),
        name="pool_mixer",
    )(v, x, w_gate, w_grp, b_grp.reshape(1, width), scale.reshape(1, width))


def _out_ln_kernel(ya_ref, yb_ref, w_ref, x_ref, g_ref, b_ref, xo_ref, xob_ref):
    half = ya_ref.shape[1]
    y = _dot(ya_ref[...], w_ref[:half, :]) + _dot(yb_ref[...], w_ref[half:, :])
    h = DEEPNORM_ALPHA * x_ref[...] + y
    mu = jnp.mean(h, axis=1, keepdims=True)
    d = h - mu
    var = jnp.mean(d * d, axis=1, keepdims=True)
    out = d * lax.rsqrt(var + LN_EPS) * g_ref[...] + b_ref[...]
    xo_ref[...] = out
    xob_ref[...] = out.astype(BF16)


def _out_ln(ya, yb, ya_blk, yb_blk, w_out, x, ln_g, ln_b):
    s, d = x.shape
    half = w_out.shape[0] // 2
    tm = OUT_TM
    return pl.pallas_call(
        _out_ln_kernel,
        grid=(s // tm,),
        in_specs=[
            pl.BlockSpec((tm, half), lambda i: (i, ya_blk)),
            pl.BlockSpec((tm, half), lambda i: (i, yb_blk)),
            pl.BlockSpec((2 * half, d), lambda i: (0, 0), pipeline_mode=pl.Buffered(1)),
            pl.BlockSpec((tm, d), lambda i: (i, 0)),
            pl.BlockSpec((1, d), lambda i: (0, 0)),
            pl.BlockSpec((1, d), lambda i: (0, 0)),
        ],
        out_specs=[pl.BlockSpec((tm, d), lambda i: (i, 0)),
                   pl.BlockSpec((tm, d), lambda i: (i, 0))],
        out_shape=[jax.ShapeDtypeStruct((s, d), F32), jax.ShapeDtypeStruct((s, d), BF16)],
        compiler_params=pltpu.CompilerParams(
            dimension_semantics=("parallel",), vmem_limit_bytes=VMEM_LIMIT),
        name="out_proj_layernorm",
    )(ya, yb, w_out.astype(BF16), x, ln_g.reshape(1, d), ln_b.reshape(1, d))


def _even_layer(x, xb, i, w_in, conv_w, conv_b, dt_bias, a_log, d_skip, ssm_norm_g,
                lq1, lk1, lq2, lk2, subln_g, w_out, ln_g, ln_b, lam_init):
    d = x.shape[1]
    att_w = d // 2
    ssm_w = d // 2
    heads = ssm_w // SSM_HEAD_DIM
    n_qkv = 3 * att_w
    n_gates = att_w + ssm_w
    n_bc = 2 * SSM_GROUPS * SSM_STATE
    c_x = n_qkv + n_gates
    c_bc = c_x + ssm_w
    conv_x = (conv_w[:, :ssm_w], conv_b[:ssm_w].reshape(1, ssm_w))
    conv_bc = (conv_w[:, ssm_w:], conv_b[ssm_w:].reshape(1, n_bc))
    qkv = _matmul(xb, w_in, i, 0, n_qkv, MM_TN, BF16, "even_in_qkv", True)
    gates = _matmul(xb, w_in, i, n_qkv, n_gates, MM_TN, F32, "even_in_gates", True, "silu")
    w_dt = jnp.pad(w_in[i, c_bc + n_bc:, :], ((0, LANES - heads), (0, 0)))
    xs, dt_raw = _xconv_dt(xb, w_in, i, c_x, ssm_w, w_dt, conv_x)
    bc = _matmul(xb, w_in, i, c_bc, n_bc, n_bc, F32, "even_in_bc", True, "conv_silu", conv_bc)
    lam_params = jnp.stack([lq1, lk1, lq2, lk2]).astype(F32)
    y_att = _attention(qkv, gates, lam_params, subln_g.reshape(1, DIFF_V_DIM), lam_init)
    y_ssm = _ssd(gates, xs, bc, dt_raw, dt_bias, a_log, d_skip, ssm_norm_g, ssm_w)
    return _out_ln(y_att, y_ssm, 0, 0, w_out, x, ln_g, ln_b)


def _odd_layer(x, xb, i, w_in, w_grp, b_grp, scale, w_out, ln_g, ln_b):
    width = w_in.shape[2] // 2
    v = _matmul(xb, w_in, i, 0, width, MM_TN, F32, "odd_in_v")
    y = _pool(v, xb, w_in[i, :, width:].astype(BF16), w_grp, i, b_grp, scale)
    return _out_ln(y, y, 0, 1, w_out, x, ln_g, ln_b)


def kernel(x, ev_w_in, ev_conv_w, ev_conv_b, ev_dt_bias, ev_a_log, ev_d_skip, ev_ssm_norm_g, ev_lambda_q1, ev_lambda_k1, ev_lambda_q2, ev_lambda_k2, ev_subln_g, ev_w_out, od_w_in, od_w_grp, od_b_grp, od_scale, od_w_out, ln_g, ln_b):
    bsz, s, d = x.shape
    ev_w_in = jnp.swapaxes(ev_w_in, 1, 2)
    outs = []
    for b in range(bsz):
        xf = x[b]
        xb = xf
        for l in range(DEPTH):
            i = l // 2
            if l % 2 == 0:
                lam_init = 0.8 - 0.6 * math.exp(-0.3 * l)
                xf, xb = _even_layer(
                    xf, xb, i, ev_w_in, ev_conv_w[i], ev_conv_b[i], ev_dt_bias[i], ev_a_log[i],
                    ev_d_skip[i], ev_ssm_norm_g[i], ev_lambda_q1[i], ev_lambda_k1[i],
                    ev_lambda_q2[i], ev_lambda_k2[i], ev_subln_g[i], ev_w_out[i],
                    ln_g[l], ln_b[l], lam_init)
            else:
                xf, xb = _odd_layer(xf, xb, i, od_w_in, od_w_grp, od_b_grp[i], od_scale[i],
                                    od_w_out[i], ln_g[l], ln_b[l])
        outs.append(xf)
    return jnp.stack(outs)
```

```python
import functools
import math

import jax
import jax.numpy as jnp
from jax import lax
from jax.experimental import pallas as pl
from jax.experimental.pallas import tpu as pltpu

F32 = jnp.float32
BF16 = jnp.bfloat16

DEPTH = 4
CHUNK = 64
CHUNK_SHIFT = CHUNK.bit_length() - 1
DIFF_HEAD_DIM = 64
DIFF_V_DIM = 2 * DIFF_HEAD_DIM
SSM_HEAD_DIM = 64
HEAD_SHIFT = SSM_HEAD_DIM.bit_length() - 1
SSM_GROUPS = 2
SSM_STATE = 128
CONV_WIDTH = 4
POOL_WINDOWS = (2, 4, 8, 16)
DEEPNORM_ALPHA = (2.0 * DEPTH) ** 0.25
LN_EPS = 1e-5
RMS_EPS = 1e-5

LANES = 128
SUBLANES = 8
NEG_BIG = -1e30
FINITE_MAX = 3.0e38
VMEM_LIMIT = 56 * 1024 * 1024

ATTN_TQ = 256
ATTN_TK = 1024
ATTN_HP = 4
SSD_L = 256
MM_TM = 1024
MM_TN = 1024
OUT_TM = 512
OUT_CHUNKS = 4
POOL_TM = 512
POOL_HALO = 24


def _silu(x):
    h = 0.5 * x
    return h + h * jnp.tanh(h)


def _softplus(x):
    return jnp.maximum(x, 0.0) + jnp.log(1.0 + jnp.exp(-jnp.abs(x)))


def _split3(a):
    hi = a.astype(BF16)
    r1 = a - hi.astype(F32)
    mid = r1.astype(BF16)
    lo = (r1 - mid.astype(F32)).astype(BF16)
    return hi, mid, lo


def _dot(a, b):
    return jnp.dot(a, b, preferred_element_type=F32)


def _dot_nt(a, b):
    return lax.dot_general(a, b, (((1,), (1,)), ((), ())), preferred_element_type=F32)


def _conv_silu(raw, tail_ref, ext_ref, w_ref, b_ref):
    rows = raw.shape[0]
    ext_ref[0:SUBLANES, :] = tail_ref[...]
    ext_ref[SUBLANES:SUBLANES + rows, :] = raw
    tail_ref[...] = raw[rows - SUBLANES:rows, :]
    acc = b_ref[...]
    for t in range(CONV_WIDTH):
        start = SUBLANES - (CONV_WIDTH - 1) + t
        acc = acc + w_ref[t:t + 1, :] * ext_ref[start:start + rows, :]
    return _silu(acc)


def _matmul_kernel(*refs, w_is_nk, epilogue):
    if epilogue == "conv_silu":
        x_ref, w_ref, cw_ref, cb_ref, o_ref, wb_ref, tail_ref, ext_ref = refs
    else:
        x_ref, w_ref, o_ref, wb_ref = refs

    @pl.when(pl.program_id(1) == 0)
    def _():
        wb_ref[...] = w_ref[...].astype(BF16)
        if epilogue == "conv_silu":
            tail_ref[...] = jnp.zeros_like(tail_ref)

    dot = _dot_nt if w_is_nk else _dot
    y = dot(x_ref[...].astype(BF16), wb_ref[...])
    if epilogue == "silu":
        y = _silu(y)
    elif epilogue == "conv_silu":
        y = _conv_silu(y, tail_ref, ext_ref, cw_ref, cb_ref)
    o_ref[...] = y.astype(o_ref.dtype)


def _matmul(x, w, layer, col0, n, tn, out_dtype, name, w_is_nk=False, epilogue=None, conv=None):
    m, k = x.shape
    tm = min(MM_TM, m)
    assert n % tn == 0 and col0 % tn == 0 and m % tm == 0
    cb = col0 // tn
    if w_is_nk:
        w_spec = pl.BlockSpec((None, tn, k), lambda j, i: (layer, cb + j, 0))
    else:
        w_spec = pl.BlockSpec((None, k, tn), lambda j, i: (layer, 0, cb + j))
    in_specs = [pl.BlockSpec((tm, k), lambda j, i: (i, 0)), w_spec]
    scratch = [pltpu.VMEM((tn, k) if w_is_nk else (k, tn), BF16)]
    operands = [x, w]
    if epilogue == "conv_silu":
        in_specs += [pl.BlockSpec((CONV_WIDTH, tn), lambda j, i: (0, j)),
                     pl.BlockSpec((1, tn), lambda j, i: (0, j))]
        scratch += [pltpu.VMEM((SUBLANES, tn), F32), pltpu.VMEM((SUBLANES + tm, tn), F32)]
        operands += list(conv)
    return pl.pallas_call(
        functools.partial(_matmul_kernel, w_is_nk=w_is_nk, epilogue=epilogue),
        grid=(n // tn, m // tm),
        in_specs=in_specs,
        out_specs=pl.BlockSpec((tm, tn), lambda j, i: (i, j)),
        out_shape=jax.ShapeDtypeStruct((m, n), out_dtype),
        scratch_shapes=scratch,
        compiler_params=pltpu.CompilerParams(
            dimension_semantics=("arbitrary", "arbitrary"), vmem_limit_bytes=VMEM_LIMIT),
        name=name,
    )(*operands)


def _xconv_dt_kernel(x_ref, w_ref, wdt_ref, cw_ref, cb_ref, o_ref, dt_ref,
                     wb_ref, wdtb_ref, tail_ref, ext_ref):
    @pl.when(pl.program_id(0) == 0)
    def _():
        wb_ref[...] = w_ref[...].astype(BF16)
        wdtb_ref[...] = wdt_ref[...].astype(BF16)
        tail_ref[...] = jnp.zeros_like(tail_ref)

    x = x_ref[...].astype(BF16)
    o_ref[...] = _conv_silu(_dot_nt(x, wb_ref[...]), tail_ref, ext_ref, cw_ref, cb_ref)
    dt_ref[...] = _dot_nt(x, wdtb_ref[...])


def _xconv_dt(x, w, layer, row0, n, w_dt, conv):
    m, k = x.shape
    tm = min(MM_TM, m)
    assert row0 % n == 0 and m % tm == 0
    const = lambda i: (0, 0)
    return pl.pallas_call(
        _xconv_dt_kernel,
        grid=(m // tm,),
        in_specs=[pl.BlockSpec((tm, k), lambda i: (i, 0)),
                  pl.BlockSpec((None, n, k), lambda i: (layer, row0 // n, 0)),
                  pl.BlockSpec((LANES, k), const),
                  pl.BlockSpec((CONV_WIDTH, n), const),
                  pl.BlockSpec((1, n), const)],
        out_specs=[pl.BlockSpec((tm, n), lambda i: (i, 0)),
                   pl.BlockSpec((tm, LANES), lambda i: (i, 0))],
        out_shape=[jax.ShapeDtypeStruct((m, n), F32), jax.ShapeDtypeStruct((m, LANES), F32)],
        scratch_shapes=[pltpu.VMEM((n, k), BF16), pltpu.VMEM((LANES, k), BF16),
                        pltpu.VMEM((SUBLANES, n), F32), pltpu.VMEM((SUBLANES + tm, n), F32)],
        compiler_params=pltpu.CompilerParams(
            dimension_semantics=("arbitrary",), vmem_limit_bytes=VMEM_LIMIT),
        name="even_in_x_dt",
    )(x, w, w_dt, *conv)


def _attn_kernel(lam_ref, subg_ref, q_ref, k_ref, v_ref, g_ref, o_ref, acc_ref,
                 *, tq, tk, hp, lam_init):
    qi = pl.program_id(1)
    lp = lam_ref[...]
    lam = (jnp.exp(jnp.sum(lp[0:1] * lp[1:2], axis=1, keepdims=True))
           - jnp.exp(jnp.sum(lp[2:3] * lp[3:4], axis=1, keepdims=True)) + lam_init)

    lane = lax.broadcasted_iota(jnp.int32, (tq, DIFF_V_DIM), 1)
    row = lax.broadcasted_iota(jnp.int32, (2 * tq, 1), 0)
    q_chunk = (qi * tq + jnp.where(row >= tq, row - tq, row)) >> CHUNK_SHIFT

    def stacked_q(j):
        q = q_ref[:, j * DIFF_V_DIM:(j + 1) * DIFF_V_DIM] * (DIFF_HEAD_DIM ** -0.5)
        zero = jnp.zeros_like(q)
        return jnp.concatenate([jnp.where(lane < DIFF_HEAD_DIM, q, zero),
                                jnp.where(lane >= DIFF_HEAD_DIM, q, zero)], axis=0)

    qq = [stacked_q(j) for j in range(hp)]

    def kv(kb, j, size):
        ks = pl.multiple_of(kb * size, size)
        cols = slice(j * DIFF_V_DIM, (j + 1) * DIFF_V_DIM)
        k = k_ref[pl.ds(ks, size), cols]
        v1 = jnp.concatenate([v_ref[pl.ds(ks, size), cols],
                              jnp.ones((size, DIFF_V_DIM), BF16)], axis=1)
        return ks, k, v1

    def exact_block(kb, j, m, acc):
        _, k, v1 = kv(kb, j, tq)
        s = _dot_nt(qq[j], k)
        m_new = jnp.maximum(m, jnp.max(s, axis=1, keepdims=True))
        p = jnp.exp(s - m_new).astype(BF16)
        return m_new, jnp.exp(m - m_new) * acc + _dot(p, v1)

    def own_block(j):
        ks, k, v1 = kv(qi, j, tq)
        k_chunk = (ks + lax.broadcasted_iota(jnp.int32, (1, tq), 1)) >> CHUNK_SHIFT
        s = jnp.where(k_chunk <= q_chunk, _dot_nt(qq[j], k), NEG_BIG)
        m = jnp.max(s, axis=1, keepdims=True)
        return m, _dot(jnp.exp(s - m).astype(BF16), v1)

    def fast_block(kb, j, size, straddles):
        ks, k, v1 = kv(kb, j, size)
        x = _dot_nt(qq[j], k) - refs[j]
        if straddles:
            before = (ks + lax.broadcasted_iota(jnp.int32, (1, size), 1)) < qi * tq
            x = jnp.where(before, x, NEG_BIG)
        return _dot(jnp.exp(x).astype(BF16), v1)

    def accumulate(blocks):
        for j in range(hp):
            total = acc_ref[j]
            for kb, size, straddles in blocks:
                total = total + fast_block(kb, j, size, straddles)
            acc_ref[j] = total

    def finish(j, acc):
        cols = slice(j * DIFF_V_DIM, (j + 1) * DIFF_V_DIM)
        inv = 1.0 / acc[:, DIFF_V_DIM:]
        o = acc[:, :DIFF_V_DIM] * inv
        o = o[:tq] - lam * o[tq:]
        bad = jnp.maximum(jnp.max(jnp.where(jnp.abs(o) < FINITE_MAX, 0.0, 1.0)),
                          jnp.max(jnp.where(inv > 0.0, 0.0, 1.0)))
        o = o * lax.rsqrt(jnp.mean(o * o, axis=1, keepdims=True) + RMS_EPS)
        o = o * subg_ref[...] * (1.0 - lam_init)
        o_ref[:, cols] = (o * g_ref[:, cols]).astype(o_ref.dtype)
        return bad

    refs = []
    for j in range(hp):
        m, acc = own_block(j)
        refs.append(m)
        acc_ref[j] = acc

    n_full = (qi * tq) // tk
    rest = qi * tq - n_full * tk

    def fast_pair(kp, carry):
        accumulate([(2 * kp, tk, False), (2 * kp + 1, tk, False)])
        return carry

    lax.fori_loop(0, n_full // 2, fast_pair, 0)

    @pl.when(n_full % 2 == 1)
    def _():
        accumulate([(n_full - 1, tk, False)])

    @pl.when(jnp.logical_and(rest > 0, rest <= tk // 2))
    def _():
        accumulate([(2 * n_full, tk // 2, True)])

    @pl.when(rest > tk // 2)
    def _():
        accumulate([(n_full, tk, True)])

    overflow = jnp.float32(0.0)
    for j in range(hp):
        overflow = jnp.maximum(overflow, finish(j, acc_ref[j]))

    @pl.when(overflow > 0.0)
    def _():
        def exact_step(kb, carry):
            return tuple(exact_block(kb, j, *carry[j]) for j in range(hp))
        carry = lax.fori_loop(0, qi, exact_step, tuple(own_block(j) for j in range(hp)))
        for j in range(hp):
            finish(j, carry[j][1])


def _attention(qkv, gates, lam_params, subln_g, lam_init):
    s = qkv.shape[0]
    heads = qkv.shape[1] // (3 * DIFF_V_DIM)
    tq, tk, hp = ATTN_TQ, ATTN_TK, ATTN_HP
    groups = heads // hp
    bw = hp * DIFF_V_DIM
    kern = functools.partial(_attn_kernel, tq=tq, tk=tk, hp=hp, lam_init=lam_init)
    return pl.pallas_call(
        kern,
        grid=(groups, s // tq),
        in_specs=[
            pl.BlockSpec((4, DIFF_HEAD_DIM), lambda h, i: (0, 0)),
            pl.BlockSpec((1, DIFF_V_DIM), lambda h, i: (0, 0)),
            pl.BlockSpec((tq, bw), lambda h, i: (i, h)),
            pl.BlockSpec((s, bw), lambda h, i: (0, groups + h)),
            pl.BlockSpec((s, bw), lambda h, i: (0, 2 * groups + h)),
            pl.BlockSpec((tq, bw), lambda h, i: (i, h)),
        ],
        out_specs=pl.BlockSpec((tq, bw), lambda h, i: (i, h)),
        out_shape=jax.ShapeDtypeStruct((s, heads * DIFF_V_DIM), BF16),
        scratch_shapes=[pltpu.VMEM((hp, 2 * tq, 2 * DIFF_V_DIM), F32)],
        compiler_params=pltpu.CompilerParams(
            dimension_semantics=("parallel", "parallel"), vmem_limit_bytes=VMEM_LIMIT),
        name="diff_attention",
    )(lam_params, subln_g, qkv, qkv, qkv, gates)


def _ssd_kernel(zs_ref, xs_ref, bc_ref, dt_ref, dtb_ref, alog_ref, dskip_ref, ng_ref, y_ref,
                state_ref, *, rows):
    c = pl.program_id(0)
    width = xs_ref.shape[1]
    gw = width // SSM_GROUPS
    heads_per_pair = LANES // SSM_HEAD_DIM

    @pl.when(c == 0)
    def _():
        state_ref[...] = jnp.zeros_like(state_ref)

    xs = xs_ref[...]
    bcv = bc_ref[...]

    dtc = _softplus(dt_ref[...] + dtb_ref[...])
    adt = -jnp.exp(alog_ref[...]) * dtc
    ri = lax.broadcasted_iota(jnp.int32, (rows, rows), 0)
    ci = lax.broadcasted_iota(jnp.int32, (rows, rows), 1)
    causal = ci <= ri
    tri = jnp.where(causal, 1.0, 0.0).astype(BF16)
    csc = sum(_dot(tri, part) for part in _split3(adt))
    cs_t = csc.T

    er = lax.broadcasted_iota(jnp.int32, (LANES, width), 0)
    ec = lax.broadcasted_iota(jnp.int32, (LANES, width), 1)
    expand = jnp.where((ec >> HEAD_SHIFT) == er, 1.0, 0.0).astype(BF16)
    dt_e = sum(_dot(part, expand) for part in _split3(dtc))
    cs_e = sum(_dot(part, expand) for part in _split3(csc))
    cs_last = cs_e[rows - 1:rows, :]

    xdt = xs * dt_e
    xdt_b = xdt.astype(BF16)
    xd_b = (xdt * jnp.exp(cs_last - cs_e)).astype(BF16)
    ecs = jnp.exp(cs_e)
    chunk_decay = jnp.exp(cs_last)
    lane = lax.broadcasted_iota(jnp.int32, (rows, LANES), 1)

    for g in range(SSM_GROUPS):
        gsl = slice(g * gw, (g + 1) * gw)
        b_f = bcv[:, g * SSM_STATE:(g + 1) * SSM_STATE]
        c_b = bcv[:, (SSM_GROUPS + g) * SSM_STATE:(SSM_GROUPS + g + 1) * SSM_STATE].astype(BF16)
        scores = _dot_nt(c_b, b_f.astype(BF16))
        st = state_ref[g]
        y_off = _dot(c_b, st.astype(BF16)) * ecs[:, gsl]
        state_ref[g] = st * chunk_decay[:, gsl] + _dot(b_f.T.astype(BF16), xd_b[:, gsl])

        y_diag = []
        for pair in range(gw // LANES):
            col0 = g * gw + pair * LANES
            xpair = xdt_b[:, col0:col0 + LANES]
            parts = []
            for hh in range(heads_per_pair):
                h = col0 // SSM_HEAD_DIM + hh
                seg = cs_e[:, h * SSM_HEAD_DIM:h * SSM_HEAD_DIM + 1] - cs_t[h:h + 1, :]
                decay = jnp.exp(jnp.where(causal, seg, NEG_BIG))
                parts.append(_dot((scores * decay).astype(BF16), xpair))
            y_diag.append(jnp.where(lane < SSM_HEAD_DIM, parts[0], parts[1]))
        y = jnp.concatenate(y_diag, axis=1) + y_off + dskip_ref[:, gsl] * xs[:, gsl]
        y = y * zs_ref[:, gsl]
        y = y * lax.rsqrt(jnp.mean(y * y, axis=1, keepdims=True) + RMS_EPS)
        y_ref[:, gsl] = (y * ng_ref[:, gsl]).astype(y_ref.dtype)


def _ssd(gates, xs, bc, dt_raw, dt_bias, a_log, d_skip, norm_g, width):
    s = xs.shape[0]
    rows = SSD_L
    bcw = 2 * SSM_GROUPS * SSM_STATE
    heads = width // SSM_HEAD_DIM

    def pad_heads(p):
        return jnp.pad(p.astype(F32), (0, LANES - heads)).reshape(1, LANES)

    def per_channel(p):
        return jnp.repeat(p.astype(F32), SSM_HEAD_DIM).reshape(1, width)

    const = lambda c: (0, 0)
    kern = functools.partial(_ssd_kernel, rows=rows)
    return pl.pallas_call(
        kern,
        grid=(s // rows,),
        in_specs=[
            pl.BlockSpec((rows, width), lambda c: (c, 1)),
            pl.BlockSpec((rows, width), lambda c: (c, 0)),
            pl.BlockSpec((rows, bcw), lambda c: (c, 0)),
            pl.BlockSpec((rows, LANES), lambda c: (c, 0)),
            pl.BlockSpec((1, LANES), const),
            pl.BlockSpec((1, LANES), const),
            pl.BlockSpec((1, width), const),
            pl.BlockSpec((1, width), const),
        ],
        out_specs=pl.BlockSpec((rows, width), lambda c: (c, 0)),
        out_shape=jax.ShapeDtypeStruct((s, width), BF16),
        scratch_shapes=[pltpu.VMEM((SSM_GROUPS, SSM_STATE, width // SSM_GROUPS), F32)],
        compiler_params=pltpu.CompilerParams(
            dimension_semantics=("arbitrary",), vmem_limit_bytes=VMEM_LIMIT),
        name="ssd_scan",
    )(gates, xs, bc, dt_raw,
      pad_heads(dt_bias), pad_heads(a_log), per_channel(d_skip), norm_g.reshape(1, width))


def _pool_kernel(v_ref, x_ref, wgate_ref, wg_ref, bg_ref, sc_ref, y_ref, tail_ref, ext_ref,
                 wb_ref, pa_ref, pb_ref, wgb_ref, *, rows):
    i = pl.program_id(0)
    gc = wg_ref.shape[1]
    top = POOL_HALO + rows

    @pl.when(i == 0)
    def _():
        tail_ref[...] = jnp.zeros_like(tail_ref)
        wb_ref[...] = wg_ref[...].astype(BF16)
        wgb_ref[...] = wgate_ref[...].astype(BF16)
        pa_ref[0:SUBLANES, :] = jnp.zeros((SUBLANES, gc), F32)
        pb_ref[0:SUBLANES, :] = jnp.zeros((SUBLANES, gc), F32)

    ext_ref[0:POOL_HALO, :] = tail_ref[...]
    ext_ref[POOL_HALO:top, :] = v_ref[...]
    tail_ref[...] = v_ref[rows - POOL_HALO:rows, :]
    pos = (i * rows + lax.broadcasted_iota(jnp.int32, (rows, 1), 0) + 1).astype(F32)

    for gi, w in enumerate(POOL_WINDOWS):
        cols = slice(gi * gc, (gi + 1) * gc)
        v = v_ref[:, cols]
        src, span = ext_ref.at[:, cols], 1
        for dst in (pa_ref, pb_ref, pa_ref, pb_ref):
            if span == w:
                break
            dst[SUBLANES:top, :] = (src[SUBLANES:top, :]
                                    + src[SUBLANES - span:top - span, :])
            src, span = dst, 2 * span
        acc = src[POOL_HALO:top, :]
        pooled = acc * (1.0 / jnp.minimum(pos, float(w))) - v
        m = _dot(pooled.astype(BF16), wb_ref[gi]) + bg_ref[:, cols]
        gate = _silu(_dot(x_ref[...], wgb_ref[:, cols]))
        y_ref[:, cols] = (m * sc_ref[:, cols] * gate).astype(y_ref.dtype)


def _pool(v, x, w_in, w_grp, layer, b_grp, scale):
    s, width = v.shape
    k = x.shape[1]
    rows = POOL_TM
    _, ng, gc, _ = w_grp.shape
    kern = functools.partial(_pool_kernel, rows=rows)
    return pl.pallas_call(
        kern,
        grid=(s // rows,),
        in_specs=[
            pl.BlockSpec((rows, width), lambda i: (i, 0)),
            pl.BlockSpec((rows, k), lambda i: (i, 0)),
            pl.BlockSpec((None, k, width), lambda i: (layer, 0, 1),
                         pipeline_mode=pl.Buffered(1)),
            pl.BlockSpec((None, ng, gc, gc), lambda i: (layer, 0, 0, 0),
                         pipeline_mode=pl.Buffered(1)),
            pl.BlockSpec((1, width), lambda i: (0, 0)),
            pl.BlockSpec((1, width), lambda i: (0, 0)),
        ],
        out_specs=pl.BlockSpec((rows, width), lambda i: (i, 0)),
        out_shape=jax.ShapeDtypeStruct((s, width), BF16),
        scratch_shapes=[pltpu.VMEM((POOL_HALO, width), F32),
                        pltpu.VMEM((POOL_HALO + rows, width), F32),
                        pltpu.VMEM((ng, gc, gc), BF16),
                        pltpu.VMEM((POOL_HALO + rows, gc), F32),
                        pltpu.VMEM((POOL_HALO + rows, gc), F32),
                        pltpu.VMEM((k, width), BF16)],
        compiler_params=pltpu.CompilerParams(
            dimension_semantics=("arbitrary",), vmem_limit_bytes=VMEM_LIMIT),
        name="pool_mixer",
    )(v, x, w_in, w_grp, b_grp.reshape(1, width), scale.reshape(1, width))


def _out_ln_kernel(ya_ref, yb_ref, w_ref, x_ref, g_ref, b_ref, xo_ref, xob_ref, wb_ref):
    half = ya_ref.shape[1]

    @pl.when(pl.program_id(0) == 0)
    def _():
        wb_ref[...] = w_ref[...].astype(BF16)

    d_model = x_ref.shape[1]
    cw = d_model // OUT_CHUNKS
    chunks = [slice(c * cw, (c + 1) * cw) for c in range(OUT_CHUNKS)]
    hs = []
    for cs in chunks:
        y = _dot(ya_ref[...], wb_ref[:half, cs]) + _dot(yb_ref[...], wb_ref[half:, cs])
        hs.append(DEEPNORM_ALPHA * x_ref[:, cs] + y)
    mu = sum(jnp.sum(h, axis=1, keepdims=True) for h in hs) * (1.0 / d_model)
    ds = [h - mu for h in hs]
    var = sum(jnp.sum(d * d, axis=1, keepdims=True) for d in ds) * (1.0 / d_model)
    rstd = lax.rsqrt(var + LN_EPS)
    for cs, d in zip(chunks, ds):
        out = d * rstd * g_ref[:, cs] + b_ref[:, cs]
        xo_ref[:, cs] = out
        xob_ref[:, cs] = out.astype(BF16)


def _out_ln(ya, yb, ya_blk, yb_blk, w_out, layer, x, ln_g, ln_b):
    s, d = x.shape
    half = w_out.shape[1] // 2
    tm = OUT_TM
    return pl.pallas_call(
        _out_ln_kernel,
        grid=(s // tm,),
        in_specs=[
            pl.BlockSpec((tm, half), lambda i: (i, ya_blk)),
            pl.BlockSpec((tm, half), lambda i: (i, yb_blk)),
            pl.BlockSpec((None, 2 * half, d), lambda i: (layer, 0, 0),
                         pipeline_mode=pl.Buffered(1)),
            pl.BlockSpec((tm, d), lambda i: (i, 0)),
            pl.BlockSpec((1, d), lambda i: (0, 0)),
            pl.BlockSpec((1, d), lambda i: (0, 0)),
        ],
        out_specs=[pl.BlockSpec((tm, d), lambda i: (i, 0)),
                   pl.BlockSpec((tm, d), lambda i: (i, 0))],
        out_shape=[jax.ShapeDtypeStruct((s, d), F32), jax.ShapeDtypeStruct((s, d), BF16)],
        scratch_shapes=[pltpu.VMEM((2 * half, d), BF16)],
        compiler_params=pltpu.CompilerParams(
            dimension_semantics=("arbitrary",), vmem_limit_bytes=VMEM_LIMIT),
        name="out_proj_layernorm",
    )(ya, yb, w_out, x, ln_g.reshape(1, d), ln_b.reshape(1, d))


def _even_layer(x, xb, i, w_in, conv_w, conv_b, dt_bias, a_log, d_skip, ssm_norm_g,
                lq1, lk1, lq2, lk2, subln_g, w_out, ln_g, ln_b, lam_init):
    d = x.shape[1]
    att_w = d // 2
    ssm_w = d // 2
    heads = ssm_w // SSM_HEAD_DIM
    n_qkv = 3 * att_w
    n_gates = att_w + ssm_w
    n_bc = 2 * SSM_GROUPS * SSM_STATE
    c_x = n_qkv + n_gates
    c_bc = c_x + ssm_w
    conv_x = (conv_w[:, :ssm_w], conv_b[:ssm_w].reshape(1, ssm_w))
    conv_bc = (conv_w[:, ssm_w:], conv_b[ssm_w:].reshape(1, n_bc))
    qkv = _matmul(xb, w_in, i, 0, n_qkv, MM_TN, BF16, "even_in_qkv", True)
    gates = _matmul(xb, w_in, i, n_qkv, n_gates, MM_TN, F32, "even_in_gates", True, "silu")
    w_dt = jnp.pad(w_in[i, c_bc + n_bc:, :], ((0, LANES - heads), (0, 0)))
    xs, dt_raw = _xconv_dt(xb, w_in, i, c_x, ssm_w, w_dt, conv_x)
    bc = _matmul(xb, w_in, i, c_bc, n_bc, n_bc, F32, "even_in_bc", True, "conv_silu", conv_bc)
    lam_params = jnp.stack([lq1, lk1, lq2, lk2]).astype(F32)
    y_att = _attention(qkv, gates, lam_params, subln_g.reshape(1, DIFF_V_DIM), lam_init)
    y_ssm = _ssd(gates, xs, bc, dt_raw, dt_bias, a_log, d_skip, ssm_norm_g, ssm_w)
    return _out_ln(y_att, y_ssm, 0, 0, w_out, i, x, ln_g, ln_b)


def _odd_layer(x, xb, i, w_in, w_grp, b_grp, scale, w_out, ln_g, ln_b):
    width = w_in.shape[2] // 2
    v = _matmul(xb, w_in, i, 0, width, MM_TN, F32, "odd_in_v")
    y = _pool(v, xb, w_in, w_grp, i, b_grp, scale)
    return _out_ln(y, y, 0, 1, w_out, i, x, ln_g, ln_b)


def kernel(x, ev_w_in, ev_conv_w, ev_conv_b, ev_dt_bias, ev_a_log, ev_d_skip, ev_ssm_norm_g, ev_lambda_q1, ev_lambda_k1, ev_lambda_q2, ev_lambda_k2, ev_subln_g, ev_w_out, od_w_in, od_w_grp, od_b_grp, od_scale, od_w_out, ln_g, ln_b):
    bsz, s, d = x.shape
    ev_w_in = jnp.swapaxes(ev_w_in, 1, 2)
    outs = []
    for b in range(bsz):
        xf = x[b]
        xb = xf
        for l in range(DEPTH):
            i = l // 2
            if l % 2 == 0:
                lam_init = 0.8 - 0.6 * math.exp(-0.3 * l)
                xf, xb = _even_layer(
                    xf, xb, i, ev_w_in, ev_conv_w[i], ev_conv_b[i], ev_dt_bias[i], ev_a_log[i],
                    ev_d_skip[i], ev_ssm_norm_g[i], ev_lambda_q1[i], ev_lambda_k1[i],
                    ev_lambda_q2[i], ev_lambda_k2[i], ev_subln_g[i], ev_w_out,
                    ln_g[l], ln_b[l], lam_init)
            else:
                xf, xb = _odd_layer(xf, xb, i, od_w_in, od_w_grp, od_b_grp[i], od_scale[i],
                                    od_w_out, ln_g[l], ln_b[l])
        outs.append(xf)
    return jnp.stack(outs)
```

```python
import functools
import math

import jax
import jax.numpy as jnp
from jax import lax
from jax.experimental import pallas as pl
from jax.experimental.pallas import tpu as pltpu

F32 = jnp.float32
BF16 = jnp.bfloat16

DEPTH = 4
CHUNK = 64
CHUNK_SHIFT = CHUNK.bit_length() - 1
DIFF_HEAD_DIM = 64
DIFF_V_DIM = 2 * DIFF_HEAD_DIM
SSM_HEAD_DIM = 64
HEAD_SHIFT = SSM_HEAD_DIM.bit_length() - 1
SSM_GROUPS = 2
SSM_STATE = 128
CONV_WIDTH = 4
POOL_WINDOWS = (2, 4, 8, 16)
DEEPNORM_ALPHA = (2.0 * DEPTH) ** 0.25
LN_EPS = 1e-5
RMS_EPS = 1e-5

LANES = 128
SUBLANES = 8
NEG_BIG = -1e30
FINITE_MAX = 3.0e38
VMEM_LIMIT = 56 * 1024 * 1024

ATTN_TQ = 256
ATTN_TK = 1024
ATTN_HP = 4
SSD_L = 256
SSD_CHUNKS_PER_STEP = 4
MM_TM = 1024
MM_TN = 1024
OUT_TM = 512
OUT_CHUNKS = 4
POOL_TM = 512
POOL_HALO = 24


def _silu(x):
    h = 0.5 * x
    return h + h * jnp.tanh(h)


def _softplus(x):
    return jnp.maximum(x, 0.0) + jnp.log(1.0 + jnp.exp(-jnp.abs(x)))


def _split3(a):
    hi = a.astype(BF16)
    r1 = a - hi.astype(F32)
    mid = r1.astype(BF16)
    lo = (r1 - mid.astype(F32)).astype(BF16)
    return hi, mid, lo


def _dot(a, b):
    return jnp.dot(a, b, preferred_element_type=F32)


def _dot_nt(a, b):
    return lax.dot_general(a, b, (((1,), (1,)), ((), ())), preferred_element_type=F32)


def _conv_silu(raw, tail_ref, ext_ref, w_ref, b_ref):
    rows = raw.shape[0]
    ext_ref[0:SUBLANES, :] = tail_ref[...]
    ext_ref[SUBLANES:SUBLANES + rows, :] = raw
    tail_ref[...] = raw[rows - SUBLANES:rows, :]
    acc = b_ref[...]
    for t in range(CONV_WIDTH):
        start = SUBLANES - (CONV_WIDTH - 1) + t
        acc = acc + w_ref[t:t + 1, :] * ext_ref[start:start + rows, :]
    return _silu(acc)


def _matmul_kernel(*refs, w_is_nk, epilogue):
    if epilogue == "conv_silu":
        x_ref, w_ref, cw_ref, cb_ref, o_ref, wb_ref, tail_ref, ext_ref = refs
    else:
        x_ref, w_ref, o_ref, wb_ref = refs

    @pl.when(pl.program_id(1) == 0)
    def _():
        wb_ref[...] = w_ref[...].astype(BF16)
        if epilogue == "conv_silu":
            tail_ref[...] = jnp.zeros_like(tail_ref)

    dot = _dot_nt if w_is_nk else _dot
    y = dot(x_ref[...].astype(BF16), wb_ref[...])
    if epilogue == "silu":
        y = _silu(y)
    elif epilogue == "conv_silu":
        y = _conv_silu(y, tail_ref, ext_ref, cw_ref, cb_ref)
    o_ref[...] = y.astype(o_ref.dtype)


def _matmul(x, w, layer, col0, n, tn, out_dtype, name, w_is_nk=False, epilogue=None, conv=None):
    m, k = x.shape
    tm = min(MM_TM, m)
    assert n % tn == 0 and col0 % tn == 0 and m % tm == 0
    cb = col0 // tn
    if w_is_nk:
        w_spec = pl.BlockSpec((None, tn, k), lambda j, i: (layer, cb + j, 0))
    else:
        w_spec = pl.BlockSpec((None, k, tn), lambda j, i: (layer, 0, cb + j))
    in_specs = [pl.BlockSpec((tm, k), lambda j, i: (i, 0)), w_spec]
    scratch = [pltpu.VMEM((tn, k) if w_is_nk else (k, tn), BF16)]
    operands = [x, w]
    if epilogue == "conv_silu":
        in_specs += [pl.BlockSpec((CONV_WIDTH, tn), lambda j, i: (0, j)),
                     pl.BlockSpec((1, tn), lambda j, i: (0, j))]
        scratch += [pltpu.VMEM((SUBLANES, tn), F32), pltpu.VMEM((SUBLANES + tm, tn), F32)]
        operands += list(conv)
    return pl.pallas_call(
        functools.partial(_matmul_kernel, w_is_nk=w_is_nk, epilogue=epilogue),
        grid=(n // tn, m // tm),
        in_specs=in_specs,
        out_specs=pl.BlockSpec((tm, tn), lambda j, i: (i, j)),
        out_shape=jax.ShapeDtypeStruct((m, n), out_dtype),
        scratch_shapes=scratch,
        compiler_params=pltpu.CompilerParams(
            dimension_semantics=("arbitrary", "arbitrary"), vmem_limit_bytes=VMEM_LIMIT),
        name=name,
    )(*operands)


def _xconv_dt_kernel(x_ref, w_ref, wdt_ref, cw_ref, cb_ref, o_ref, dt_ref,
                     wb_ref, wdtb_ref, tail_ref, ext_ref):
    @pl.when(pl.program_id(0) == 0)
    def _():
        wb_ref[...] = w_ref[...].astype(BF16)
        wdtb_ref[...] = wdt_ref[...].astype(BF16)
        tail_ref[...] = jnp.zeros_like(tail_ref)

    x = x_ref[...].astype(BF16)
    o_ref[...] = _conv_silu(_dot_nt(x, wb_ref[...]), tail_ref, ext_ref, cw_ref, cb_ref)
    dt_ref[...] = _dot_nt(x, wdtb_ref[...])


def _xconv_dt(x, w, layer, row0, n, w_dt, conv):
    m, k = x.shape
    tm = min(MM_TM, m)
    assert row0 % n == 0 and m % tm == 0
    const = lambda i: (0, 0)
    return pl.pallas_call(
        _xconv_dt_kernel,
        grid=(m // tm,),
        in_specs=[pl.BlockSpec((tm, k), lambda i: (i, 0)),
                  pl.BlockSpec((None, n, k), lambda i: (layer, row0 // n, 0)),
                  pl.BlockSpec((LANES, k), const),
                  pl.BlockSpec((CONV_WIDTH, n), const),
                  pl.BlockSpec((1, n), const)],
        out_specs=[pl.BlockSpec((tm, n), lambda i: (i, 0)),
                   pl.BlockSpec((tm, LANES), lambda i: (i, 0))],
        out_shape=[jax.ShapeDtypeStruct((m, n), F32), jax.ShapeDtypeStruct((m, LANES), F32)],
        scratch_shapes=[pltpu.VMEM((n, k), BF16), pltpu.VMEM((LANES, k), BF16),
                        pltpu.VMEM((SUBLANES, n), F32), pltpu.VMEM((SUBLANES + tm, n), F32)],
        compiler_params=pltpu.CompilerParams(
            dimension_semantics=("arbitrary",), vmem_limit_bytes=VMEM_LIMIT),
        name="even_in_x_dt",
    )(x, w, w_dt, *conv)


def _attn_kernel(lam_ref, subg_ref, q_ref, k_ref, v_ref, g_ref, o_ref, acc_ref,
                 *, tq, tk, hp, lam_init):
    qi = pl.program_id(1)
    lp = lam_ref[...]
    lam = (jnp.exp(jnp.sum(lp[0:1] * lp[1:2], axis=1, keepdims=True))
           - jnp.exp(jnp.sum(lp[2:3] * lp[3:4], axis=1, keepdims=True)) + lam_init)

    lane = lax.broadcasted_iota(jnp.int32, (tq, DIFF_V_DIM), 1)
    row = lax.broadcasted_iota(jnp.int32, (2 * tq, 1), 0)
    q_chunk = (qi * tq + jnp.where(row >= tq, row - tq, row)) >> CHUNK_SHIFT

    def stacked_q(j):
        q = q_ref[:, j * DIFF_V_DIM:(j + 1) * DIFF_V_DIM] * (DIFF_HEAD_DIM ** -0.5)
        zero = jnp.zeros_like(q)
        return jnp.concatenate([jnp.where(lane < DIFF_HEAD_DIM, q, zero),
                                jnp.where(lane >= DIFF_HEAD_DIM, q, zero)], axis=0)

    qq = [stacked_q(j) for j in range(hp)]

    def kv(kb, j, size):
        ks = pl.multiple_of(kb * size, size)
        cols = slice(j * DIFF_V_DIM, (j + 1) * DIFF_V_DIM)
        k = k_ref[pl.ds(ks, size), cols]
        v1 = jnp.concatenate([v_ref[pl.ds(ks, size), cols],
                              jnp.ones((size, DIFF_V_DIM), BF16)], axis=1)
        return ks, k, v1

    def exact_block(kb, j, m, acc):
        _, k, v1 = kv(kb, j, tq)
        s = _dot_nt(qq[j], k)
        m_new = jnp.maximum(m, jnp.max(s, axis=1, keepdims=True))
        p = jnp.exp(s - m_new).astype(BF16)
        return m_new, jnp.exp(m - m_new) * acc + _dot(p, v1)

    def own_block(j):
        ks, k, v1 = kv(qi, j, tq)
        k_chunk = (ks + lax.broadcasted_iota(jnp.int32, (1, tq), 1)) >> CHUNK_SHIFT
        s = jnp.where(k_chunk <= q_chunk, _dot_nt(qq[j], k), NEG_BIG)
        m = jnp.max(s, axis=1, keepdims=True)
        return m, _dot(jnp.exp(s - m).astype(BF16), v1)

    def fast_block(kb, j, size, straddles):
        ks, k, v1 = kv(kb, j, size)
        x = _dot_nt(qq[j], k) - refs[j]
        if straddles:
            before = (ks + lax.broadcasted_iota(jnp.int32, (1, size), 1)) < qi * tq
            x = jnp.where(before, x, NEG_BIG)
        return _dot(jnp.exp(x).astype(BF16), v1)

    def accumulate(blocks):
        for j in range(hp):
            total = acc_ref[j]
            for kb, size, straddles in blocks:
                total = total + fast_block(kb, j, size, straddles)
            acc_ref[j] = total

    def finish(j, acc):
        cols = slice(j * DIFF_V_DIM, (j + 1) * DIFF_V_DIM)
        inv = 1.0 / acc[:, DIFF_V_DIM:]
        o = acc[:, :DIFF_V_DIM] * inv
        o = o[:tq] - lam * o[tq:]
        bad = jnp.maximum(jnp.max(jnp.where(jnp.abs(o) < FINITE_MAX, 0.0, 1.0)),
                          jnp.max(jnp.where(inv > 0.0, 0.0, 1.0)))
        o = o * lax.rsqrt(jnp.mean(o * o, axis=1, keepdims=True) + RMS_EPS)
        o = o * subg_ref[...] * (1.0 - lam_init)
        o_ref[:, cols] = (o * g_ref[:, cols]).astype(o_ref.dtype)
        return bad

    refs = []
    for j in range(hp):
        m, acc = own_block(j)
        refs.append(m)
        acc_ref[j] = acc

    n_full = (qi * tq) // tk
    rest = qi * tq - n_full * tk

    def fast_pair(kp, carry):
        accumulate([(2 * kp, tk, False), (2 * kp + 1, tk, False)])
        return carry

    lax.fori_loop(0, n_full // 2, fast_pair, 0)

    @pl.when(n_full % 2 == 1)
    def _():
        accumulate([(n_full - 1, tk, False)])

    @pl.when(jnp.logical_and(rest > 0, rest <= tk // 2))
    def _():
        accumulate([(2 * n_full, tk // 2, True)])

    @pl.when(rest > tk // 2)
    def _():
        accumulate([(n_full, tk, True)])

    overflow = jnp.float32(0.0)
    for j in range(hp):
        overflow = jnp.maximum(overflow, finish(j, acc_ref[j]))

    @pl.when(overflow > 0.0)
    def _():
        def exact_step(kb, carry):
            return tuple(exact_block(kb, j, *carry[j]) for j in range(hp))
        carry = lax.fori_loop(0, qi, exact_step, tuple(own_block(j) for j in range(hp)))
        for j in range(hp):
            finish(j, carry[j][1])


def _attention(qkv, gates, lam_params, subln_g, lam_init):
    s = qkv.shape[0]
    heads = qkv.shape[1] // (3 * DIFF_V_DIM)
    tq, tk, hp = ATTN_TQ, ATTN_TK, ATTN_HP
    groups = heads // hp
    bw = hp * DIFF_V_DIM
    kern = functools.partial(_attn_kernel, tq=tq, tk=tk, hp=hp, lam_init=lam_init)
    return pl.pallas_call(
        kern,
        grid=(groups, s // tq),
        in_specs=[
            pl.BlockSpec((4, DIFF_HEAD_DIM), lambda h, i: (0, 0)),
            pl.BlockSpec((1, DIFF_V_DIM), lambda h, i: (0, 0)),
            pl.BlockSpec((tq, bw), lambda h, i: (i, h)),
            pl.BlockSpec((s, bw), lambda h, i: (0, groups + h)),
            pl.BlockSpec((s, bw), lambda h, i: (0, 2 * groups + h)),
            pl.BlockSpec((tq, bw), lambda h, i: (i, h)),
        ],
        out_specs=pl.BlockSpec((tq, bw), lambda h, i: (i, h)),
        out_shape=jax.ShapeDtypeStruct((s, heads * DIFF_V_DIM), BF16),
        scratch_shapes=[pltpu.VMEM((hp, 2 * tq, 2 * DIFF_V_DIM), F32)],
        compiler_params=pltpu.CompilerParams(
            dimension_semantics=("parallel", "parallel"), vmem_limit_bytes=VMEM_LIMIT),
        name="diff_attention",
    )(lam_params, subln_g, qkv, qkv, qkv, gates)


def _ssd_kernel(zs_ref, xs_ref, bc_ref, dt_ref, dtb_ref, alog_ref, dskip_ref, ng_ref, y_ref,
                state_ref, *, rows, chunks):
    c = pl.program_id(0)
    width = xs_ref.shape[1]
    gw = width // SSM_GROUPS
    heads_per_pair = LANES // SSM_HEAD_DIM

    @pl.when(c == 0)
    def _():
        state_ref[...] = jnp.zeros_like(state_ref)

    ri = lax.broadcasted_iota(jnp.int32, (rows, rows), 0)
    ci = lax.broadcasted_iota(jnp.int32, (rows, rows), 1)
    causal = ci <= ri
    tri = jnp.where(causal, 1.0, 0.0).astype(BF16)
    er = lax.broadcasted_iota(jnp.int32, (LANES, width), 0)
    ec = lax.broadcasted_iota(jnp.int32, (LANES, width), 1)
    expand = jnp.where((ec >> HEAD_SHIFT) == er, 1.0, 0.0).astype(BF16)
    lane = lax.broadcasted_iota(jnp.int32, (rows, LANES), 1)

    def prepare(rs):
        xs = xs_ref[rs, :]
        dtc = _softplus(dt_ref[rs, :] + dtb_ref[...])
        adt = -jnp.exp(alog_ref[...]) * dtc
        csc = sum(_dot(tri, part) for part in _split3(adt))
        dt_e = sum(_dot(part, expand) for part in _split3(dtc))
        cs_e = sum(_dot(part, expand) for part in _split3(csc))
        cs_last = cs_e[rows - 1:rows, :]
        xdt = xs * dt_e
        return dict(xs=xs, cs_e=cs_e, cs_t=csc.T, xdt_b=xdt.astype(BF16),
                    xd_b=(xdt * jnp.exp(cs_last - cs_e)).astype(BF16), ecs=jnp.exp(cs_e),
                    chunk_decay=jnp.exp(cs_last))

    def scan(rs, p):
        bcv = bc_ref[rs, :]
        for g in range(SSM_GROUPS):
            gsl = slice(g * gw, (g + 1) * gw)
            b_f = bcv[:, g * SSM_STATE:(g + 1) * SSM_STATE]
            c_b = bcv[:, (SSM_GROUPS + g) * SSM_STATE:(SSM_GROUPS + g + 1) * SSM_STATE].astype(BF16)
            scores = _dot_nt(c_b, b_f.astype(BF16))
            st = state_ref[g]
            y_off = _dot(c_b, st.astype(BF16)) * p["ecs"][:, gsl]
            state_ref[g] = (st * p["chunk_decay"][:, gsl]
                            + _dot(b_f.T.astype(BF16), p["xd_b"][:, gsl]))

            y_diag = []
            for pair in range(gw // LANES):
                col0 = g * gw + pair * LANES
                xpair = p["xdt_b"][:, col0:col0 + LANES]
                parts = []
                for hh in range(heads_per_pair):
                    h = col0 // SSM_HEAD_DIM + hh
                    seg = (p["cs_e"][:, h * SSM_HEAD_DIM:h * SSM_HEAD_DIM + 1]
                           - p["cs_t"][h:h + 1, :])
                    decay = jnp.exp(jnp.where(causal, seg, NEG_BIG))
                    parts.append(_dot((scores * decay).astype(BF16), xpair))
                y_diag.append(jnp.where(lane < SSM_HEAD_DIM, parts[0], parts[1]))
            y = jnp.concatenate(y_diag, axis=1) + y_off + dskip_ref[:, gsl] * p["xs"][:, gsl]
            y = y * zs_ref[rs, gsl]
            y = y * lax.rsqrt(jnp.mean(y * y, axis=1, keepdims=True) + RMS_EPS)
            y_ref[rs, gsl] = (y * ng_ref[:, gsl]).astype(y_ref.dtype)

    spans = [slice(k * rows, (k + 1) * rows) for k in range(chunks)]
    prepared = [prepare(rs) for rs in spans]
    for rs, p in zip(spans, prepared):
        scan(rs, p)


def _ssd(gates, xs, bc, dt_raw, dt_bias, a_log, d_skip, norm_g, width):
    s = xs.shape[0]
    rows = SSD_L * SSD_CHUNKS_PER_STEP
    bcw = 2 * SSM_GROUPS * SSM_STATE
    heads = width // SSM_HEAD_DIM

    def pad_heads(p):
        return jnp.pad(p.astype(F32), (0, LANES - heads)).reshape(1, LANES)

    def per_channel(p):
        return jnp.repeat(p.astype(F32), SSM_HEAD_DIM).reshape(1, width)

    const = lambda c: (0, 0)
    kern = functools.partial(_ssd_kernel, rows=SSD_L, chunks=SSD_CHUNKS_PER_STEP)
    return pl.pallas_call(
        kern,
        grid=(s // rows,),
        in_specs=[
            pl.BlockSpec((rows, width), lambda c: (c, 1)),
            pl.BlockSpec((rows, width), lambda c: (c, 0)),
            pl.BlockSpec((rows, bcw), lambda c: (c, 0)),
            pl.BlockSpec((rows, LANES), lambda c: (c, 0)),
            pl.BlockSpec((1, LANES), const),
            pl.BlockSpec((1, LANES), const),
            pl.BlockSpec((1, width), const),
            pl.BlockSpec((1, width), const),
        ],
        out_specs=pl.BlockSpec((rows, width), lambda c: (c, 0)),
        out_shape=jax.ShapeDtypeStruct((s, width), BF16),
        scratch_shapes=[pltpu.VMEM((SSM_GROUPS, SSM_STATE, width // SSM_GROUPS), F32)],
        compiler_params=pltpu.CompilerParams(
            dimension_semantics=("arbitrary",), vmem_limit_bytes=VMEM_LIMIT),
        name="ssd_scan",
    )(gates, xs, bc, dt_raw,
      pad_heads(dt_bias), pad_heads(a_log), per_channel(d_skip), norm_g.reshape(1, width))


def _pool_kernel(v_ref, x_ref, wgate_ref, wg_ref, bg_ref, sc_ref, y_ref, tail_ref, ext_ref,
                 wb_ref, pa_ref, pb_ref, wgb_ref, *, rows):
    i = pl.program_id(0)
    gc = wg_ref.shape[1]
    top = POOL_HALO + rows

    @pl.when(i == 0)
    def _():
        tail_ref[...] = jnp.zeros_like(tail_ref)
        wb_ref[...] = wg_ref[...].astype(BF16)
        wgb_ref[...] = wgate_ref[...].astype(BF16)
        pa_ref[0:SUBLANES, :] = jnp.zeros((SUBLANES, gc), F32)
        pb_ref[0:SUBLANES, :] = jnp.zeros((SUBLANES, gc), F32)

    ext_ref[0:POOL_HALO, :] = tail_ref[...]
    ext_ref[POOL_HALO:top, :] = v_ref[...]
    tail_ref[...] = v_ref[rows - POOL_HALO:rows, :]
    pos = (i * rows + lax.broadcasted_iota(jnp.int32, (rows, 1), 0) + 1).astype(F32)

    for gi, w in enumerate(POOL_WINDOWS):
        cols = slice(gi * gc, (gi + 1) * gc)
        v = v_ref[:, cols]
        src, span = ext_ref.at[:, cols], 1
        for dst in (pa_ref, pb_ref, pa_ref, pb_ref):
            if span == w:
                break
            dst[SUBLANES:top, :] = (src[SUBLANES:top, :]
                                    + src[SUBLANES - span:top - span, :])
            src, span = dst, 2 * span
        acc = src[POOL_HALO:top, :]
        pooled = acc * (1.0 / jnp.minimum(pos, float(w))) - v
        m = _dot(pooled.astype(BF16), wb_ref[gi]) + bg_ref[:, cols]
        gate = _silu(_dot(x_ref[...], wgb_ref[:, cols]))
        y_ref[:, cols] = (m * sc_ref[:, cols] * gate).astype(y_ref.dtype)


def _pool(v, x, w_in, w_grp, layer, b_grp, scale):
    s, width = v.shape
    k = x.shape[1]
    rows = POOL_TM
    _, ng, gc, _ = w_grp.shape
    kern = functools.partial(_pool_kernel, rows=rows)
    return pl.pallas_call(
        kern,
        grid=(s // rows,),
        in_specs=[
            pl.BlockSpec((rows, width), lambda i: (i, 0)),
            pl.BlockSpec((rows, k), lambda i: (i, 0)),
            pl.BlockSpec((None, k, width), lambda i: (layer, 0, 1),
                         pipeline_mode=pl.Buffered(1)),
            pl.BlockSpec((None, ng, gc, gc), lambda i: (layer, 0, 0, 0),
                         pipeline_mode=pl.Buffered(1)),
            pl.BlockSpec((1, width), lambda i: (0, 0)),
            pl.BlockSpec((1, width), lambda i: (0, 0)),
        ],
        out_specs=pl.BlockSpec((rows, width), lambda i: (i, 0)),
        out_shape=jax.ShapeDtypeStruct((s, width), BF16),
        scratch_shapes=[pltpu.VMEM((POOL_HALO, width), F32),
                        pltpu.VMEM((POOL_HALO + rows, width), F32),
                        pltpu.VMEM((ng, gc, gc), BF16),
                        pltpu.VMEM((POOL_HALO + rows, gc), F32),
                        pltpu.VMEM((POOL_HALO + rows, gc), F32),
                        pltpu.VMEM((k, width), BF16)],
        compiler_params=pltpu.CompilerParams(
            dimension_semantics=("arbitrary",), vmem_limit_bytes=VMEM_LIMIT),
        name="pool_mixer",
    )(v, x, w_in, w_grp, b_grp.reshape(1, width), scale.reshape(1, width))


def _out_ln_kernel(ya_ref, yb_ref, w_ref, x_ref, g_ref, b_ref, xo_ref, xob_ref, wb_ref):
    half = ya_ref.shape[1]

    @pl.when(pl.program_id(0) == 0)
    def _():
        wb_ref[...] = w_ref[...].astype(BF16)

    d_model = x_ref.shape[1]
    cw = d_model // OUT_CHUNKS
    chunks = [slice(c * cw, (c + 1) * cw) for c in range(OUT_CHUNKS)]
    hs = []
    for cs in chunks:
        y = _dot(ya_ref[...], wb_ref[:half, cs]) + _dot(yb_ref[...], wb_ref[half:, cs])
        hs.append(DEEPNORM_ALPHA * x_ref[:, cs] + y)
    mu = sum(jnp.sum(h, axis=1, keepdims=True) for h in hs) * (1.0 / d_model)
    ds = [h - mu for h in hs]
    var = sum(jnp.sum(d * d, axis=1, keepdims=True) for d in ds) * (1.0 / d_model)
    rstd = lax.rsqrt(var + LN_EPS)
    for cs, d in zip(chunks, ds):
        out = d * rstd * g_ref[:, cs] + b_ref[:, cs]
        xo_ref[:, cs] = out
        xob_ref[:, cs] = out.astype(BF16)


def _out_ln(ya, yb, ya_blk, yb_blk, w_out, layer, x, ln_g, ln_b):
    s, d = x.shape
    half = w_out.shape[1] // 2
    tm = OUT_TM
    return pl.pallas_call(
        _out_ln_kernel,
        grid=(s // tm,),
        in_specs=[
            pl.BlockSpec((tm, half), lambda i: (i, ya_blk)),
            pl.BlockSpec((tm, half), lambda i: (i, yb_blk)),
            pl.BlockSpec((None, 2 * half, d), lambda i: (layer, 0, 0),
                         pipeline_mode=pl.Buffered(1)),
            pl.BlockSpec((tm, d), lambda i: (i, 0)),
            pl.BlockSpec((1, d), lambda i: (0, 0)),
            pl.BlockSpec((1, d), lambda i: (0, 0)),
        ],
        out_specs=[pl.BlockSpec((tm, d), lambda i: (i, 0)),
                   pl.BlockSpec((tm, d), lambda i: (i, 0))],
        out_shape=[jax.ShapeDtypeStruct((s, d), F32), jax.ShapeDtypeStruct((s, d), BF16)],
        scratch_shapes=[pltpu.VMEM((2 * half, d), BF16)],
        compiler_params=pltpu.CompilerParams(
            dimension_semantics=("arbitrary",), vmem_limit_bytes=VMEM_LIMIT),
        name="out_proj_layernorm",
    )(ya, yb, w_out, x, ln_g.reshape(1, d), ln_b.reshape(1, d))


def _even_layer(x, xb, i, w_in, conv_w, conv_b, dt_bias, a_log, d_skip, ssm_norm_g,
                lq1, lk1, lq2, lk2, subln_g, w_out, ln_g, ln_b, lam_init):
    d = x.shape[1]
    att_w = d // 2
    ssm_w = d // 2
    heads = ssm_w // SSM_HEAD_DIM
    n_qkv = 3 * att_w
    n_gates = att_w + ssm_w
    n_bc = 2 * SSM_GROUPS * SSM_STATE
    c_x = n_qkv + n_gates
    c_bc = c_x + ssm_w
    conv_x = (conv_w[:, :ssm_w], conv_b[:ssm_w].reshape(1, ssm_w))
    conv_bc = (conv_w[:, ssm_w:], conv_b[ssm_w:].reshape(1, n_bc))
    qkv = _matmul(xb, w_in, i, 0, n_qkv, MM_TN, BF16, "even_in_qkv", True)
    gates = _matmul(xb, w_in, i, n_qkv, n_gates, MM_TN, F32, "even_in_gates", True, "silu")
    w_dt = jnp.pad(w_in[i, c_bc + n_bc:, :], ((0, LANES - heads), (0, 0)))
    xs, dt_raw = _xconv_dt(xb, w_in, i, c_x, ssm_w, w_dt, conv_x)
    bc = _matmul(xb, w_in, i, c_bc, n_bc, n_bc, F32, "even_in_bc", True, "conv_silu", conv_bc)
    lam_params = jnp.stack([lq1, lk1, lq2, lk2]).astype(F32)
    y_att = _attention(qkv, gates, lam_params, subln_g.reshape(1, DIFF_V_DIM), lam_init)
    y_ssm = _ssd(gates, xs, bc, dt_raw, dt_bias, a_log, d_skip, ssm_norm_g, ssm_w)
    return _out_ln(y_att, y_ssm, 0, 0, w_out, i, x, ln_g, ln_b)


def _odd_layer(x, xb, i, w_in, w_grp, b_grp, scale, w_out, ln_g, ln_b):
    width = w_in.shape[2] // 2
    v = _matmul(xb, w_in, i, 0, width, MM_TN, F32, "odd_in_v")
    y = _pool(v, xb, w_in, w_grp, i, b_grp, scale)
    return _out_ln(y, y, 0, 1, w_out, i, x, ln_g, ln_b)


def kernel(x, ev_w_in, ev_conv_w, ev_conv_b, ev_dt_bias, ev_a_log, ev_d_skip, ev_ssm_norm_g, ev_lambda_q1, ev_lambda_k1, ev_lambda_q2, ev_lambda_k2, ev_subln_g, ev_w_out, od_w_in, od_w_grp, od_b_grp, od_scale, od_w_out, ln_g, ln_b):
    bsz, s, d = x.shape
    ev_w_in = jnp.swapaxes(ev_w_in, 1, 2)
    outs = []
    for b in range(bsz):
        xf = x[b]
        xb = xf
        for l in range(DEPTH):
            i = l // 2
            if l % 2 == 0:
                lam_init = 0.8 - 0.6 * math.exp(-0.3 * l)
                xf, xb = _even_layer(
                    xf, xb, i, ev_w_in, ev_conv_w[i], ev_conv_b[i], ev_dt_bias[i], ev_a_log[i],
                    ev_d_skip[i], ev_ssm_norm_g[i], ev_lambda_q1[i], ev_lambda_k1[i],
                    ev_lambda_q2[i], ev_lambda_k2[i], ev_subln_g[i], ev_w_out,
                    ln_g[l], ln_b[l], lam_init)
            else:
                xf, xb = _odd_layer(xf, xb, i, od_w_in, od_w_grp, od_b_grp[i], od_scale[i],
                                    od_w_out, ln_g[l], ln_b[l])
        outs.append(xf)
    return jnp.stack(outs)
```

```python
import functools
import math

import jax
import jax.numpy as jnp
from jax import lax
from jax.experimental import pallas as pl
from jax.experimental.pallas import tpu as pltpu

F32 = jnp.float32
BF16 = jnp.bfloat16

DEPTH = 4
CHUNK = 64
CHUNK_SHIFT = CHUNK.bit_length() - 1
DIFF_HEAD_DIM = 64
DIFF_V_DIM = 2 * DIFF_HEAD_DIM
SSM_HEAD_DIM = 64
HEAD_SHIFT = SSM_HEAD_DIM.bit_length() - 1
SSM_GROUPS = 2
SSM_STATE = 128
CONV_WIDTH = 4
POOL_WINDOWS = (2, 4, 8, 16)
DEEPNORM_ALPHA = (2.0 * DEPTH) ** 0.25
LN_EPS = 1e-5
RMS_EPS = 1e-5

LANES = 128
SUBLANES = 8
NEG_BIG = -1e30
FINITE_MAX = 3.0e38
VMEM_LIMIT = 56 * 1024 * 1024

ATTN_TQ = 256
ATTN_TK = 1024
ATTN_HP = 4
SSD_L = 128
SSD_CHUNKS_PER_STEP = 8
MM_TM = 1024
MM_TN = 1024
OUT_TM = 512
OUT_CHUNKS = 4
POOL_TM = 512
POOL_HALO = 24


def _silu(x):
    h = 0.5 * x
    return h + h * jnp.tanh(h)


def _softplus(x):
    return jnp.maximum(x, 0.0) + jnp.log(1.0 + jnp.exp(-jnp.abs(x)))


def _split3(a):
    hi = a.astype(BF16)
    r1 = a - hi.astype(F32)
    mid = r1.astype(BF16)
    lo = (r1 - mid.astype(F32)).astype(BF16)
    return hi, mid, lo


def _dot(a, b):
    return jnp.dot(a, b, preferred_element_type=F32)


def _dot_nt(a, b):
    return lax.dot_general(a, b, (((1,), (1,)), ((), ())), preferred_element_type=F32)


def _conv_silu(raw, tail_ref, ext_ref, w_ref, b_ref):
    rows = raw.shape[0]
    ext_ref[0:SUBLANES, :] = tail_ref[...]
    ext_ref[SUBLANES:SUBLANES + rows, :] = raw
    tail_ref[...] = raw[rows - SUBLANES:rows, :]
    acc = b_ref[...]
    for t in range(CONV_WIDTH):
        start = SUBLANES - (CONV_WIDTH - 1) + t
        acc = acc + w_ref[t:t + 1, :] * ext_ref[start:start + rows, :]
    return _silu(acc)


def _matmul_kernel(*refs, w_is_nk, epilogue):
    if epilogue == "conv_silu":
        x_ref, w_ref, cw_ref, cb_ref, o_ref, wb_ref, tail_ref, ext_ref = refs
    else:
        x_ref, w_ref, o_ref, wb_ref = refs

    @pl.when(pl.program_id(1) == 0)
    def _():
        wb_ref[...] = w_ref[...].astype(BF16)
        if epilogue == "conv_silu":
            tail_ref[...] = jnp.zeros_like(tail_ref)

    dot = _dot_nt if w_is_nk else _dot
    y = dot(x_ref[...].astype(BF16), wb_ref[...])
    if epilogue == "silu":
        y = _silu(y)
    elif epilogue == "conv_silu":
        y = _conv_silu(y, tail_ref, ext_ref, cw_ref, cb_ref)
    o_ref[...] = y.astype(o_ref.dtype)


def _matmul(x, w, layer, col0, n, tn, out_dtype, name, w_is_nk=False, epilogue=None, conv=None):
    m, k = x.shape
    tm = min(MM_TM, m)
    assert n % tn == 0 and col0 % tn == 0 and m % tm == 0
    cb = col0 // tn
    if w_is_nk:
        w_spec = pl.BlockSpec((None, tn, k), lambda j, i: (layer, cb + j, 0))
    else:
        w_spec = pl.BlockSpec((None, k, tn), lambda j, i: (layer, 0, cb + j))
    in_specs = [pl.BlockSpec((tm, k), lambda j, i: (i, 0)), w_spec]
    scratch = [pltpu.VMEM((tn, k) if w_is_nk else (k, tn), BF16)]
    operands = [x, w]
    if epilogue == "conv_silu":
        in_specs += [pl.BlockSpec((CONV_WIDTH, tn), lambda j, i: (0, j)),
                     pl.BlockSpec((1, tn), lambda j, i: (0, j))]
        scratch += [pltpu.VMEM((SUBLANES, tn), F32), pltpu.VMEM((SUBLANES + tm, tn), F32)]
        operands += list(conv)
    return pl.pallas_call(
        functools.partial(_matmul_kernel, w_is_nk=w_is_nk, epilogue=epilogue),
        grid=(n // tn, m // tm),
        in_specs=in_specs,
        out_specs=pl.BlockSpec((tm, tn), lambda j, i: (i, j)),
        out_shape=jax.ShapeDtypeStruct((m, n), out_dtype),
        scratch_shapes=scratch,
        compiler_params=pltpu.CompilerParams(
            dimension_semantics=("arbitrary", "arbitrary"), vmem_limit_bytes=VMEM_LIMIT),
        name=name,
    )(*operands)


def _xconv_dt_kernel(x_ref, w_ref, wdt_ref, cw_ref, cb_ref, o_ref, dt_ref,
                     wb_ref, wdtb_ref, tail_ref, ext_ref):
    @pl.when(pl.program_id(0) == 0)
    def _():
        wb_ref[...] = w_ref[...].astype(BF16)
        wdtb_ref[...] = wdt_ref[...].astype(BF16)
        tail_ref[...] = jnp.zeros_like(tail_ref)

    x = x_ref[...].astype(BF16)
    o_ref[...] = _conv_silu(_dot_nt(x, wb_ref[...]), tail_ref, ext_ref, cw_ref, cb_ref)
    dt_ref[...] = _dot_nt(x, wdtb_ref[...])


def _xconv_dt(x, w, layer, row0, n, w_dt, conv):
    m, k = x.shape
    tm = min(MM_TM, m)
    assert row0 % n == 0 and m % tm == 0
    const = lambda i: (0, 0)
    return pl.pallas_call(
        _xconv_dt_kernel,
        grid=(m // tm,),
        in_specs=[pl.BlockSpec((tm, k), lambda i: (i, 0)),
                  pl.BlockSpec((None, n, k), lambda i: (layer, row0 // n, 0)),
                  pl.BlockSpec((LANES, k), const),
                  pl.BlockSpec((CONV_WIDTH, n), const),
                  pl.BlockSpec((1, n), const)],
        out_specs=[pl.BlockSpec((tm, n), lambda i: (i, 0)),
                   pl.BlockSpec((tm, LANES), lambda i: (i, 0))],
        out_shape=[jax.ShapeDtypeStruct((m, n), F32), jax.ShapeDtypeStruct((m, LANES), F32)],
        scratch_shapes=[pltpu.VMEM((n, k), BF16), pltpu.VMEM((LANES, k), BF16),
                        pltpu.VMEM((SUBLANES, n), F32), pltpu.VMEM((SUBLANES + tm, n), F32)],
        compiler_params=pltpu.CompilerParams(
            dimension_semantics=("arbitrary",), vmem_limit_bytes=VMEM_LIMIT),
        name="even_in_x_dt",
    )(x, w, w_dt, *conv)


def _attn_kernel(lam_ref, subg_ref, q_ref, k_ref, v_ref, g_ref, o_ref, acc_ref,
                 *, tq, tk, hp, lam_init):
    qi = pl.program_id(1)
    lp = lam_ref[...]
    lam = (jnp.exp(jnp.sum(lp[0:1] * lp[1:2], axis=1, keepdims=True))
           - jnp.exp(jnp.sum(lp[2:3] * lp[3:4], axis=1, keepdims=True)) + lam_init)

    lane = lax.broadcasted_iota(jnp.int32, (tq, DIFF_V_DIM), 1)
    row = lax.broadcasted_iota(jnp.int32, (2 * tq, 1), 0)
    q_chunk = (qi * tq + jnp.where(row >= tq, row - tq, row)) >> CHUNK_SHIFT

    def stacked_q(j):
        q = q_ref[:, j * DIFF_V_DIM:(j + 1) * DIFF_V_DIM] * (DIFF_HEAD_DIM ** -0.5)
        zero = jnp.zeros_like(q)
        return jnp.concatenate([jnp.where(lane < DIFF_HEAD_DIM, q, zero),
                                jnp.where(lane >= DIFF_HEAD_DIM, q, zero)], axis=0)

    qq = [stacked_q(j) for j in range(hp)]

    def kv(kb, j, size):
        ks = pl.multiple_of(kb * size, size)
        cols = slice(j * DIFF_V_DIM, (j + 1) * DIFF_V_DIM)
        k = k_ref[pl.ds(ks, size), cols]
        v1 = jnp.concatenate([v_ref[pl.ds(ks, size), cols],
                              jnp.ones((size, DIFF_V_DIM), BF16)], axis=1)
        return ks, k, v1

    def exact_block(kb, j, m, acc):
        _, k, v1 = kv(kb, j, tq)
        s = _dot_nt(qq[j], k)
        m_new = jnp.maximum(m, jnp.max(s, axis=1, keepdims=True))
        p = jnp.exp(s - m_new).astype(BF16)
        return m_new, jnp.exp(m - m_new) * acc + _dot(p, v1)

    def own_block(j):
        ks, k, v1 = kv(qi, j, tq)
        k_chunk = (ks + lax.broadcasted_iota(jnp.int32, (1, tq), 1)) >> CHUNK_SHIFT
        s = jnp.where(k_chunk <= q_chunk, _dot_nt(qq[j], k), NEG_BIG)
        m = jnp.max(s, axis=1, keepdims=True)
        return m, _dot(jnp.exp(s - m).astype(BF16), v1)

    def fast_block(kb, j, size, straddles):
        ks, k, v1 = kv(kb, j, size)
        x = _dot_nt(qq[j], k) - refs[j]
        if straddles:
            before = (ks + lax.broadcasted_iota(jnp.int32, (1, size), 1)) < qi * tq
            x = jnp.where(before, x, NEG_BIG)
        return _dot(jnp.exp(x).astype(BF16), v1)

    def accumulate(blocks):
        for j in range(hp):
            total = acc_ref[j]
            for kb, size, straddles in blocks:
                total = total + fast_block(kb, j, size, straddles)
            acc_ref[j] = total

    def finish(j, acc):
        cols = slice(j * DIFF_V_DIM, (j + 1) * DIFF_V_DIM)
        inv = 1.0 / acc[:, DIFF_V_DIM:]
        o = acc[:, :DIFF_V_DIM] * inv
        o = o[:tq] - lam * o[tq:]
        bad = jnp.maximum(jnp.max(jnp.where(jnp.abs(o) < FINITE_MAX, 0.0, 1.0)),
                          jnp.max(jnp.where(inv > 0.0, 0.0, 1.0)))
        o = o * lax.rsqrt(jnp.mean(o * o, axis=1, keepdims=True) + RMS_EPS)
        o = o * subg_ref[...] * (1.0 - lam_init)
        o_ref[:, cols] = (o * g_ref[:, cols]).astype(o_ref.dtype)
        return bad

    refs = []
    for j in range(hp):
        m, acc = own_block(j)
        refs.append(m)
        acc_ref[j] = acc

    n_full = (qi * tq) // tk
    rest = qi * tq - n_full * tk

    def fast_pair(kp, carry):
        accumulate([(2 * kp, tk, False), (2 * kp + 1, tk, False)])
        return carry

    lax.fori_loop(0, n_full // 2, fast_pair, 0)

    @pl.when(n_full % 2 == 1)
    def _():
        accumulate([(n_full - 1, tk, False)])

    @pl.when(jnp.logical_and(rest > 0, rest <= tk // 2))
    def _():
        accumulate([(2 * n_full, tk // 2, True)])

    @pl.when(rest > tk // 2)
    def _():
        accumulate([(n_full, tk, True)])

    overflow = jnp.float32(0.0)
    for j in range(hp):
        overflow = jnp.maximum(overflow, finish(j, acc_ref[j]))

    @pl.when(overflow > 0.0)
    def _():
        def exact_step(kb, carry):
            return tuple(exact_block(kb, j, *carry[j]) for j in range(hp))
        carry = lax.fori_loop(0, qi, exact_step, tuple(own_block(j) for j in range(hp)))
        for j in range(hp):
            finish(j, carry[j][1])


def _attention(qkv, gates, lam_params, subln_g, lam_init):
    s = qkv.shape[0]
    heads = qkv.shape[1] // (3 * DIFF_V_DIM)
    tq, tk, hp = ATTN_TQ, ATTN_TK, ATTN_HP
    groups = heads // hp
    bw = hp * DIFF_V_DIM
    kern = functools.partial(_attn_kernel, tq=tq, tk=tk, hp=hp, lam_init=lam_init)
    return pl.pallas_call(
        kern,
        grid=(groups, s // tq),
        in_specs=[
            pl.BlockSpec((4, DIFF_HEAD_DIM), lambda h, i: (0, 0)),
            pl.BlockSpec((1, DIFF_V_DIM), lambda h, i: (0, 0)),
            pl.BlockSpec((tq, bw), lambda h, i: (i, h)),
            pl.BlockSpec((s, bw), lambda h, i: (0, groups + h)),
            pl.BlockSpec((s, bw), lambda h, i: (0, 2 * groups + h)),
            pl.BlockSpec((tq, bw), lambda h, i: (i, h)),
        ],
        out_specs=pl.BlockSpec((tq, bw), lambda h, i: (i, h)),
        out_shape=jax.ShapeDtypeStruct((s, heads * DIFF_V_DIM), BF16),
        scratch_shapes=[pltpu.VMEM((hp, 2 * tq, 2 * DIFF_V_DIM), F32)],
        compiler_params=pltpu.CompilerParams(
            dimension_semantics=("parallel", "parallel"), vmem_limit_bytes=VMEM_LIMIT),
        name="diff_attention",
    )(lam_params, subln_g, qkv, qkv, qkv, gates)


def _ssd_kernel(zs_ref, xs_ref, bc_ref, dt_ref, dtb_ref, alog_ref, dskip_ref, ng_ref, y_ref,
                state_ref, *, rows, chunks):
    c = pl.program_id(0)
    width = xs_ref.shape[1]
    gw = width // SSM_GROUPS
    heads_per_pair = LANES // SSM_HEAD_DIM

    @pl.when(c == 0)
    def _():
        state_ref[...] = jnp.zeros_like(state_ref)

    ri = lax.broadcasted_iota(jnp.int32, (rows, rows), 0)
    ci = lax.broadcasted_iota(jnp.int32, (rows, rows), 1)
    causal = ci <= ri
    tri = jnp.where(causal, 1.0, 0.0).astype(BF16)
    er = lax.broadcasted_iota(jnp.int32, (LANES, width), 0)
    ec = lax.broadcasted_iota(jnp.int32, (LANES, width), 1)
    expand = jnp.where((ec >> HEAD_SHIFT) == er, 1.0, 0.0).astype(BF16)
    lane = lax.broadcasted_iota(jnp.int32, (rows, LANES), 1)

    def prepare(rs):
        xs = xs_ref[rs, :]
        dtc = _softplus(dt_ref[rs, :] + dtb_ref[...])
        adt = -jnp.exp(alog_ref[...]) * dtc
        csc = sum(_dot(tri, part) for part in _split3(adt))
        dt_e = sum(_dot(part, expand) for part in _split3(dtc))
        cs_e = sum(_dot(part, expand) for part in _split3(csc))
        cs_last = cs_e[rows - 1:rows, :]
        xdt = xs * dt_e
        return dict(xs=xs, cs_e=cs_e, cs_t=csc.T, xdt_b=xdt.astype(BF16),
                    xd_b=(xdt * jnp.exp(cs_last - cs_e)).astype(BF16), ecs=jnp.exp(cs_e),
                    chunk_decay=jnp.exp(cs_last))

    def scan(rs, p):
        bcv = bc_ref[rs, :]
        for g in range(SSM_GROUPS):
            gsl = slice(g * gw, (g + 1) * gw)
            b_f = bcv[:, g * SSM_STATE:(g + 1) * SSM_STATE]
            c_b = bcv[:, (SSM_GROUPS + g) * SSM_STATE:(SSM_GROUPS + g + 1) * SSM_STATE].astype(BF16)
            scores = _dot_nt(c_b, b_f.astype(BF16))
            st = state_ref[g]
            y_off = _dot(c_b, st.astype(BF16)) * p["ecs"][:, gsl]
            state_ref[g] = (st * p["chunk_decay"][:, gsl]
                            + _dot(b_f.T.astype(BF16), p["xd_b"][:, gsl]))

            y_diag = []
            for pair in range(gw // LANES):
                col0 = g * gw + pair * LANES
                xpair = p["xdt_b"][:, col0:col0 + LANES]
                parts = []
                for hh in range(heads_per_pair):
                    h = col0 // SSM_HEAD_DIM + hh
                    seg = (p["cs_e"][:, h * SSM_HEAD_DIM:h * SSM_HEAD_DIM + 1]
                           - p["cs_t"][h:h + 1, :])
                    decay = jnp.exp(jnp.where(causal, seg, NEG_BIG))
                    parts.append(_dot((scores * decay).astype(BF16), xpair))
                y_diag.append(jnp.where(lane < SSM_HEAD_DIM, parts[0], parts[1]))
            y = jnp.concatenate(y_diag, axis=1) + y_off + dskip_ref[:, gsl] * p["xs"][:, gsl]
            y = y * zs_ref[rs, gsl]
            y = y * lax.rsqrt(jnp.mean(y * y, axis=1, keepdims=True) + RMS_EPS)
            y_ref[rs, gsl] = (y * ng_ref[:, gsl]).astype(y_ref.dtype)

    spans = [slice(k * rows, (k + 1) * rows) for k in range(chunks)]
    prepared = [prepare(rs) for rs in spans]
    for rs, p in zip(spans, prepared):
        scan(rs, p)


def _ssd(gates, xs, bc, dt_raw, dt_bias, a_log, d_skip, norm_g, width):
    s = xs.shape[0]
    rows = SSD_L * SSD_CHUNKS_PER_STEP
    bcw = 2 * SSM_GROUPS * SSM_STATE
    heads = width // SSM_HEAD_DIM

    def pad_heads(p):
        return jnp.pad(p.astype(F32), (0, LANES - heads)).reshape(1, LANES)

    def per_channel(p):
        return jnp.repeat(p.astype(F32), SSM_HEAD_DIM).reshape(1, width)

    const = lambda c: (0, 0)
    kern = functools.partial(_ssd_kernel, rows=SSD_L, chunks=SSD_CHUNKS_PER_STEP)
    return pl.pallas_call(
        kern,
        grid=(s // rows,),
        in_specs=[
            pl.BlockSpec((rows, width), lambda c: (c, 1)),
            pl.BlockSpec((rows, width), lambda c: (c, 0)),
            pl.BlockSpec((rows, bcw), lambda c: (c, 0)),
            pl.BlockSpec((rows, LANES), lambda c: (c, 0)),
            pl.BlockSpec((1, LANES), const),
            pl.BlockSpec((1, LANES), const),
            pl.BlockSpec((1, width), const),
            pl.BlockSpec((1, width), const),
        ],
        out_specs=pl.BlockSpec((rows, width), lambda c: (c, 0)),
        out_shape=jax.ShapeDtypeStruct((s, width), BF16),
        scratch_shapes=[pltpu.VMEM((SSM_GROUPS, SSM_STATE, width // SSM_GROUPS), F32)],
        compiler_params=pltpu.CompilerParams(
            dimension_semantics=("arbitrary",), vmem_limit_bytes=VMEM_LIMIT),
        name="ssd_scan",
    )(gates, xs, bc, dt_raw,
      pad_heads(dt_bias), pad_heads(a_log), per_channel(d_skip), norm_g.reshape(1, width))


def _pool_kernel(v_ref, x_ref, wgate_ref, wg_ref, bg_ref, sc_ref, y_ref, tail_ref, ext_ref,
                 wb_ref, pa_ref, pb_ref, wgb_ref, *, rows):
    i = pl.program_id(0)
    gc = wg_ref.shape[1]
    top = POOL_HALO + rows

    @pl.when(i == 0)
    def _():
        tail_ref[...] = jnp.zeros_like(tail_ref)
        wb_ref[...] = wg_ref[...].astype(BF16)
        wgb_ref[...] = wgate_ref[...].astype(BF16)
        pa_ref[0:SUBLANES, :] = jnp.zeros((SUBLANES, gc), F32)
        pb_ref[0:SUBLANES, :] = jnp.zeros((SUBLANES, gc), F32)

    ext_ref[0:POOL_HALO, :] = tail_ref[...]
    ext_ref[POOL_HALO:top, :] = v_ref[...]
    tail_ref[...] = v_ref[rows - POOL_HALO:rows, :]
    pos = (i * rows + lax.broadcasted_iota(jnp.int32, (rows, 1), 0) + 1).astype(F32)

    for gi, w in enumerate(POOL_WINDOWS):
        cols = slice(gi * gc, (gi + 1) * gc)
        v = v_ref[:, cols]
        src, span = ext_ref.at[:, cols], 1
        for dst in (pa_ref, pb_ref, pa_ref, pb_ref):
            if span == w:
                break
            dst[SUBLANES:top, :] = (src[SUBLANES:top, :]
                                    + src[SUBLANES - span:top - span, :])
            src, span = dst, 2 * span
        acc = src[POOL_HALO:top, :]
        pooled = acc * (1.0 / jnp.minimum(pos, float(w))) - v
        m = _dot(pooled.astype(BF16), wb_ref[gi]) + bg_ref[:, cols]
        gate = _silu(_dot(x_ref[...], wgb_ref[:, cols]))
        y_ref[:, cols] = (m * sc_ref[:, cols] * gate).astype(y_ref.dtype)


def _pool(v, x, w_in, w_grp, layer, b_grp, scale):
    s, width = v.shape
    k = x.shape[1]
    rows = POOL_TM
    _, ng, gc, _ = w_grp.shape
    kern = functools.partial(_pool_kernel, rows=rows)
    return pl.pallas_call(
        kern,
        grid=(s // rows,),
        in_specs=[
            pl.BlockSpec((rows, width), lambda i: (i, 0)),
            pl.BlockSpec((rows, k), lambda i: (i, 0)),
            pl.BlockSpec((None, k, width), lambda i: (layer, 0, 1),
                         pipeline_mode=pl.Buffered(1)),
            pl.BlockSpec((None, ng, gc, gc), lambda i: (layer, 0, 0, 0),
                         pipeline_mode=pl.Buffered(1)),
            pl.BlockSpec((1, width), lambda i: (0, 0)),
            pl.BlockSpec((1, width), lambda i: (0, 0)),
        ],
        out_specs=pl.BlockSpec((rows, width), lambda i: (i, 0)),
        out_shape=jax.ShapeDtypeStruct((s, width), BF16),
        scratch_shapes=[pltpu.VMEM((POOL_HALO, width), F32),
                        pltpu.VMEM((POOL_HALO + rows, width), F32),
                        pltpu.VMEM((ng, gc, gc), BF16),
                        pltpu.VMEM((POOL_HALO + rows, gc), F32),
                        pltpu.VMEM((POOL_HALO + rows, gc), F32),
                        pltpu.VMEM((k, width), BF16)],
        compiler_params=pltpu.CompilerParams(
            dimension_semantics=("arbitrary",), vmem_limit_bytes=VMEM_LIMIT),
        name="pool_mixer",
    )(v, x, w_in, w_grp, b_grp.reshape(1, width), scale.reshape(1, width))


def _out_ln_kernel(ya_ref, yb_ref, w_ref, x_ref, g_ref, b_ref, xo_ref, xob_ref, wb_ref):
    half = ya_ref.shape[1]

    @pl.when(pl.program_id(0) == 0)
    def _():
        wb_ref[...] = w_ref[...].astype(BF16)

    d_model = x_ref.shape[1]
    cw = d_model // OUT_CHUNKS
    chunks = [slice(c * cw, (c + 1) * cw) for c in range(OUT_CHUNKS)]
    hs = []
    for cs in chunks:
        y = _dot(ya_ref[...], wb_ref[:half, cs]) + _dot(yb_ref[...], wb_ref[half:, cs])
        hs.append(DEEPNORM_ALPHA * x_ref[:, cs] + y)
    mu = sum(jnp.sum(h, axis=1, keepdims=True) for h in hs) * (1.0 / d_model)
    ds = [h - mu for h in hs]
    var = sum(jnp.sum(d * d, axis=1, keepdims=True) for d in ds) * (1.0 / d_model)
    rstd = lax.rsqrt(var + LN_EPS)
    for cs, d in zip(chunks, ds):
        out = d * rstd * g_ref[:, cs] + b_ref[:, cs]
        xo_ref[:, cs] = out
        xob_ref[:, cs] = out.astype(BF16)


def _out_ln(ya, yb, ya_blk, yb_blk, w_out, layer, x, ln_g, ln_b):
    s, d = x.shape
    half = w_out.shape[1] // 2
    tm = OUT_TM
    return pl.pallas_call(
        _out_ln_kernel,
        grid=(s // tm,),
        in_specs=[
            pl.BlockSpec((tm, half), lambda i: (i, ya_blk)),
            pl.BlockSpec((tm, half), lambda i: (i, yb_blk)),
            pl.BlockSpec((None, 2 * half, d), lambda i: (layer, 0, 0),
                         pipeline_mode=pl.Buffered(1)),
            pl.BlockSpec((tm, d), lambda i: (i, 0)),
            pl.BlockSpec((1, d), lambda i: (0, 0)),
            pl.BlockSpec((1, d), lambda i: (0, 0)),
        ],
        out_specs=[pl.BlockSpec((tm, d), lambda i: (i, 0)),
                   pl.BlockSpec((tm, d), lambda i: (i, 0))],
        out_shape=[jax.ShapeDtypeStruct((s, d), F32), jax.ShapeDtypeStruct((s, d), BF16)],
        scratch_shapes=[pltpu.VMEM((2 * half, d), BF16)],
        compiler_params=pltpu.CompilerParams(
            dimension_semantics=("arbitrary",), vmem_limit_bytes=VMEM_LIMIT),
        name="out_proj_layernorm",
    )(ya, yb, w_out, x, ln_g.reshape(1, d), ln_b.reshape(1, d))


def _even_layer(x, xb, i, w_in, conv_w, conv_b, dt_bias, a_log, d_skip, ssm_norm_g,
                lq1, lk1, lq2, lk2, subln_g, w_out, ln_g, ln_b, lam_init):
    d = x.shape[1]
    att_w = d // 2
    ssm_w = d // 2
    heads = ssm_w // SSM_HEAD_DIM
    n_qkv = 3 * att_w
    n_gates = att_w + ssm_w
    n_bc = 2 * SSM_GROUPS * SSM_STATE
    c_x = n_qkv + n_gates
    c_bc = c_x + ssm_w
    conv_x = (conv_w[:, :ssm_w], conv_b[:ssm_w].reshape(1, ssm_w))
    conv_bc = (conv_w[:, ssm_w:], conv_b[ssm_w:].reshape(1, n_bc))
    qkv = _matmul(xb, w_in, i, 0, n_qkv, MM_TN, BF16, "even_in_qkv", True)
    gates = _matmul(xb, w_in, i, n_qkv, n_gates, MM_TN, F32, "even_in_gates", True, "silu")
    w_dt = jnp.pad(w_in[i, c_bc + n_bc:, :], ((0, LANES - heads), (0, 0)))
    xs, dt_raw = _xconv_dt(xb, w_in, i, c_x, ssm_w, w_dt, conv_x)
    bc = _matmul(xb, w_in, i, c_bc, n_bc, n_bc, F32, "even_in_bc", True, "conv_silu", conv_bc)
    lam_params = jnp.stack([lq1, lk1, lq2, lk2]).astype(F32)
    y_att = _attention(qkv, gates, lam_params, subln_g.reshape(1, DIFF_V_DIM), lam_init)
    y_ssm = _ssd(gates, xs, bc, dt_raw, dt_bias, a_log, d_skip, ssm_norm_g, ssm_w)
    return _out_ln(y_att, y_ssm, 0, 0, w_out, i, x, ln_g, ln_b)


def _odd_layer(x, xb, i, w_in, w_grp, b_grp, scale, w_out, ln_g, ln_b):
    width = w_in.shape[2] // 2
    v = _matmul(xb, w_in, i, 0, width, MM_TN, F32, "odd_in_v")
    y = _pool(v, xb, w_in, w_grp, i, b_grp, scale)
    return _out_ln(y, y, 0, 1, w_out, i, x, ln_g, ln_b)


def kernel(x, ev_w_in, ev_conv_w, ev_conv_b, ev_dt_bias, ev_a_log, ev_d_skip, ev_ssm_norm_g, ev_lambda_q1, ev_lambda_k1, ev_lambda_q2, ev_lambda_k2, ev_subln_g, ev_w_out, od_w_in, od_w_grp, od_b_grp, od_scale, od_w_out, ln_g, ln_b):
    bsz, s, d = x.shape
    ev_w_in = jnp.swapaxes(ev_w_in, 1, 2)
    outs = []
    for b in range(bsz):
        xf = x[b]
        xb = xf
        for l in range(DEPTH):
            i = l // 2
            if l % 2 == 0:
                lam_init = 0.8 - 0.6 * math.exp(-0.3 * l)
                xf, xb = _even_layer(
                    xf, xb, i, ev_w_in, ev_conv_w[i], ev_conv_b[i], ev_dt_bias[i], ev_a_log[i],
                    ev_d_skip[i], ev_ssm_norm_g[i], ev_lambda_q1[i], ev_lambda_k1[i],
                    ev_lambda_q2[i], ev_lambda_k2[i], ev_subln_g[i], ev_w_out,
                    ln_g[l], ln_b[l], lam_init)
            else:
                xf, xb = _odd_layer(xf, xb, i, od_w_in, od_w_grp, od_b_grp[i], od_scale[i],
                                    od_w_out, ln_g[l], ln_b[l])
        outs.append(xf)
    return jnp.stack(outs)
```

```python
import functools
import math

import jax
import jax.numpy as jnp
from jax import lax
from jax.experimental import pallas as pl
from jax.experimental.pallas import tpu as pltpu

F32 = jnp.float32
BF16 = jnp.bfloat16

DEPTH = 4
CHUNK = 64
CHUNK_SHIFT = CHUNK.bit_length() - 1
DIFF_HEAD_DIM = 64
DIFF_V_DIM = 2 * DIFF_HEAD_DIM
SSM_HEAD_DIM = 64
HEAD_SHIFT = SSM_HEAD_DIM.bit_length() - 1
SSM_GROUPS = 2
SSM_STATE = 128
CONV_WIDTH = 4
POOL_WINDOWS = (2, 4, 8, 16)
DEEPNORM_ALPHA = (2.0 * DEPTH) ** 0.25
LN_EPS = 1e-5
RMS_EPS = 1e-5

LANES = 128
SUBLANES = 8
NEG_BIG = -1e30
FINITE_MAX = 3.0e38
VMEM_LIMIT = 56 * 1024 * 1024

ATTN_TQ = 256
ATTN_TK = 1024
ATTN_HP = 4
SSD_L = 128
SSD_CHUNKS_PER_STEP = 8
MM_TM = 1024
MM_TN = 1024
OUT_TM = 512
OUT_CHUNKS = 4
POOL_TM = 512
POOL_HALO = 24


def _silu(x):
    h = 0.5 * x
    return h + h * jnp.tanh(h)


def _softplus(x):
    return jnp.maximum(x, 0.0) + jnp.log(1.0 + jnp.exp(-jnp.abs(x)))


def _split3(a):
    hi = a.astype(BF16)
    r1 = a - hi.astype(F32)
    mid = r1.astype(BF16)
    lo = (r1 - mid.astype(F32)).astype(BF16)
    return hi, mid, lo


def _dot(a, b):
    return jnp.dot(a, b, preferred_element_type=F32)


def _dot_nt(a, b):
    return lax.dot_general(a, b, (((1,), (1,)), ((), ())), preferred_element_type=F32)


def _conv_silu(raw, tail_ref, ext_ref, w_ref, b_ref):
    rows = raw.shape[0]
    ext_ref[0:SUBLANES, :] = tail_ref[...]
    ext_ref[SUBLANES:SUBLANES + rows, :] = raw
    tail_ref[...] = raw[rows - SUBLANES:rows, :]
    acc = b_ref[...]
    for t in range(CONV_WIDTH):
        start = SUBLANES - (CONV_WIDTH - 1) + t
        acc = acc + w_ref[t:t + 1, :] * ext_ref[start:start + rows, :]
    return _silu(acc)


def _matmul_kernel(*refs, w_is_nk, epilogue):
    if epilogue == "conv_silu":
        x_ref, w_ref, cw_ref, cb_ref, o_ref, wb_ref, tail_ref, ext_ref = refs
    else:
        x_ref, w_ref, o_ref, wb_ref = refs

    @pl.when(pl.program_id(1) == 0)
    def _():
        wb_ref[...] = w_ref[...].astype(BF16)
        if epilogue == "conv_silu":
            tail_ref[...] = jnp.zeros_like(tail_ref)

    dot = _dot_nt if w_is_nk else _dot
    y = dot(x_ref[...].astype(BF16), wb_ref[...])
    if epilogue == "silu":
        y = _silu(y)
    elif epilogue == "conv_silu":
        y = _conv_silu(y, tail_ref, ext_ref, cw_ref, cb_ref)
    o_ref[...] = y.astype(o_ref.dtype)


def _matmul(x, w, layer, col0, n, tn, out_dtype, name, w_is_nk=False, epilogue=None, conv=None):
    m, k = x.shape
    tm = min(MM_TM, m)
    assert n % tn == 0 and col0 % tn == 0 and m % tm == 0
    cb = col0 // tn
    if w_is_nk:
        w_spec = pl.BlockSpec((None, tn, k), lambda j, i: (layer, cb + j, 0))
    else:
        w_spec = pl.BlockSpec((None, k, tn), lambda j, i: (layer, 0, cb + j))
    in_specs = [pl.BlockSpec((tm, k), lambda j, i: (i, 0)), w_spec]
    scratch = [pltpu.VMEM((tn, k) if w_is_nk else (k, tn), BF16)]
    operands = [x, w]
    if epilogue == "conv_silu":
        in_specs += [pl.BlockSpec((CONV_WIDTH, tn), lambda j, i: (0, j)),
                     pl.BlockSpec((1, tn), lambda j, i: (0, j))]
        scratch += [pltpu.VMEM((SUBLANES, tn), F32), pltpu.VMEM((SUBLANES + tm, tn), F32)]
        operands += list(conv)
    return pl.pallas_call(
        functools.partial(_matmul_kernel, w_is_nk=w_is_nk, epilogue=epilogue),
        grid=(n // tn, m // tm),
        in_specs=in_specs,
        out_specs=pl.BlockSpec((tm, tn), lambda j, i: (i, j)),
        out_shape=jax.ShapeDtypeStruct((m, n), out_dtype),
        scratch_shapes=scratch,
        compiler_params=pltpu.CompilerParams(
            dimension_semantics=("arbitrary", "arbitrary"), vmem_limit_bytes=VMEM_LIMIT),
        name=name,
    )(*operands)


def _xconv_dt_kernel(x_ref, w_ref, wdt_ref, cw_ref, cb_ref, o_ref, dt_ref,
                     wb_ref, wdtb_ref, tail_ref, ext_ref):
    @pl.when(pl.program_id(0) == 0)
    def _():
        wb_ref[...] = w_ref[...].astype(BF16)
        wdtb_ref[...] = wdt_ref[...].astype(BF16)
        tail_ref[...] = jnp.zeros_like(tail_ref)

    x = x_ref[...].astype(BF16)
    o_ref[...] = _conv_silu(_dot_nt(x, wb_ref[...]), tail_ref, ext_ref, cw_ref, cb_ref)
    dt_ref[...] = _dot_nt(x, wdtb_ref[...])


def _xconv_dt(x, w, layer, row0, n, w_dt, conv):
    m, k = x.shape
    tm = min(MM_TM, m)
    assert row0 % n == 0 and m % tm == 0
    const = lambda i: (0, 0)
    return pl.pallas_call(
        _xconv_dt_kernel,
        grid=(m // tm,),
        in_specs=[pl.BlockSpec((tm, k), lambda i: (i, 0)),
                  pl.BlockSpec((None, n, k), lambda i: (layer, row0 // n, 0)),
                  pl.BlockSpec((LANES, k), const),
                  pl.BlockSpec((CONV_WIDTH, n), const),
                  pl.BlockSpec((1, n), const)],
        out_specs=[pl.BlockSpec((tm, n), lambda i: (i, 0)),
                   pl.BlockSpec((tm, LANES), lambda i: (i, 0))],
        out_shape=[jax.ShapeDtypeStruct((m, n), F32), jax.ShapeDtypeStruct((m, LANES), F32)],
        scratch_shapes=[pltpu.VMEM((n, k), BF16), pltpu.VMEM((LANES, k), BF16),
                        pltpu.VMEM((SUBLANES, n), F32), pltpu.VMEM((SUBLANES + tm, n), F32)],
        compiler_params=pltpu.CompilerParams(
            dimension_semantics=("arbitrary",), vmem_limit_bytes=VMEM_LIMIT),
        name="even_in_x_dt",
    )(x, w, w_dt, *conv)


def _attn_kernel(lam_ref, subg_ref, q_ref, k_ref, v_ref, g_ref, o_ref, acc_ref,
                 *, tq, tk, hp, lam_init):
    qi = pl.program_id(1)
    lp = lam_ref[...]
    lam = (jnp.exp(jnp.sum(lp[0:1] * lp[1:2], axis=1, keepdims=True))
           - jnp.exp(jnp.sum(lp[2:3] * lp[3:4], axis=1, keepdims=True)) + lam_init)

    lane = lax.broadcasted_iota(jnp.int32, (tq, DIFF_V_DIM), 1)
    row = lax.broadcasted_iota(jnp.int32, (2 * tq, 1), 0)
    q_chunk = (qi * tq + jnp.where(row >= tq, row - tq, row)) >> CHUNK_SHIFT

    def stacked_q(j):
        q = q_ref[:, j * DIFF_V_DIM:(j + 1) * DIFF_V_DIM] * (DIFF_HEAD_DIM ** -0.5)
        zero = jnp.zeros_like(q)
        return jnp.concatenate([jnp.where(lane < DIFF_HEAD_DIM, q, zero),
                                jnp.where(lane >= DIFF_HEAD_DIM, q, zero)], axis=0)

    qq = [stacked_q(j) for j in range(hp)]

    def kv(kb, j, size):
        ks = pl.multiple_of(kb * size, size)
        cols = slice(j * DIFF_V_DIM, (j + 1) * DIFF_V_DIM)
        k = k_ref[pl.ds(ks, size), cols]
        v1 = jnp.concatenate([v_ref[pl.ds(ks, size), cols],
                              jnp.ones((size, DIFF_V_DIM), BF16)], axis=1)
        return ks, k, v1

    def exact_block(kb, j, m, acc):
        _, k, v1 = kv(kb, j, tq)
        s = _dot_nt(qq[j], k)
        m_new = jnp.maximum(m, jnp.max(s, axis=1, keepdims=True))
        p = jnp.exp(s - m_new).astype(BF16)
        return m_new, jnp.exp(m - m_new) * acc + _dot(p, v1)

    def own_block(j):
        ks, k, v1 = kv(qi, j, tq)
        k_chunk = (ks + lax.broadcasted_iota(jnp.int32, (1, tq), 1)) >> CHUNK_SHIFT
        s = jnp.where(k_chunk <= q_chunk, _dot_nt(qq[j], k), NEG_BIG)
        m = jnp.max(s, axis=1, keepdims=True)
        return m, _dot(jnp.exp(s - m).astype(BF16), v1)

    def fast_block(kb, j, size, straddles):
        ks, k, v1 = kv(kb, j, size)
        x = _dot_nt(qq[j], k) - refs[j]
        if straddles:
            before = (ks + lax.broadcasted_iota(jnp.int32, (1, size), 1)) < qi * tq
            x = jnp.where(before, x, NEG_BIG)
        return _dot(jnp.exp(x).astype(BF16), v1)

    def accumulate(blocks):
        for j in range(hp):
            total = acc_ref[j]
            for kb, size, straddles in blocks:
                total = total + fast_block(kb, j, size, straddles)
            acc_ref[j] = total

    def finish(j, acc):
        cols = slice(j * DIFF_V_DIM, (j + 1) * DIFF_V_DIM)
        inv = 1.0 / acc[:, DIFF_V_DIM:]
        o = acc[:, :DIFF_V_DIM] * inv
        o = o[:tq] - lam * o[tq:]
        bad = jnp.maximum(jnp.max(jnp.where(jnp.abs(o) < FINITE_MAX, 0.0, 1.0)),
                          jnp.max(jnp.where(inv > 0.0, 0.0, 1.0)))
        o = o * lax.rsqrt(jnp.mean(o * o, axis=1, keepdims=True) + RMS_EPS)
        o = o * subg_ref[...] * (1.0 - lam_init)
        o_ref[:, cols] = (o * g_ref[:, cols]).astype(o_ref.dtype)
        return bad

    refs = []
    for j in range(hp):
        m, acc = own_block(j)
        refs.append(m)
        acc_ref[j] = acc

    n_full = (qi * tq) // tk
    rest = qi * tq - n_full * tk

    def fast_pair(kp, carry):
        accumulate([(2 * kp, tk, False), (2 * kp + 1, tk, False)])
        return carry

    lax.fori_loop(0, n_full // 2, fast_pair, 0)

    @pl.when(n_full % 2 == 1)
    def _():
        accumulate([(n_full - 1, tk, False)])

    @pl.when(jnp.logical_and(rest > 0, rest <= tk // 2))
    def _():
        accumulate([(2 * n_full, tk // 2, True)])

    @pl.when(rest > tk // 2)
    def _():
        accumulate([(n_full, tk, True)])

    overflow = jnp.float32(0.0)
    for j in range(hp):
        overflow = jnp.maximum(overflow, finish(j, acc_ref[j]))

    @pl.when(overflow > 0.0)
    def _():
        def exact_step(kb, carry):
            return tuple(exact_block(kb, j, *carry[j]) for j in range(hp))
        carry = lax.fori_loop(0, qi, exact_step, tuple(own_block(j) for j in range(hp)))
        for j in range(hp):
            finish(j, carry[j][1])


def _attention(qkv, gates, lam_params, subln_g, lam_init):
    s = qkv.shape[0]
    heads = qkv.shape[1] // (3 * DIFF_V_DIM)
    tq, tk, hp = ATTN_TQ, ATTN_TK, ATTN_HP
    groups = heads // hp
    bw = hp * DIFF_V_DIM
    kern = functools.partial(_attn_kernel, tq=tq, tk=tk, hp=hp, lam_init=lam_init)
    return pl.pallas_call(
        kern,
        grid=(groups, s // tq),
        in_specs=[
            pl.BlockSpec((4, DIFF_HEAD_DIM), lambda h, i: (0, 0)),
            pl.BlockSpec((1, DIFF_V_DIM), lambda h, i: (0, 0)),
            pl.BlockSpec((tq, bw), lambda h, i: (i, h)),
            pl.BlockSpec((s, bw), lambda h, i: (0, groups + h)),
            pl.BlockSpec((s, bw), lambda h, i: (0, 2 * groups + h)),
            pl.BlockSpec((tq, bw), lambda h, i: (i, h)),
        ],
        out_specs=pl.BlockSpec((tq, bw), lambda h, i: (i, h)),
        out_shape=jax.ShapeDtypeStruct((s, heads * DIFF_V_DIM), BF16),
        scratch_shapes=[pltpu.VMEM((hp, 2 * tq, 2 * DIFF_V_DIM), F32)],
        compiler_params=pltpu.CompilerParams(
            dimension_semantics=("parallel", "parallel"), vmem_limit_bytes=VMEM_LIMIT),
        name="diff_attention",
    )(lam_params, subln_g, qkv, qkv, qkv, gates)


def _ssd_kernel(zs_ref, xs_ref, bc_ref, dt_ref, dtb_ref, alog_ref, dskip_ref, ng_ref, y_ref,
                state_ref, *, rows, chunks):
    c = pl.program_id(0)
    width = xs_ref.shape[1]
    gw = width // SSM_GROUPS
    heads_per_pair = LANES // SSM_HEAD_DIM

    @pl.when(c == 0)
    def _():
        state_ref[...] = jnp.zeros_like(state_ref)

    ri = lax.broadcasted_iota(jnp.int32, (rows, rows), 0)
    ci = lax.broadcasted_iota(jnp.int32, (rows, rows), 1)
    causal = ci <= ri
    tri = jnp.where(causal, 1.0, 0.0).astype(BF16)
    er = lax.broadcasted_iota(jnp.int32, (LANES, width), 0)
    ec = lax.broadcasted_iota(jnp.int32, (LANES, width), 1)
    expand = jnp.where((ec >> HEAD_SHIFT) == er, 1.0, 0.0).astype(BF16)
    lane = lax.broadcasted_iota(jnp.int32, (rows, LANES), 1)

    def prepare(rs):
        xs = xs_ref[rs, :]
        dtc = _softplus(dt_ref[rs, :] + dtb_ref[...])
        adt = -jnp.exp(alog_ref[...]) * dtc
        csc = sum(_dot(tri, part) for part in _split3(adt))
        dt_e = sum(_dot(part, expand) for part in _split3(dtc))
        cs_e = sum(_dot(part, expand) for part in _split3(csc))
        cs_last = cs_e[rows - 1:rows, :]
        xdt = xs * dt_e
        bcv = bc_ref[rs, :]
        b_f = [bcv[:, g * SSM_STATE:(g + 1) * SSM_STATE] for g in range(SSM_GROUPS)]
        c_b = [bcv[:, (SSM_GROUPS + g) * SSM_STATE:(SSM_GROUPS + g + 1) * SSM_STATE].astype(BF16)
               for g in range(SSM_GROUPS)]
        scores = [_dot_nt(c_b[g], b_f[g].astype(BF16)) for g in range(SSM_GROUPS)]
        return dict(xs=xs, cs_e=cs_e, cs_t=csc.T, xdt_b=xdt.astype(BF16), b_f=b_f, c_b=c_b,
                    scores=scores,
                    xd_b=(xdt * jnp.exp(cs_last - cs_e)).astype(BF16), ecs=jnp.exp(cs_e),
                    chunk_decay=jnp.exp(cs_last))

    def scan(rs, p):
        for g in range(SSM_GROUPS):
            gsl = slice(g * gw, (g + 1) * gw)
            b_f, c_b, scores = p["b_f"][g], p["c_b"][g], p["scores"][g]
            st = state_ref[g]
            y_off = _dot(c_b, st.astype(BF16)) * p["ecs"][:, gsl]
            state_ref[g] = (st * p["chunk_decay"][:, gsl]
                            + _dot(b_f.T.astype(BF16), p["xd_b"][:, gsl]))

            y_diag = []
            for pair in range(gw // LANES):
                col0 = g * gw + pair * LANES
                xpair = p["xdt_b"][:, col0:col0 + LANES]
                parts = []
                for hh in range(heads_per_pair):
                    h = col0 // SSM_HEAD_DIM + hh
                    seg = (p["cs_e"][:, h * SSM_HEAD_DIM:h * SSM_HEAD_DIM + 1]
                           - p["cs_t"][h:h + 1, :])
                    decay = jnp.exp(jnp.where(causal, seg, NEG_BIG))
                    parts.append(_dot((scores * decay).astype(BF16), xpair))
                y_diag.append(jnp.where(lane < SSM_HEAD_DIM, parts[0], parts[1]))
            y = jnp.concatenate(y_diag, axis=1) + y_off + dskip_ref[:, gsl] * p["xs"][:, gsl]
            y = y * zs_ref[rs, gsl]
            y = y * lax.rsqrt(jnp.mean(y * y, axis=1, keepdims=True) + RMS_EPS)
            y_ref[rs, gsl] = (y * ng_ref[:, gsl]).astype(y_ref.dtype)

    spans = [slice(k * rows, (k + 1) * rows) for k in range(chunks)]
    prepared = [prepare(rs) for rs in spans]
    for rs, p in zip(spans, prepared):
        scan(rs, p)


def _ssd(gates, xs, bc, dt_raw, dt_bias, a_log, d_skip, norm_g, width):
    s = xs.shape[0]
    rows = SSD_L * SSD_CHUNKS_PER_STEP
    bcw = 2 * SSM_GROUPS * SSM_STATE
    heads = width // SSM_HEAD_DIM

    def pad_heads(p):
        return jnp.pad(p.astype(F32), (0, LANES - heads)).reshape(1, LANES)

    def per_channel(p):
        return jnp.repeat(p.astype(F32), SSM_HEAD_DIM).reshape(1, width)

    const = lambda c: (0, 0)
    kern = functools.partial(_ssd_kernel, rows=SSD_L, chunks=SSD_CHUNKS_PER_STEP)
    return pl.pallas_call(
        kern,
        grid=(s // rows,),
        in_specs=[
            pl.BlockSpec((rows, width), lambda c: (c, 1)),
            pl.BlockSpec((rows, width), lambda c: (c, 0)),
            pl.BlockSpec((rows, bcw), lambda c: (c, 0)),
            pl.BlockSpec((rows, LANES), lambda c: (c, 0)),
            pl.BlockSpec((1, LANES), const),
            pl.BlockSpec((1, LANES), const),
            pl.BlockSpec((1, width), const),
            pl.BlockSpec((1, width), const),
        ],
        out_specs=pl.BlockSpec((rows, width), lambda c: (c, 0)),
        out_shape=jax.ShapeDtypeStruct((s, width), BF16),
        scratch_shapes=[pltpu.VMEM((SSM_GROUPS, SSM_STATE, width // SSM_GROUPS), F32)],
        compiler_params=pltpu.CompilerParams(
            dimension_semantics=("arbitrary",), vmem_limit_bytes=VMEM_LIMIT),
        name="ssd_scan",
    )(gates, xs, bc, dt_raw,
      pad_heads(dt_bias), pad_heads(a_log), per_channel(d_skip), norm_g.reshape(1, width))


def _pool_kernel(v_ref, x_ref, wgate_ref, wg_ref, bg_ref, sc_ref, y_ref, tail_ref, ext_ref,
                 wb_ref, pa_ref, pb_ref, wgb_ref, *, rows):
    i = pl.program_id(0)
    gc = wg_ref.shape[1]
    top = POOL_HALO + rows

    @pl.when(i == 0)
    def _():
        tail_ref[...] = jnp.zeros_like(tail_ref)
        wb_ref[...] = wg_ref[...].astype(BF16)
        wgb_ref[...] = wgate_ref[...].astype(BF16)
        pa_ref[0:SUBLANES, :] = jnp.zeros((SUBLANES, gc), F32)
        pb_ref[0:SUBLANES, :] = jnp.zeros((SUBLANES, gc), F32)

    ext_ref[0:POOL_HALO, :] = tail_ref[...]
    ext_ref[POOL_HALO:top, :] = v_ref[...]
    tail_ref[...] = v_ref[rows - POOL_HALO:rows, :]
    pos = (i * rows + lax.broadcasted_iota(jnp.int32, (rows, 1), 0) + 1).astype(F32)

    for gi, w in enumerate(POOL_WINDOWS):
        cols = slice(gi * gc, (gi + 1) * gc)
        v = v_ref[:, cols]
        src, span = ext_ref.at[:, cols], 1
        for dst in (pa_ref, pb_ref, pa_ref, pb_ref):
            if span == w:
                break
            dst[SUBLANES:top, :] = (src[SUBLANES:top, :]
                                    + src[SUBLANES - span:top - span, :])
            src, span = dst, 2 * span
        acc = src[POOL_HALO:top, :]
        pooled = acc * (1.0 / jnp.minimum(pos, float(w))) - v
        m = _dot(pooled.astype(BF16), wb_ref[gi]) + bg_ref[:, cols]
        gate = _silu(_dot(x_ref[...], wgb_ref[:, cols]))
        y_ref[:, cols] = (m * sc_ref[:, cols] * gate).astype(y_ref.dtype)


def _pool(v, x, w_in, w_grp, layer, b_grp, scale):
    s, width = v.shape
    k = x.shape[1]
    rows = POOL_TM
    _, ng, gc, _ = w_grp.shape
    kern = functools.partial(_pool_kernel, rows=rows)
    return pl.pallas_call(
        kern,
        grid=(s // rows,),
        in_specs=[
            pl.BlockSpec((rows, width), lambda i: (i, 0)),
            pl.BlockSpec((rows, k), lambda i: (i, 0)),
            pl.BlockSpec((None, k, width), lambda i: (layer, 0, 1),
                         pipeline_mode=pl.Buffered(1)),
            pl.BlockSpec((None, ng, gc, gc), lambda i: (layer, 0, 0, 0),
                         pipeline_mode=pl.Buffered(1)),
            pl.BlockSpec((1, width), lambda i: (0, 0)),
            pl.BlockSpec((1, width), lambda i: (0, 0)),
        ],
        out_specs=pl.BlockSpec((rows, width), lambda i: (i, 0)),
        out_shape=jax.ShapeDtypeStruct((s, width), BF16),
        scratch_shapes=[pltpu.VMEM((POOL_HALO, width), F32),
                        pltpu.VMEM((POOL_HALO + rows, width), F32),
                        pltpu.VMEM((ng, gc, gc), BF16),
                        pltpu.VMEM((POOL_HALO + rows, gc), F32),
                        pltpu.VMEM((POOL_HALO + rows, gc), F32),
                        pltpu.VMEM((k, width), BF16)],
        compiler_params=pltpu.CompilerParams(
            dimension_semantics=("arbitrary",), vmem_limit_bytes=VMEM_LIMIT),
        name="pool_mixer",
    )(v, x, w_in, w_grp, b_grp.reshape(1, width), scale.reshape(1, width))


def _out_ln_kernel(ya_ref, yb_ref, w_ref, x_ref, g_ref, b_ref, xo_ref, xob_ref, wb_ref):
    half = ya_ref.shape[1]

    @pl.when(pl.program_id(0) == 0)
    def _():
        wb_ref[...] = w_ref[...].astype(BF16)

    d_model = x_ref.shape[1]
    cw = d_model // OUT_CHUNKS
    chunks = [slice(c * cw, (c + 1) * cw) for c in range(OUT_CHUNKS)]
    hs = []
    for cs in chunks:
        y = _dot(ya_ref[...], wb_ref[:half, cs]) + _dot(yb_ref[...], wb_ref[half:, cs])
        hs.append(DEEPNORM_ALPHA * x_ref[:, cs] + y)
    mu = sum(jnp.sum(h, axis=1, keepdims=True) for h in hs) * (1.0 / d_model)
    ds = [h - mu for h in hs]
    var = sum(jnp.sum(d * d, axis=1, keepdims=True) for d in ds) * (1.0 / d_model)
    rstd = lax.rsqrt(var + LN_EPS)
    for cs, d in zip(chunks, ds):
        out = d * rstd * g_ref[:, cs] + b_ref[:, cs]
        xo_ref[:, cs] = out
        xob_ref[:, cs] = out.astype(BF16)


def _out_ln(ya, yb, ya_blk, yb_blk, w_out, layer, x, ln_g, ln_b):
    s, d = x.shape
    half = w_out.shape[1] // 2
    tm = OUT_TM
    return pl.pallas_call(
        _out_ln_kernel,
        grid=(s // tm,),
        in_specs=[
            pl.BlockSpec((tm, half), lambda i: (i, ya_blk)),
            pl.BlockSpec((tm, half), lambda i: (i, yb_blk)),
            pl.BlockSpec((None, 2 * half, d), lambda i: (layer, 0, 0),
                         pipeline_mode=pl.Buffered(1)),
            pl.BlockSpec((tm, d), lambda i: (i, 0)),
            pl.BlockSpec((1, d), lambda i: (0, 0)),
            pl.BlockSpec((1, d), lambda i: (0, 0)),
        ],
        out_specs=[pl.BlockSpec((tm, d), lambda i: (i, 0)),
                   pl.BlockSpec((tm, d), lambda i: (i, 0))],
        out_shape=[jax.ShapeDtypeStruct((s, d), F32), jax.ShapeDtypeStruct((s, d), BF16)],
        scratch_shapes=[pltpu.VMEM((2 * half, d), BF16)],
        compiler_params=pltpu.CompilerParams(
            dimension_semantics=("arbitrary",), vmem_limit_bytes=VMEM_LIMIT),
        name="out_proj_layernorm",
    )(ya, yb, w_out, x, ln_g.reshape(1, d), ln_b.reshape(1, d))


def _even_layer(x, xb, i, w_in, conv_w, conv_b, dt_bias, a_log, d_skip, ssm_norm_g,
                lq1, lk1, lq2, lk2, subln_g, w_out, ln_g, ln_b, lam_init):
    d = x.shape[1]
    att_w = d // 2
    ssm_w = d // 2
    heads = ssm_w // SSM_HEAD_DIM
    n_qkv = 3 * att_w
    n_gates = att_w + ssm_w
    n_bc = 2 * SSM_GROUPS * SSM_STATE
    c_x = n_qkv + n_gates
    c_bc = c_x + ssm_w
    conv_x = (conv_w[:, :ssm_w], conv_b[:ssm_w].reshape(1, ssm_w))
    conv_bc = (conv_w[:, ssm_w:], conv_b[ssm_w:].reshape(1, n_bc))
    qkv = _matmul(xb, w_in, i, 0, n_qkv, MM_TN, BF16, "even_in_qkv", True)
    gates = _matmul(xb, w_in, i, n_qkv, n_gates, MM_TN, F32, "even_in_gates", True, "silu")
    w_dt = jnp.pad(w_in[i, c_bc + n_bc:, :], ((0, LANES - heads), (0, 0)))
    xs, dt_raw = _xconv_dt(xb, w_in, i, c_x, ssm_w, w_dt, conv_x)
    bc = _matmul(xb, w_in, i, c_bc, n_bc, n_bc, F32, "even_in_bc", True, "conv_silu", conv_bc)
    lam_params = jnp.stack([lq1, lk1, lq2, lk2]).astype(F32)
    y_att = _attention(qkv, gates, lam_params, subln_g.reshape(1, DIFF_V_DIM), lam_init)
    y_ssm = _ssd(gates, xs, bc, dt_raw, dt_bias, a_log, d_skip, ssm_norm_g, ssm_w)
    return _out_ln(y_att, y_ssm, 0, 0, w_out, i, x, ln_g, ln_b)


def _odd_layer(x, xb, i, w_in, w_grp, b_grp, scale, w_out, ln_g, ln_b):
    width = w_in.shape[2] // 2
    v = _matmul(xb, w_in, i, 0, width, MM_TN, F32, "odd_in_v")
    y = _pool(v, xb, w_in, w_grp, i, b_grp, scale)
    return _out_ln(y, y, 0, 1, w_out, i, x, ln_g, ln_b)


def kernel(x, ev_w_in, ev_conv_w, ev_conv_b, ev_dt_bias, ev_a_log, ev_d_skip, ev_ssm_norm_g, ev_lambda_q1, ev_lambda_k1, ev_lambda_q2, ev_lambda_k2, ev_subln_g, ev_w_out, od_w_in, od_w_grp, od_b_grp, od_scale, od_w_out, ln_g, ln_b):
    bsz, s, d = x.shape
    ev_w_in = jnp.swapaxes(ev_w_in, 1, 2)
    outs = []
    for b in range(bsz):
        xf = x[b]
        xb = xf
        for l in range(DEPTH):
            i = l // 2
            if l % 2 == 0:
                lam_init = 0.8 - 0.6 * math.exp(-0.3 * l)
                xf, xb = _even_layer(
                    xf, xb, i, ev_w_in, ev_conv_w[i], ev_conv_b[i], ev_dt_bias[i], ev_a_log[i],
                    ev_d_skip[i], ev_ssm_norm_g[i], ev_lambda_q1[i], ev_lambda_k1[i],
                    ev_lambda_q2[i], ev_lambda_k2[i], ev_subln_g[i], ev_w_out,
                    ln_g[l], ln_b[l], lam_init)
            else:
                xf, xb = _odd_layer(xf, xb, i, od_w_in, od_w_grp, od_b_grp[i], od_scale[i],
                                    od_w_out, ln_g[l], ln_b[l])
        outs.append(xf)
    return jnp.stack(outs)
```

```python
import functools
import math

import jax
import jax.numpy as jnp
from jax import lax
from jax.experimental import pallas as pl
from jax.experimental.pallas import tpu as pltpu

F32 = jnp.float32
BF16 = jnp.bfloat16

DEPTH = 4
CHUNK = 64
CHUNK_SHIFT = CHUNK.bit_length() - 1
DIFF_HEAD_DIM = 64
DIFF_V_DIM = 2 * DIFF_HEAD_DIM
SSM_HEAD_DIM = 64
HEAD_SHIFT = SSM_HEAD_DIM.bit_length() - 1
SSM_GROUPS = 2
SSM_STATE = 128
CONV_WIDTH = 4
POOL_WINDOWS = (2, 4, 8, 16)
DEEPNORM_ALPHA = (2.0 * DEPTH) ** 0.25
LN_EPS = 1e-5
RMS_EPS = 1e-5

LANES = 128
SUBLANES = 8
NEG_BIG = -1e30
FINITE_MAX = 3.0e38
VMEM_LIMIT = 56 * 1024 * 1024

ATTN_TQ = 256
ATTN_TK = 1024
ATTN_HP = 4
SSD_L = 128
SSD_CHUNKS_PER_STEP = 8
MM_TM = 1024
MM_TN = 1024
OUT_TM = 512
OUT_CHUNKS = 4
POOL_TM = 512
POOL_HALO = 24


def _silu(x):
    h = 0.5 * x
    return h + h * jnp.tanh(h)


def _softplus(x):
    return jnp.maximum(x, 0.0) + jnp.log(1.0 + jnp.exp(-jnp.abs(x)))


def _split3(a):
    hi = a.astype(BF16)
    r1 = a - hi.astype(F32)
    mid = r1.astype(BF16)
    lo = (r1 - mid.astype(F32)).astype(BF16)
    return hi, mid, lo


def _dot(a, b):
    return jnp.dot(a, b, preferred_element_type=F32)


def _dot_nt(a, b):
    return lax.dot_general(a, b, (((1,), (1,)), ((), ())), preferred_element_type=F32)


def _conv_silu(raw, tail_ref, ext_ref, w_ref, b_ref):
    rows = raw.shape[0]
    ext_ref[0:SUBLANES, :] = tail_ref[...]
    ext_ref[SUBLANES:SUBLANES + rows, :] = raw
    tail_ref[...] = raw[rows - SUBLANES:rows, :]
    acc = b_ref[...]
    for t in range(CONV_WIDTH):
        start = SUBLANES - (CONV_WIDTH - 1) + t
        acc = acc + w_ref[t:t + 1, :] * ext_ref[start:start + rows, :]
    return _silu(acc)


def _matmul_kernel(*refs, w_is_nk, epilogue):
    if epilogue == "conv_silu":
        x_ref, w_ref, cw_ref, cb_ref, o_ref, wb_ref, tail_ref, ext_ref = refs
    else:
        x_ref, w_ref, o_ref, wb_ref = refs

    @pl.when(pl.program_id(1) == 0)
    def _():
        wb_ref[...] = w_ref[...].astype(BF16)
        if epilogue == "conv_silu":
            tail_ref[...] = jnp.zeros_like(tail_ref)

    dot = _dot_nt if w_is_nk else _dot
    y = dot(x_ref[...].astype(BF16), wb_ref[...])
    if epilogue == "silu":
        y = _silu(y)
    elif epilogue == "conv_silu":
        y = _conv_silu(y, tail_ref, ext_ref, cw_ref, cb_ref)
    o_ref[...] = y.astype(o_ref.dtype)


def _matmul(x, w, layer, col0, n, tn, out_dtype, name, w_is_nk=False, epilogue=None, conv=None):
    m, k = x.shape
    tm = min(MM_TM, m)
    assert n % tn == 0 and col0 % tn == 0 and m % tm == 0
    cb = col0 // tn
    if w_is_nk:
        w_spec = pl.BlockSpec((None, tn, k), lambda j, i: (layer, cb + j, 0))
    else:
        w_spec = pl.BlockSpec((None, k, tn), lambda j, i: (layer, 0, cb + j))
    in_specs = [pl.BlockSpec((tm, k), lambda j, i: (i, 0)), w_spec]
    scratch = [pltpu.VMEM((tn, k) if w_is_nk else (k, tn), BF16)]
    operands = [x, w]
    if epilogue == "conv_silu":
        in_specs += [pl.BlockSpec((CONV_WIDTH, tn), lambda j, i: (0, j)),
                     pl.BlockSpec((1, tn), lambda j, i: (0, j))]
        scratch += [pltpu.VMEM((SUBLANES, tn), F32), pltpu.VMEM((SUBLANES + tm, tn), F32)]
        operands += list(conv)
    return pl.pallas_call(
        functools.partial(_matmul_kernel, w_is_nk=w_is_nk, epilogue=epilogue),
        grid=(n // tn, m // tm),
        in_specs=in_specs,
        out_specs=pl.BlockSpec((tm, tn), lambda j, i: (i, j)),
        out_shape=jax.ShapeDtypeStruct((m, n), out_dtype),
        scratch_shapes=scratch,
        compiler_params=pltpu.CompilerParams(
            dimension_semantics=("arbitrary", "arbitrary"), vmem_limit_bytes=VMEM_LIMIT),
        name=name,
    )(*operands)


def _xconv_dt_kernel(x_ref, w_ref, wdt_ref, cw_ref, cb_ref, o_ref, dt_ref,
                     wb_ref, wdtb_ref, tail_ref, ext_ref):
    @pl.when(pl.program_id(0) == 0)
    def _():
        wb_ref[...] = w_ref[...].astype(BF16)
        wdtb_ref[...] = wdt_ref[...].astype(BF16)
        tail_ref[...] = jnp.zeros_like(tail_ref)

    x = x_ref[...].astype(BF16)
    o_ref[...] = _conv_silu(_dot_nt(x, wb_ref[...]), tail_ref, ext_ref, cw_ref, cb_ref)
    dt_ref[...] = _dot_nt(x, wdtb_ref[...])


def _xconv_dt(x, w, layer, row0, n, w_dt, conv):
    m, k = x.shape
    tm = min(MM_TM, m)
    assert row0 % n == 0 and m % tm == 0
    const = lambda i: (0, 0)
    return pl.pallas_call(
        _xconv_dt_kernel,
        grid=(m // tm,),
        in_specs=[pl.BlockSpec((tm, k), lambda i: (i, 0)),
                  pl.BlockSpec((None, n, k), lambda i: (layer, row0 // n, 0)),
                  pl.BlockSpec((LANES, k), const),
                  pl.BlockSpec((CONV_WIDTH, n), const),
                  pl.BlockSpec((1, n), const)],
        out_specs=[pl.BlockSpec((tm, n), lambda i: (i, 0)),
                   pl.BlockSpec((tm, LANES), lambda i: (i, 0))],
        out_shape=[jax.ShapeDtypeStruct((m, n), F32), jax.ShapeDtypeStruct((m, LANES), F32)],
        scratch_shapes=[pltpu.VMEM((n, k), BF16), pltpu.VMEM((LANES, k), BF16),
                        pltpu.VMEM((SUBLANES, n), F32), pltpu.VMEM((SUBLANES + tm, n), F32)],
        compiler_params=pltpu.CompilerParams(
            dimension_semantics=("arbitrary",), vmem_limit_bytes=VMEM_LIMIT),
        name="even_in_x_dt",
    )(x, w, w_dt, *conv)


def _attn_kernel(lam_ref, subg_ref, q_ref, k_ref, v_ref, g_ref, o_ref, acc_ref,
                 *, tq, tk, hp, lam_init):
    qi = pl.program_id(1)
    lp = lam_ref[...]
    lam = (jnp.exp(jnp.sum(lp[0:1] * lp[1:2], axis=1, keepdims=True))
           - jnp.exp(jnp.sum(lp[2:3] * lp[3:4], axis=1, keepdims=True)) + lam_init)

    lane = lax.broadcasted_iota(jnp.int32, (tq, DIFF_V_DIM), 1)
    row = lax.broadcasted_iota(jnp.int32, (2 * tq, 1), 0)
    q_chunk = (qi * tq + jnp.where(row >= tq, row - tq, row)) >> CHUNK_SHIFT

    def stacked_q(j):
        q = q_ref[:, j * DIFF_V_DIM:(j + 1) * DIFF_V_DIM] * (DIFF_HEAD_DIM ** -0.5)
        zero = jnp.zeros_like(q)
        return jnp.concatenate([jnp.where(lane < DIFF_HEAD_DIM, q, zero),
                                jnp.where(lane >= DIFF_HEAD_DIM, q, zero)], axis=0)

    qq = [stacked_q(j) for j in range(hp)]

    def kv(kb, j, size):
        ks = pl.multiple_of(kb * size, size)
        cols = slice(j * DIFF_V_DIM, (j + 1) * DIFF_V_DIM)
        k = k_ref[pl.ds(ks, size), cols]
        v1 = jnp.concatenate([v_ref[pl.ds(ks, size), cols],
                              jnp.ones((size, DIFF_V_DIM), BF16)], axis=1)
        return ks, k, v1

    def exact_block(kb, j, m, acc):
        _, k, v1 = kv(kb, j, tq)
        s = _dot_nt(qq[j], k)
        m_new = jnp.maximum(m, jnp.max(s, axis=1, keepdims=True))
        p = jnp.exp(s - m_new).astype(BF16)
        return m_new, jnp.exp(m - m_new) * acc + _dot(p, v1)

    def own_block(j):
        ks, k, v1 = kv(qi, j, tq)
        k_chunk = (ks + lax.broadcasted_iota(jnp.int32, (1, tq), 1)) >> CHUNK_SHIFT
        s = jnp.where(k_chunk <= q_chunk, _dot_nt(qq[j], k), NEG_BIG)
        m = jnp.max(s, axis=1, keepdims=True)
        return m, _dot(jnp.exp(s - m).astype(BF16), v1)

    def fast_block(kb, j, size, straddles):
        ks, k, v1 = kv(kb, j, size)
        x = _dot_nt(qq[j], k) - refs[j]
        if straddles:
            before = (ks + lax.broadcasted_iota(jnp.int32, (1, size), 1)) < qi * tq
            x = jnp.where(before, x, NEG_BIG)
        return _dot(jnp.exp(x).astype(BF16), v1)

    def accumulate(blocks):
        for j in range(hp):
            total = acc_ref[j]
            for kb, size, straddles in blocks:
                total = total + fast_block(kb, j, size, straddles)
            acc_ref[j] = total

    def finish(j, acc):
        cols = slice(j * DIFF_V_DIM, (j + 1) * DIFF_V_DIM)
        inv = 1.0 / acc[:, DIFF_V_DIM:]
        o = acc[:, :DIFF_V_DIM] * inv
        o = o[:tq] - lam * o[tq:]
        bad = jnp.maximum(jnp.max(jnp.where(jnp.abs(o) < FINITE_MAX, 0.0, 1.0)),
                          jnp.max(jnp.where(inv > 0.0, 0.0, 1.0)))
        o = o * lax.rsqrt(jnp.mean(o * o, axis=1, keepdims=True) + RMS_EPS)
        o = o * subg_ref[...] * (1.0 - lam_init)
        o_ref[:, cols] = (o * g_ref[:, cols]).astype(o_ref.dtype)
        return bad

    refs = []
    for j in range(hp):
        m, acc = own_block(j)
        refs.append(m)
        acc_ref[j] = acc

    n_full = (qi * tq) // tk
    rest = qi * tq - n_full * tk

    def fast_pair(kp, carry):
        accumulate([(2 * kp, tk, False), (2 * kp + 1, tk, False)])
        return carry

    lax.fori_loop(0, n_full // 2, fast_pair, 0)

    @pl.when(n_full % 2 == 1)
    def _():
        accumulate([(n_full - 1, tk, False)])

    @pl.when(jnp.logical_and(rest > 0, rest <= tk // 2))
    def _():
        accumulate([(2 * n_full, tk // 2, True)])

    @pl.when(rest > tk // 2)
    def _():
        accumulate([(n_full, tk, True)])

    overflow = jnp.float32(0.0)
    for j in range(hp):
        overflow = jnp.maximum(overflow, finish(j, acc_ref[j]))

    @pl.when(overflow > 0.0)
    def _():
        def exact_step(kb, carry):
            return tuple(exact_block(kb, j, *carry[j]) for j in range(hp))
        carry = lax.fori_loop(0, qi, exact_step, tuple(own_block(j) for j in range(hp)))
        for j in range(hp):
            finish(j, carry[j][1])


def _attention(qkv, gates, lam_params, subln_g, lam_init):
    s = qkv.shape[0]
    heads = qkv.shape[1] // (3 * DIFF_V_DIM)
    tq, tk, hp = ATTN_TQ, ATTN_TK, ATTN_HP
    groups = heads // hp
    bw = hp * DIFF_V_DIM
    kern = functools.partial(_attn_kernel, tq=tq, tk=tk, hp=hp, lam_init=lam_init)
    return pl.pallas_call(
        kern,
        grid=(groups, s // tq),
        in_specs=[
            pl.BlockSpec((4, DIFF_HEAD_DIM), lambda h, i: (0, 0)),
            pl.BlockSpec((1, DIFF_V_DIM), lambda h, i: (0, 0)),
            pl.BlockSpec((tq, bw), lambda h, i: (i, h)),
            pl.BlockSpec((s, bw), lambda h, i: (0, groups + h)),
            pl.BlockSpec((s, bw), lambda h, i: (0, 2 * groups + h)),
            pl.BlockSpec((tq, bw), lambda h, i: (i, h)),
        ],
        out_specs=pl.BlockSpec((tq, bw), lambda h, i: (i, h)),
        out_shape=jax.ShapeDtypeStruct((s, heads * DIFF_V_DIM), BF16),
        scratch_shapes=[pltpu.VMEM((hp, 2 * tq, 2 * DIFF_V_DIM), F32)],
        compiler_params=pltpu.CompilerParams(
            dimension_semantics=("parallel", "parallel"), vmem_limit_bytes=VMEM_LIMIT),
        name="diff_attention",
    )(lam_params, subln_g, qkv, qkv, qkv, gates)


def _ssd_kernel(zs_ref, xs_ref, bc_ref, dt_ref, dtb_ref, alog_ref, dskip_ref, ng_ref, y_ref,
                state_ref, *, rows, chunks):
    c = pl.program_id(0)
    width = xs_ref.shape[1]
    gw = width // SSM_GROUPS
    heads_per_pair = LANES // SSM_HEAD_DIM

    @pl.when(c == 0)
    def _():
        state_ref[...] = jnp.zeros_like(state_ref)

    ri = lax.broadcasted_iota(jnp.int32, (rows, rows), 0)
    ci = lax.broadcasted_iota(jnp.int32, (rows, rows), 1)
    causal = ci <= ri
    tri = jnp.where(causal, 1.0, 0.0).astype(BF16)
    er = lax.broadcasted_iota(jnp.int32, (LANES, width), 0)
    ec = lax.broadcasted_iota(jnp.int32, (LANES, width), 1)
    expand = jnp.where((ec >> HEAD_SHIFT) == er, 1.0, 0.0).astype(BF16)
    lane = lax.broadcasted_iota(jnp.int32, (rows, LANES), 1)

    def prepare(rs):
        xs = xs_ref[rs, :]
        dtc = _softplus(dt_ref[rs, :] + dtb_ref[...])
        adt = -jnp.exp(alog_ref[...]) * dtc
        csc = sum(_dot(tri, part) for part in _split3(adt))
        dt_e = sum(_dot(part, expand) for part in _split3(dtc))
        cs_e = sum(_dot(part, expand) for part in _split3(csc))
        cs_last = cs_e[rows - 1:rows, :]
        xdt = xs * dt_e
        bcv = bc_ref[rs, :]
        b_f = [bcv[:, g * SSM_STATE:(g + 1) * SSM_STATE] for g in range(SSM_GROUPS)]
        c_b = [bcv[:, (SSM_GROUPS + g) * SSM_STATE:(SSM_GROUPS + g + 1) * SSM_STATE].astype(BF16)
               for g in range(SSM_GROUPS)]
        scores = [_dot_nt(c_b[g], b_f[g].astype(BF16)) for g in range(SSM_GROUPS)]
        return dict(xs=xs, cs_e=cs_e, cs_t=csc.T, xdt_b=xdt.astype(BF16), b_f=b_f, c_b=c_b,
                    scores=scores,
                    xd_b=(xdt * jnp.exp(cs_last - cs_e)).astype(BF16), ecs=jnp.exp(cs_e),
                    chunk_decay=jnp.exp(cs_last))

    def scan(rs, p):
        for g in range(SSM_GROUPS):
            gsl = slice(g * gw, (g + 1) * gw)
            b_f, c_b, scores = p["b_f"][g], p["c_b"][g], p["scores"][g]
            st = state_ref[g]
            y_off = _dot(c_b, st.astype(BF16)) * p["ecs"][:, gsl]
            state_ref[g] = (st * p["chunk_decay"][:, gsl]
                            + _dot(b_f.T.astype(BF16), p["xd_b"][:, gsl]))

            y_diag = []
            for pair in range(gw // LANES):
                col0 = g * gw + pair * LANES
                xpair = p["xdt_b"][:, col0:col0 + LANES]
                parts = []
                for hh in range(heads_per_pair):
                    h = col0 // SSM_HEAD_DIM + hh
                    seg = (p["cs_e"][:, h * SSM_HEAD_DIM:h * SSM_HEAD_DIM + 1]
                           - p["cs_t"][h:h + 1, :])
                    decay = jnp.exp(jnp.where(causal, seg, NEG_BIG))
                    parts.append(_dot((scores * decay).astype(BF16), xpair))
                y_diag.append(jnp.where(lane < SSM_HEAD_DIM, parts[0], parts[1]))
            y = jnp.concatenate(y_diag, axis=1) + y_off + dskip_ref[:, gsl] * p["xs"][:, gsl]
            y = y * zs_ref[rs, gsl]
            y = y * lax.rsqrt(jnp.mean(y * y, axis=1, keepdims=True) + RMS_EPS)
            y_ref[rs, gsl] = (y * ng_ref[:, gsl]).astype(y_ref.dtype)

    spans = [slice(k * rows, (k + 1) * rows) for k in range(chunks)]
    prepared = [prepare(rs) for rs in spans]
    for rs, p in zip(spans, prepared):
        scan(rs, p)


def _ssd(gates, xs, bc, dt_raw, dt_bias, a_log, d_skip, norm_g, width):
    s = xs.shape[0]
    rows = SSD_L * SSD_CHUNKS_PER_STEP
    bcw = 2 * SSM_GROUPS * SSM_STATE
    heads = width // SSM_HEAD_DIM

    def pad_heads(p):
        return jnp.pad(p.astype(F32), (0, LANES - heads)).reshape(1, LANES)

    def per_channel(p):
        return jnp.repeat(p.astype(F32), SSM_HEAD_DIM).reshape(1, width)

    const = lambda c: (0, 0)
    kern = functools.partial(_ssd_kernel, rows=SSD_L, chunks=SSD_CHUNKS_PER_STEP)
    return pl.pallas_call(
        kern,
        grid=(s // rows,),
        in_specs=[
            pl.BlockSpec((rows, width), lambda c: (c, 1)),
            pl.BlockSpec((rows, width), lambda c: (c, 0)),
            pl.BlockSpec((rows, bcw), lambda c: (c, 0)),
            pl.BlockSpec((rows, LANES), lambda c: (c, 0)),
            pl.BlockSpec((1, LANES), const),
            pl.BlockSpec((1, LANES), const),
            pl.BlockSpec((1, width), const),
            pl.BlockSpec((1, width), const),
        ],
        out_specs=pl.BlockSpec((rows, width), lambda c: (c, 0)),
        out_shape=jax.ShapeDtypeStruct((s, width), BF16),
        scratch_shapes=[pltpu.VMEM((SSM_GROUPS, SSM_STATE, width // SSM_GROUPS), F32)],
        compiler_params=pltpu.CompilerParams(
            dimension_semantics=("arbitrary",), vmem_limit_bytes=VMEM_LIMIT),
        name="ssd_scan",
    )(gates, xs, bc, dt_raw,
      pad_heads(dt_bias), pad_heads(a_log), per_channel(d_skip), norm_g.reshape(1, width))


def _pool_kernel(v_ref, x_ref, wgate_ref, wg_ref, bg_ref, sc_ref, y_ref, tail_ref, ext_ref,
                 wb_ref, pa_ref, pb_ref, wgb_ref, *, rows):
    i = pl.program_id(0)
    gc = wg_ref.shape[1]
    top = POOL_HALO + rows

    @pl.when(i == 0)
    def _():
        tail_ref[...] = jnp.zeros_like(tail_ref)
        wb_ref[...] = wg_ref[...].astype(BF16)
        wgb_ref[...] = wgate_ref[...].astype(BF16)
        pa_ref[0:SUBLANES, :] = jnp.zeros((SUBLANES, gc), F32)
        pb_ref[0:SUBLANES, :] = jnp.zeros((SUBLANES, gc), F32)

    ext_ref[0:POOL_HALO, :] = tail_ref[...]
    ext_ref[POOL_HALO:top, :] = v_ref[...]
    tail_ref[...] = v_ref[rows - POOL_HALO:rows, :]
    pos = (i * rows + lax.broadcasted_iota(jnp.int32, (rows, 1), 0) + 1).astype(F32)

    for gi, w in enumerate(POOL_WINDOWS):
        cols = slice(gi * gc, (gi + 1) * gc)
        v = v_ref[:, cols]
        gate = _silu(_dot(x_ref[...], wgb_ref[:, cols]))
        src, span = ext_ref.at[:, cols], 1
        for dst in (pa_ref, pb_ref, pa_ref, pb_ref):
            if span == w:
                break
            dst[SUBLANES:top, :] = (src[SUBLANES:top, :]
                                    + src[SUBLANES - span:top - span, :])
            src, span = dst, 2 * span
        acc = src[POOL_HALO:top, :]
        pooled = acc * (1.0 / jnp.minimum(pos, float(w))) - v
        m = _dot(pooled.astype(BF16), wb_ref[gi]) + bg_ref[:, cols]
        y_ref[:, cols] = (m * sc_ref[:, cols] * gate).astype(y_ref.dtype)


def _pool(v, x, w_in, w_grp, layer, b_grp, scale):
    s, width = v.shape
    k = x.shape[1]
    rows = POOL_TM
    _, ng, gc, _ = w_grp.shape
    kern = functools.partial(_pool_kernel, rows=rows)
    return pl.pallas_call(
        kern,
        grid=(s // rows,),
        in_specs=[
            pl.BlockSpec((rows, width), lambda i: (i, 0)),
            pl.BlockSpec((rows, k), lambda i: (i, 0)),
            pl.BlockSpec((None, k, width), lambda i: (layer, 0, 1),
                         pipeline_mode=pl.Buffered(1)),
            pl.BlockSpec((None, ng, gc, gc), lambda i: (layer, 0, 0, 0),
                         pipeline_mode=pl.Buffered(1)),
            pl.BlockSpec((1, width), lambda i: (0, 0)),
            pl.BlockSpec((1, width), lambda i: (0, 0)),
        ],
        out_specs=pl.BlockSpec((rows, width), lambda i: (i, 0)),
        out_shape=jax.ShapeDtypeStruct((s, width), BF16),
        scratch_shapes=[pltpu.VMEM((POOL_HALO, width), F32),
                        pltpu.VMEM((POOL_HALO + rows, width), F32),
                        pltpu.VMEM((ng, gc, gc), BF16),
                        pltpu.VMEM((POOL_HALO + rows, gc), F32),
                        pltpu.VMEM((POOL_HALO + rows, gc), F32),
                        pltpu.VMEM((k, width), BF16)],
        compiler_params=pltpu.CompilerParams(
            dimension_semantics=("arbitrary",), vmem_limit_bytes=VMEM_LIMIT),
        name="pool_mixer",
    )(v, x, w_in, w_grp, b_grp.reshape(1, width), scale.reshape(1, width))


def _out_ln_kernel(ya_ref, yb_ref, w_ref, x_ref, g_ref, b_ref, xo_ref, xob_ref, wb_ref):
    half = ya_ref.shape[1]

    @pl.when(pl.program_id(0) == 0)
    def _():
        wb_ref[...] = w_ref[...].astype(BF16)

    d_model = x_ref.shape[1]
    cw = d_model // OUT_CHUNKS
    chunks = [slice(c * cw, (c + 1) * cw) for c in range(OUT_CHUNKS)]
    hs = []
    for cs in chunks:
        y = _dot(ya_ref[...], wb_ref[:half, cs]) + _dot(yb_ref[...], wb_ref[half:, cs])
        hs.append(DEEPNORM_ALPHA * x_ref[:, cs] + y)
    mu = sum(jnp.sum(h, axis=1, keepdims=True) for h in hs) * (1.0 / d_model)
    ds = [h - mu for h in hs]
    var = sum(jnp.sum(d * d, axis=1, keepdims=True) for d in ds) * (1.0 / d_model)
    rstd = lax.rsqrt(var + LN_EPS)
    for cs, d in zip(chunks, ds):
        out = d * rstd * g_ref[:, cs] + b_ref[:, cs]
        xo_ref[:, cs] = out
        xob_ref[:, cs] = out.astype(BF16)


def _out_ln(ya, yb, ya_blk, yb_blk, w_out, layer, x, ln_g, ln_b):
    s, d = x.shape
    half = w_out.shape[1] // 2
    tm = OUT_TM
    return pl.pallas_call(
        _out_ln_kernel,
        grid=(s // tm,),
        in_specs=[
            pl.BlockSpec((tm, half), lambda i: (i, ya_blk)),
            pl.BlockSpec((tm, half), lambda i: (i, yb_blk)),
            pl.BlockSpec((None, 2 * half, d), lambda i: (layer, 0, 0),
                         pipeline_mode=pl.Buffered(1)),
            pl.BlockSpec((tm, d), lambda i: (i, 0)),
            pl.BlockSpec((1, d), lambda i: (0, 0)),
            pl.BlockSpec((1, d), lambda i: (0, 0)),
        ],
        out_specs=[pl.BlockSpec((tm, d), lambda i: (i, 0)),
                   pl.BlockSpec((tm, d), lambda i: (i, 0))],
        out_shape=[jax.ShapeDtypeStruct((s, d), F32), jax.ShapeDtypeStruct((s, d), BF16)],
        scratch_shapes=[pltpu.VMEM((2 * half, d), BF16)],
        compiler_params=pltpu.CompilerParams(
            dimension_semantics=("arbitrary",), vmem_limit_bytes=VMEM_LIMIT),
        name="out_proj_layernorm",
    )(ya, yb, w_out, x, ln_g.reshape(1, d), ln_b.reshape(1, d))


def _even_layer(x, xb, i, w_in, conv_w, conv_b, dt_bias, a_log, d_skip, ssm_norm_g,
                lq1, lk1, lq2, lk2, subln_g, w_out, ln_g, ln_b, lam_init):
    d = x.shape[1]
    att_w = d // 2
    ssm_w = d // 2
    heads = ssm_w // SSM_HEAD_DIM
    n_qkv = 3 * att_w
    n_gates = att_w + ssm_w
    n_bc = 2 * SSM_GROUPS * SSM_STATE
    c_x = n_qkv + n_gates
    c_bc = c_x + ssm_w
    conv_x = (conv_w[:, :ssm_w], conv_b[:ssm_w].reshape(1, ssm_w))
    conv_bc = (conv_w[:, ssm_w:], conv_b[ssm_w:].reshape(1, n_bc))
    qkv = _matmul(xb, w_in, i, 0, n_qkv, MM_TN, BF16, "even_in_qkv", True)
    gates = _matmul(xb, w_in, i, n_qkv, n_gates, MM_TN, F32, "even_in_gates", True, "silu")
    w_dt = jnp.pad(w_in[i, c_bc + n_bc:, :], ((0, LANES - heads), (0, 0)))
    xs, dt_raw = _xconv_dt(xb, w_in, i, c_x, ssm_w, w_dt, conv_x)
    bc = _matmul(xb, w_in, i, c_bc, n_bc, n_bc, F32, "even_in_bc", True, "conv_silu", conv_bc)
    lam_params = jnp.stack([lq1, lk1, lq2, lk2]).astype(F32)
    y_att = _attention(qkv, gates, lam_params, subln_g.reshape(1, DIFF_V_DIM), lam_init)
    y_ssm = _ssd(gates, xs, bc, dt_raw, dt_bias, a_log, d_skip, ssm_norm_g, ssm_w)
    return _out_ln(y_att, y_ssm, 0, 0, w_out, i, x, ln_g, ln_b)


def _odd_layer(x, xb, i, w_in, w_grp, b_grp, scale, w_out, ln_g, ln_b):
    width = w_in.shape[2] // 2
    v = _matmul(xb, w_in, i, 0, width, MM_TN, F32, "odd_in_v")
    y = _pool(v, xb, w_in, w_grp, i, b_grp, scale)
    return _out_ln(y, y, 0, 1, w_out, i, x, ln_g, ln_b)


def kernel(x, ev_w_in, ev_conv_w, ev_conv_b, ev_dt_bias, ev_a_log, ev_d_skip, ev_ssm_norm_g, ev_lambda_q1, ev_lambda_k1, ev_lambda_q2, ev_lambda_k2, ev_subln_g, ev_w_out, od_w_in, od_w_grp, od_b_grp, od_scale, od_w_out, ln_g, ln_b):
    bsz, s, d = x.shape
    ev_w_in = jnp.swapaxes(ev_w_in, 1, 2)
    outs = []
    for b in range(bsz):
        xf = x[b]
        xb = xf
        for l in range(DEPTH):
            i = l // 2
            if l % 2 == 0:
                lam_init = 0.8 - 0.6 * math.exp(-0.3 * l)
                xf, xb = _even_layer(
                    xf, xb, i, ev_w_in, ev_conv_w[i], ev_conv_b[i], ev_dt_bias[i], ev_a_log[i],
                    ev_d_skip[i], ev_ssm_norm_g[i], ev_lambda_q1[i], ev_lambda_k1[i],
                    ev_lambda_q2[i], ev_lambda_k2[i], ev_subln_g[i], ev_w_out,
                    ln_g[l], ln_b[l], lam_init)
            else:
                xf, xb = _odd_layer(xf, xb, i, od_w_in, od_w_grp, od_b_grp[i], od_scale[i],
                                    od_w_out, ln_g[l], ln_b[l])
        outs.append(xf)
    return jnp.stack(outs)
```
